```python
import math
import jax, jax.numpy as jnp
from jax import lax
import numpy as np

D_MODEL = 1024
BATCH = 2
SEQ = 8192
DEPTH = 2

POOL_GROUPS = 4
POOL_GROUP_DIM = 128
POOL_WINDOWS = (2, 4, 8, 16)
POOL_DIM = POOL_GROUPS * POOL_GROUP_DIM
GLA_HEADS = 4
GLA_DK = 64
GLA_DV = 128
GLA_QK_DIM = GLA_HEADS * GLA_DK
GLA_V_DIM = GLA_HEADS * GLA_DV
GLA_GATE_RANK = 16
GLA_TAU = 16.0
GLA_CHUNK = 64
DIFF_HEADS = 4
DIFF_DK = 64
DIFF_DV = 128
DIFF_QK_DIM = DIFF_HEADS * 2 * DIFF_DK
DIFF_V_DIM = DIFF_HEADS * DIFF_DV
Q_BLOCK = 128
REL_BUCKETS = 32
REL_MAX_DIST = 128
N_BRANCHES = 3
BRANCH_DIM = 512
FFN_DIM = -(-8 * D_MODEL // (3 * 256)) * 256
RMS_EPS = 1e-6

IN_SPLITS = (POOL_DIM, GLA_QK_DIM, GLA_QK_DIM, GLA_V_DIM, GLA_V_DIM, GLA_GATE_RANK,
             DIFF_QK_DIM, DIFF_QK_DIM, DIFF_V_DIM, N_BRANCHES * D_MODEL)
IN_DIM = sum(IN_SPLITS)

kernel_name = "hybrid_pool_gla_diffattn_gated_block"


def rms_norm(x, gain):
    xf = x.astype(jnp.float32)
    y = xf * lax.rsqrt(jnp.mean(xf * xf, axis=-1, keepdims=True) + RMS_EPS)
    return (y * gain.astype(jnp.float32)).astype(x.dtype)


def t5_bucket(rel):
    n = jnp.maximum(rel, 0)
    max_exact = REL_BUCKETS // 2
    nf = jnp.maximum(n, 1).astype(jnp.float32)
    large = max_exact + (jnp.log(nf / max_exact) / math.log(REL_MAX_DIST / max_exact)
                         * (REL_BUCKETS - max_exact)).astype(jnp.int32)
    large = jnp.minimum(large, REL_BUCKETS - 1)
    return jnp.where(n < max_exact, n, large)


def pool_mixer(u, w_group, scale):
    b_, s_, _ = u.shape
    uf = u.astype(jnp.float32).reshape(b_, s_, POOL_GROUPS, POOL_GROUP_DIM)
    cs = jnp.cumsum(uf, axis=1)
    cs = jnp.concatenate([jnp.zeros_like(cs[:, :1]), cs], axis=1)
    t = jnp.arange(s_)
    outs = []
    for g, w in enumerate(POOL_WINDOWS):
        lo = jnp.maximum(t + 1 - w, 0)
        win_sum = cs[:, t + 1, g] - cs[:, lo, g]
        cnt = jnp.minimum(t + 1, w).astype(jnp.float32)[None, :, None]
        outs.append(win_sum / cnt - uf[:, :, g])
    pooled = jnp.stack(outs, axis=2)
    mixed = jnp.einsum('bsgc,gcd->bsgd', pooled, w_group.astype(jnp.float32))
    return (mixed.reshape(b_, s_, POOL_DIM) * scale.astype(jnp.float32)).astype(u.dtype)


def gla_mixer(q, k, v, g_out, z_decay, w_decay, b_decay, norm_gain):
    b_, s_, _ = q.shape
    n_c = s_ // GLA_CHUNK
    f32 = jnp.float32

    def heads(t, d):
        return t.astype(f32).reshape(b_, n_c, GLA_CHUNK, GLA_HEADS, d).transpose(0, 3, 1, 2, 4)

    log_a = jax.nn.log_sigmoid(z_decay.astype(f32) @ w_decay.astype(f32)
                               + b_decay.astype(f32)) / GLA_TAU
    qh = heads(q, GLA_DK) * (GLA_DK ** -0.5)
    kh = heads(k, GLA_DK)
    vh = heads(v, GLA_DV)
    cum = jnp.cumsum(heads(log_a, GLA_DK), axis=3)
    cum_last = cum[:, :, :, -1:]
    q_dec = qh * jnp.exp(cum)
    k_dec = kh * jnp.exp(-cum)
    k_to_end = kh * jnp.exp(cum_last - cum)
    causal = jnp.tril(jnp.ones((GLA_CHUNK, GLA_CHUNK), dtype=bool))
    scores = jnp.where(causal, jnp.einsum('bhnik,bhnjk->bhnij', q_dec, k_dec), 0.0)
    o_intra = jnp.einsum('bhnij,bhnjv->bhniv', scores, vh)
    chunk_decay = jnp.exp(cum_last[:, :, :, 0])
    chunk_kv = jnp.einsum('bhnjk,bhnjv->bhnkv', k_to_end, vh)

    def step(state, inp):
        dec, kv = inp
        return dec[..., None] * state + kv, state

    init = jnp.zeros((b_, GLA_HEADS, GLA_DK, GLA_DV), f32)
    _, s_prev = lax.scan(step, init, (jnp.moveaxis(chunk_decay, 2, 0), jnp.moveaxis(chunk_kv, 2, 0)))
    s_prev = jnp.moveaxis(s_prev, 0, 2)
    o = o_intra + jnp.einsum('bhnik,bhnkv->bhniv', q_dec, s_prev)
    o = o.transpose(0, 2, 3, 1, 4).reshape(b_, s_, GLA_HEADS, GLA_DV)
    o = rms_norm(o, norm_gain).reshape(b_, s_, GLA_V_DIM)
    return (o * jax.nn.silu(g_out.astype(f32))).astype(q.dtype)


def diff_attention(q, k, v, lam, lam_init, norm_gain, rel_bias):
    b_, s_, _ = q.shape
    n_b = s_ // Q_BLOCK
    f32 = jnp.float32
    qh = q.astype(f32).reshape(b_, s_, DIFF_HEADS, 2, DIFF_DK).transpose(0, 2, 3, 1, 4) * (DIFF_DK ** -0.5)
    kh = k.astype(f32).reshape(b_, s_, DIFF_HEADS, 2, DIFF_DK).transpose(0, 2, 3, 1, 4)
    vh = v.astype(f32).reshape(b_, s_, DIFF_HEADS, DIFF_DV).transpose(0, 2, 1, 3)
    lamf = lam.astype(f32)
    lam_full = jnp.exp(jnp.sum(lamf[0] * lamf[1])) - jnp.exp(jnp.sum(lamf[2] * lamf[3])) + lam_init
    q_blocks = qh.reshape(b_, DIFF_HEADS, 2, n_b, Q_BLOCK, DIFF_DK).transpose(3, 0, 1, 2, 4, 5)
    k_pos = jnp.arange(s_)
    table = rel_bias.astype(f32)

    def block(args):
        qb, bi = args
        rel = (bi * Q_BLOCK + jnp.arange(Q_BLOCK))[:, None] - k_pos[None, :]
        bias = table[t5_bucket(rel)].transpose(2, 0, 1)[None, :, None]
        logits = jnp.einsum('bhmqd,bhmkd->bhmqk', qb, kh) + bias
        logits = jnp.where(rel >= 0, logits, -jnp.inf)
        p = jax.nn.softmax(logits, axis=-1)
        a = p[:, :, 0] - lam_full * p[:, :, 1]
        return jnp.einsum('bhqk,bhkv->bhqv', a, vh)

    out = lax.map(block, (q_blocks, jnp.arange(n_b)))
    out = rms_norm(out, norm_gain) * (1.0 - lam_init)
    return out.transpose(1, 0, 3, 2, 4).reshape(b_, s_, DIFF_V_DIM).astype(q.dtype)


def setup_inputs(seed: int = 0) -> dict:
    key = jax.random.key(seed)
    ks = jax.random.split(key, 20)
    nrm = jax.random.normal
    f32 = jnp.float32

    def gain(k_, shape):
        return 1.0 + 0.02 * nrm(k_, shape, f32)

    return {
        "x": nrm(ks[0], (BATCH, SEQ, D_MODEL), f32),
        "rel_bias": 0.5 * nrm(ks[1], (REL_BUCKETS, DIFF_HEADS), f32),
        "ln_mix_pre": gain(ks[2], (DEPTH, D_MODEL)),
        "w_in": nrm(ks[3], (DEPTH, D_MODEL, IN_DIM), f32) * D_MODEL ** -0.5,
        "pool_w": nrm(ks[4], (DEPTH, POOL_GROUPS, POOL_GROUP_DIM, POOL_GROUP_DIM), f32) * POOL_GROUP_DIM ** -0.5,
        "pool_scale": gain(ks[5], (DEPTH, POOL_DIM)),
        "gla_w_decay": nrm(ks[6], (DEPTH, GLA_GATE_RANK, GLA_QK_DIM), f32) * GLA_GATE_RANK ** -0.5,
        "gla_b_decay": 0.1 * nrm(ks[7], (DEPTH, GLA_QK_DIM), f32),
        "gla_norm": gain(ks[8], (DEPTH, GLA_DV)),
        "diff_lambda": 0.1 * nrm(ks[9], (DEPTH, 4, DIFF_DK), f32),
        "diff_norm": gain(ks[10], (DEPTH, DIFF_DV)),
        "w_branch": nrm(ks[11], (DEPTH, N_BRANCHES, BRANCH_DIM, D_MODEL), f32) * BRANCH_DIM ** -0.5,
        "w_o": nrm(ks[12], (DEPTH, D_MODEL, D_MODEL), f32) * D_MODEL ** -0.5,
        "ln_mix_post": gain(ks[13], (DEPTH, D_MODEL)),
        "ln_ffn_pre": gain(ks[14], (DEPTH, D_MODEL)),
        "ffn_w_gate": nrm(ks[15], (DEPTH, D_MODEL, FFN_DIM), f32) * D_MODEL ** -0.5,
        "ffn_w_up": nrm(ks[16], (DEPTH, D_MODEL, FFN_DIM), f32) * D_MODEL ** -0.5,
        "ffn_w_down": nrm(ks[17], (DEPTH, FFN_DIM, D_MODEL), f32) * FFN_DIM ** -0.5,
        "ln_ffn_post": gain(ks[18], (DEPTH, D_MODEL)),
    }


def reference(x, rel_bias, ln_mix_pre, w_in, pool_w, pool_scale, gla_w_decay, gla_b_decay, gla_norm,
              diff_lambda, diff_norm, w_branch, w_o, ln_mix_post, ln_ffn_pre, ffn_w_gate, ffn_w_up,
              ffn_w_down, ln_ffn_post):
    b_, s_, _ = x.shape
    split_points = [int(p) for p in np.cumsum(IN_SPLITS)[:-1]]
    for l in range(DEPTH):
        lam_init = 0.8 - 0.6 * math.exp(-0.3 * l)
        h = rms_norm(x, ln_mix_pre[l])
        proj = h @ w_in[l]
        (u_pool, g_q, g_k, g_v, g_o, g_z, d_q, d_k, d_v, gates) = jnp.split(proj, split_points, axis=-1)
        y_pool = pool_mixer(u_pool, pool_w[l], pool_scale[l])
        y_gla = gla_mixer(g_q, g_k, g_v, g_o, g_z, gla_w_decay[l], gla_b_decay[l], gla_norm[l])
        y_diff = diff_attention(d_q, d_k, d_v, diff_lambda[l], lam_init, diff_norm[l], rel_bias)
        branches = jnp.stack([y_pool, y_gla, y_diff], axis=2)
        up = jnp.einsum('bsnc,ncd->bsnd', branches, w_branch[l])
        gate = jax.nn.sigmoid(gates.reshape(b_, s_, N_BRANCHES, D_MODEL))
        merged = jnp.sum(gate * up, axis=2)
        x = x + rms_norm(merged @ w_o[l], ln_mix_post[l])
        h = rms_norm(x, ln_ffn_pre[l])
        f = (jax.nn.silu(h @ ffn_w_gate[l]) * (h @ ffn_w_up[l])) @ ffn_w_down[l]
        x = x + rms_norm(f, ln_ffn_post[l])
    return x
```

```python
import functools
import math

import numpy as np
import jax
import jax.numpy as jnp
from jax import lax
from jax.experimental import pallas as pl
from jax.experimental.pallas import tpu as pltpu

F32 = jnp.float32
BF16 = jnp.bfloat16

D_MODEL = 1024
POOL_WINDOWS = (2, 4, 8, 16)
POOL_GROUP_DIM = 128
POOL_DIM = 512
GLA_HEADS = 4
GLA_DK = 64
GLA_DV = 128
GLA_QK_DIM = 256
GLA_V_DIM = 512
GLA_GATE_RANK = 16
GLA_TAU = 16.0
GLA_CHUNK = 64
DIFF_HEADS = 4
DIFF_DK = 64
DIFF_DV = 128
REL_BUCKETS = 32
REL_MAX_DIST = 128
BRANCH_DIM = 512
FFN_DIM = 2816
RMS_EPS = 1e-6
IN_SPLITS = (512, 256, 256, 512, 512, 16, 512, 512, 512, 3072)

LANES = 128
VMEM_LIMIT = 56 * 1024 * 1024

C_GATE = 0
C_U = 3072
C_GQ = 3584
C_GK = 3840
C_GV = 4096
C_GO = 4608
C_DQ = 5120
C_DK = 5632
C_DV = 6144
C_Z = 6656
Z_PAD = LANES
PROJ_DIM = C_Z + Z_PAD

TM_PROJ = 512
TM_MERGE = 256
TM_FFN = 512
TC_GLA = 512
T_ATT = 256
POOL_HALO = 16


def _rms(x, gain):
    ms = jnp.mean(x * x, axis=-1, keepdims=True)
    return x * lax.rsqrt(ms + RMS_EPS) * gain


def _nt_dot(a, b):
    return lax.dot_general(a, b, (((1,), (1,)), ((), ())), preferred_element_type=F32)


def _tn_dot(a, b):
    return lax.dot_general(a, b, (((0,), (0,)), ((), ())), preferred_element_type=F32)


def _const_spec(shape):
    nd = len(shape)
    return pl.BlockSpec(shape, lambda *_: (0,) * nd, pipeline_mode=pl.Buffered(1))


def _in_proj_kernel(x_ref, g_ref, w_ref, o_ref):
    h = _rms(x_ref[...], g_ref[...]).astype(BF16)
    step = 1024
    for c in range(0, PROJ_DIM, step):
        w = min(step, PROJ_DIM - c)
        o_ref[:, c:c + w] = jnp.dot(h, w_ref[:, c:c + w], preferred_element_type=F32).astype(BF16)


def _in_proj(x2, gain, w):
    n = x2.shape[0]
    return pl.pallas_call(
        _in_proj_kernel,
        out_shape=jax.ShapeDtypeStruct((n, PROJ_DIM), BF16),
        grid=(n // TM_PROJ,),
        in_specs=[
            pl.BlockSpec((TM_PROJ, D_MODEL), lambda i: (i, 0)),
            _const_spec((1, D_MODEL)),
            _const_spec((D_MODEL, PROJ_DIM)),
        ],
        out_specs=pl.BlockSpec((TM_PROJ, PROJ_DIM), lambda i: (i, 0)),
        compiler_params=pltpu.CompilerParams(
            dimension_semantics=("parallel",), vmem_limit_bytes=VMEM_LIMIT),
        name="in_proj",
    )(x2, gain, w)


def _log_sigmoid(x):
    return jnp.minimum(x, 0.0) - jnp.log1p(jnp.exp(-jnp.abs(x)))


def _gla_kernel(q_ref, k_ref, v_ref, go_ref, z_ref, wd_ref, bd_ref, gn_ref, y_ref, st_ref, la_ref):
    @pl.when(pl.program_id(1) == 0)
    def _():
        st_ref[...] = jnp.zeros_like(st_ref)

    zl = jnp.dot(z_ref[...], wd_ref[...], preferred_element_type=F32) + bd_ref[...]
    la_ref[...] = _log_sigmoid(zl) / GLA_TAU

    row = lax.broadcasted_iota(jnp.int32, (GLA_CHUNK, GLA_CHUNK), 0)
    col = lax.broadcasted_iota(jnp.int32, (GLA_CHUNK, GLA_CHUNK), 1)
    tril = row >= col
    tril_bf = jnp.where(tril, 1.0, 0.0).astype(BF16)
    lane = lax.broadcasted_iota(jnp.int32, (1, LANES), 1)
    head_mask = (jnp.where(lane < GLA_DK, 1.0, 0.0), jnp.where(lane >= GLA_DK, 1.0, 0.0))
    gn = gn_ref[...]

    def chunk(c, carry):
        r = pl.ds(pl.multiple_of(c * GLA_CHUNK, GLA_CHUNK), GLA_CHUNK)
        la = la_ref[r, :]
        hi = la.astype(BF16)
        r1 = la - hi.astype(F32)
        mid = r1.astype(BF16)
        lo = (r1 - mid.astype(F32)).astype(BF16)
        cs = jnp.dot(tril_bf, jnp.concatenate([hi, mid, lo], axis=1), preferred_element_type=F32)
        cum = cs[:, :GLA_QK_DIM] + cs[:, GLA_QK_DIM:2 * GLA_QK_DIM] + cs[:, 2 * GLA_QK_DIM:]
        cum_last = cum[GLA_CHUNK - 1:GLA_CHUNK, :]
        q = q_ref[r, :].astype(F32)
        k = k_ref[r, :].astype(F32)
        q_dec = (q * (GLA_DK ** -0.5) * jnp.exp(cum)).astype(BF16)
        k_dec = k * jnp.exp(-cum)
        k_end = k * jnp.exp(cum_last - cum)
        decay = jnp.exp(cum_last)
        for h in range(GLA_HEADS):
            pair = slice((h // 2) * LANES, (h // 2 + 1) * LANES)
            vcols = slice(h * GLA_DV, (h + 1) * GLA_DV)
            hm = head_mask[h % 2]
            q_p = q_dec[:, pair]
            k_dec_h = (k_dec[:, pair] * hm).astype(BF16)
            k_end_h = (k_end[:, pair] * hm).astype(BF16)
            v_h = v_ref[r, vcols]
            scores = jnp.where(tril, _nt_dot(q_p, k_dec_h), 0.0).astype(BF16)
            st = st_ref[h]
            o = jnp.dot(scores, v_h, preferred_element_type=F32) + _nt_dot(q_p, st.astype(BF16))
            st_ref[h] = st * decay[:, pair] + _tn_dot(v_h, k_end_h)
            g = go_ref[r, vcols].astype(F32)
            y_ref[r, vcols] = (_rms(o, gn) * (g * jax.nn.sigmoid(g))).astype(BF16)
        return carry

    lax.fori_loop(0, TC_GLA // GLA_CHUNK, chunk, 0)


def _gla(proj, w_decay, b_decay, norm_gain, batch, seq):
    n = proj.shape[0]
    tps = seq // TC_GLA
    row = lambda b, t: b * tps + t
    return pl.pallas_call(
        _gla_kernel,
        out_shape=jax.ShapeDtypeStruct((n, GLA_V_DIM), BF16),
        grid=(batch, tps),
        in_specs=[
            pl.BlockSpec((TC_GLA, GLA_QK_DIM), lambda b, t: (row(b, t), C_GQ // GLA_QK_DIM)),
            pl.BlockSpec((TC_GLA, GLA_QK_DIM), lambda b, t: (row(b, t), C_GK // GLA_QK_DIM)),
            pl.BlockSpec((TC_GLA, GLA_V_DIM), lambda b, t: (row(b, t), C_GV // GLA_V_DIM)),
            pl.BlockSpec((TC_GLA, GLA_V_DIM), lambda b, t: (row(b, t), C_GO // GLA_V_DIM)),
            pl.BlockSpec((TC_GLA, Z_PAD), lambda b, t: (row(b, t), C_Z // Z_PAD)),
            _const_spec((Z_PAD, GLA_QK_DIM)),
            _const_spec((1, GLA_QK_DIM)),
            _const_spec((1, GLA_DV)),
        ],
        out_specs=pl.BlockSpec((TC_GLA, GLA_V_DIM), lambda b, t: (row(b, t), 0)),
        scratch_shapes=[
            pltpu.VMEM((GLA_HEADS, GLA_DV, LANES), F32),
            pltpu.VMEM((TC_GLA, GLA_QK_DIM), F32),
        ],
        compiler_params=pltpu.CompilerParams(
            dimension_semantics=("parallel", "arbitrary"), vmem_limit_bytes=VMEM_LIMIT),
        name="gla",
    )(proj, proj, proj, proj, proj, w_decay, b_decay, norm_gain)


def _diff_kernel(lam_ref, q_ref, k_ref, v_ref, bias_ref, gn_ref, y_ref,
                 qs_ref, m_ref, l_ref, acc_ref, *, lam_init):
    i = pl.program_id(2)
    t = T_ATT
    lane = lax.broadcasted_iota(jnp.int32, (1, LANES), 1)
    q = q_ref[...].astype(F32) * (DIFF_DK ** -0.5)
    qs_ref[0:t, :] = jnp.where(lane < DIFF_DK, q, 0.0).astype(BF16)
    qs_ref[t:2 * t, :] = jnp.where(lane >= DIFF_DK, q, 0.0).astype(BF16)
    m_ref[...] = jnp.full_like(m_ref, -1e30)
    l_ref[...] = jnp.zeros_like(l_ref)
    acc_ref[...] = jnp.zeros_like(acc_ref)

    def step(j, bias):
        r = pl.ds(pl.multiple_of(j * t, t), t)
        s = _nt_dot(k_ref[r, :], qs_ref[...])
        if bias is not None:
            s = s + jnp.concatenate([bias, bias], axis=1)
        m_old = m_ref[...]
        m_new = jnp.maximum(m_old, jnp.max(s, axis=0, keepdims=True))
        alpha = jnp.exp(m_old - m_new)
        p = jnp.exp(s - m_new)
        l_ref[...] = alpha * l_ref[...] + jnp.sum(p, axis=0, keepdims=True)
        acc_ref[...] = alpha * acc_ref[...] + _tn_dot(v_ref[r, :], p.astype(BF16))
        m_ref[...] = m_new

    def far_step(j, carry):
        step(j, None)
        return carry

    lax.fori_loop(0, jnp.maximum(i - 1, 0), far_step, 0)

    @pl.when(i >= 1)
    def _():
        step(i - 1, bias_ref[1])

    step(i, bias_ref[0])

    lam_p = lam_ref[...]
    lam = (jnp.exp(jnp.sum(lam_p[0:1] * lam_p[1:2], axis=1, keepdims=True))
           - jnp.exp(jnp.sum(lam_p[2:3] * lam_p[3:4], axis=1, keepdims=True)) + lam_init)
    acc = acc_ref[...]
    l = l_ref[...]
    o = acc[:, :t] / l[:, :t] - lam * (acc[:, t:] / l[:, t:])
    ms = jnp.mean(o * o, axis=0, keepdims=True)
    o = o * lax.rsqrt(ms + RMS_EPS) * gn_ref[...] * (1.0 - lam_init)
    y_ref[...] = o.T.astype(BF16)


def _diff(proj, lam, bias, norm_gain_col, lam_init, batch, seq):
    n = proj.shape[0]
    nq = seq // T_ATT
    return pl.pallas_call(
        functools.partial(_diff_kernel, lam_init=lam_init),
        out_shape=jax.ShapeDtypeStruct((n, DIFF_HEADS * DIFF_DV), BF16),
        grid=(batch, DIFF_HEADS, nq),
        in_specs=[
            _const_spec((4, DIFF_DK)),
            pl.BlockSpec((T_ATT, LANES), lambda b, h, i: (b * nq + i, C_DQ // LANES + h)),
            pl.BlockSpec((seq, LANES), lambda b, h, i: (b, C_DK // LANES + h)),
            pl.BlockSpec((seq, LANES), lambda b, h, i: (b, C_DV // LANES + h)),
            pl.BlockSpec((2, None, T_ATT, T_ATT), lambda b, h, i: (0, h, 0, 0)),
            _const_spec((DIFF_DV, 1)),
        ],
        out_specs=pl.BlockSpec((T_ATT, DIFF_DV), lambda b, h, i: (b * nq + i, h)),
        scratch_shapes=[
            pltpu.VMEM((2 * T_ATT, LANES), BF16),
            pltpu.VMEM((1, 2 * T_ATT), F32),
            pltpu.VMEM((1, 2 * T_ATT), F32),
            pltpu.VMEM((DIFF_DV, 2 * T_ATT), F32),
        ],
        compiler_params=pltpu.CompilerParams(
            dimension_semantics=("parallel", "parallel", "arbitrary"), vmem_limit_bytes=VMEM_LIMIT),
        name="diff_attn",
    )(lam, proj, proj, proj, bias, norm_gain_col)


def _t5_bucket(rel):
    n = jnp.maximum(rel, 0)
    max_exact = REL_BUCKETS // 2
    nf = jnp.maximum(n, 1).astype(F32)
    large = max_exact + (jnp.log(nf / max_exact) / math.log(REL_MAX_DIST / max_exact)
                         * (REL_BUCKETS - max_exact)).astype(jnp.int32)
    large = jnp.minimum(large, REL_BUCKETS - 1)
    return jnp.where(n < max_exact, n, large)


def _bias_tiles(rel_bias):
    assert T_ATT + 1 >= REL_MAX_DIST
    kk = np.arange(T_ATT)[:, None]
    qq = np.arange(T_ATT)[None, :]
    rel = np.stack([qq - kk, T_ATT + qq - kk])
    table = rel_bias.astype(F32)
    vals = table[_t5_bucket(jnp.asarray(rel, jnp.int32))] - table[REL_BUCKETS - 1]
    vals = jnp.where(jnp.asarray(rel >= 0)[..., None], vals, -jnp.inf)
    return vals.transpose(0, 3, 1, 2)


def _merge_kernel(x_ref, g0_ref, g1_ref, g2_ref, u_ref, halo_ref, ygla_ref, ydiff_ref,
                  pw_ref, ps_ref, wb_ref, wo_ref, gpost_ref, o_ref, ucat_ref, *, tiles_per_seq):
    i = pl.program_id(0)
    tm = TM_MERGE
    t_in_seq = i % tiles_per_seq
    halo = halo_ref[...].astype(F32)
    ucat_ref[0:POOL_HALO, :] = jnp.where(t_in_seq == 0, 0.0, halo)
    ucat_ref[POOL_HALO:, :] = u_ref[...].astype(F32)
    pos = t_in_seq * tm + lax.broadcasted_iota(jnp.int32, (tm, 1), 0)
    mixed = []
    for g, w in enumerate(POOL_WINDOWS):
        cols = slice(g * POOL_GROUP_DIM, (g + 1) * POOL_GROUP_DIM)
        cur = ucat_ref[POOL_HALO:POOL_HALO + tm, cols]
        win = cur
        for s in range(1, w):
            win = win + ucat_ref[POOL_HALO - s:POOL_HALO - s + tm, cols]
        cnt = jnp.minimum(pos + 1, w).astype(F32)
        pooled = win / cnt - cur
        mixed.append(jnp.dot(pooled.astype(BF16), pw_ref[g], preferred_element_type=F32))
    y_pool = (jnp.concatenate(mixed, axis=1) * ps_ref[...]).astype(BF16)

    merged = jax.nn.sigmoid(g0_ref[...].astype(F32)) * jnp.dot(y_pool, wb_ref[0], preferred_element_type=F32)
    merged = merged + jax.nn.sigmoid(g1_ref[...].astype(F32)) * jnp.dot(
        ygla_ref[...], wb_ref[1], preferred_element_type=F32)
    merged = merged + jax.nn.sigmoid(g2_ref[...].astype(F32)) * jnp.dot(
        ydiff_ref[...], wb_ref[2], preferred_element_type=F32)
    out = jnp.dot(merged.astype(BF16), wo_ref[...], preferred_element_type=F32)
    o_ref[...] = x_ref[...] + _rms(out, gpost_ref[...])


def _merge(x2, proj, y_gla, y_diff, pool_w, pool_scale, w_branch, w_o, gpost, seq):
    n = x2.shape[0]
    tm = TM_MERGE
    halo_blocks = tm // POOL_HALO
    return pl.pallas_call(
        functools.partial(_merge_kernel, tiles_per_seq=seq // tm),
        out_shape=jax.ShapeDtypeStruct((n, D_MODEL), F32),
        grid=(n // tm,),
        in_specs=[
            pl.BlockSpec((tm, D_MODEL), lambda i: (i, 0)),
            pl.BlockSpec((tm, D_MODEL), lambda i: (i, C_GATE // D_MODEL + 0)),
            pl.BlockSpec((tm, D_MODEL), lambda i: (i, C_GATE // D_MODEL + 1)),
            pl.BlockSpec((tm, D_MODEL), lambda i: (i, C_GATE // D_MODEL + 2)),
            pl.BlockSpec((tm, POOL_DIM), lambda i: (i, C_U // POOL_DIM)),
            pl.BlockSpec((POOL_HALO, POOL_DIM),
                         lambda i: (jnp.maximum(i * halo_blocks - 1, 0), C_U // POOL_DIM)),
            pl.BlockSpec((tm, BRANCH_DIM), lambda i: (i, 0)),
            pl.BlockSpec((tm, BRANCH_DIM), lambda i: (i, 0)),
            _const_spec((len(POOL_WINDOWS), POOL_GROUP_DIM, POOL_GROUP_DIM)),
            _const_spec((1, POOL_DIM)),
            _const_spec((3, BRANCH_DIM, D_MODEL)),
            _const_spec((D_MODEL, D_MODEL)),
            _const_spec((1, D_MODEL)),
        ],
        out_specs=pl.BlockSpec((tm, D_MODEL), lambda i: (i, 0)),
        scratch_shapes=[pltpu.VMEM((tm + POOL_HALO, POOL_DIM), F32)],
        compiler_params=pltpu.CompilerParams(
            dimension_semantics=("parallel",), vmem_limit_bytes=VMEM_LIMIT),
        name="merge",
    )(x2, proj, proj, proj, proj, proj, y_gla, y_diff, pool_w, pool_scale, w_branch, w_o, gpost)


def _ffn_kernel(x_ref, gpre_ref, wg_ref, wu_ref, wd_ref, gpost_ref, o_ref):
    x = x_ref[...]
    h = _rms(x, gpre_ref[...]).astype(BF16)
    g = jnp.dot(h, wg_ref[...], preferred_element_type=F32)
    u = jnp.dot(h, wu_ref[...], preferred_element_type=F32)
    a = (g * jax.nn.sigmoid(g) * u).astype(BF16)
    f = jnp.dot(a, wd_ref[...], preferred_element_type=F32)
    o_ref[...] = x + _rms(f, gpost_ref[...])


def _ffn(x2, gpre, wg, wu, wd, gpost):
    n = x2.shape[0]
    tm = TM_FFN
    return pl.pallas_call(
        _ffn_kernel,
        out_shape=jax.ShapeDtypeStruct((n, D_MODEL), F32),
        grid=(n // tm,),
        in_specs=[
            pl.BlockSpec((tm, D_MODEL), lambda i: (i, 0)),
            _const_spec((1, D_MODEL)),
            _const_spec((D_MODEL, FFN_DIM)),
            _const_spec((D_MODEL, FFN_DIM)),
            _const_spec((FFN_DIM, D_MODEL)),
            _const_spec((1, D_MODEL)),
        ],
        out_specs=pl.BlockSpec((tm, D_MODEL), lambda i: (i, 0)),
        compiler_params=pltpu.CompilerParams(
            dimension_semantics=("parallel",), vmem_limit_bytes=VMEM_LIMIT),
        name="ffn",
    )(x2, gpre, wg, wu, wd, gpost)


def _reorder_w_in(w):
    pts = [int(p) for p in np.cumsum(IN_SPLITS)[:-1]]
    u, gq, gk, gv, go, gz, dq, dk, dv, gates = jnp.split(w, pts, axis=1)
    pad = jnp.zeros((w.shape[0], Z_PAD - GLA_GATE_RANK), w.dtype)
    return jnp.concatenate([gates, u, gq, gk, gv, go, dq, dk, dv, gz, pad], axis=1).astype(BF16)


def kernel(x, rel_bias, ln_mix_pre, w_in, pool_w, pool_scale, gla_w_decay, gla_b_decay, gla_norm,
           diff_lambda, diff_norm, w_branch, w_o, ln_mix_post, ln_ffn_pre, ffn_w_gate, ffn_w_up,
           ffn_w_down, ln_ffn_post):
    batch, seq, d = x.shape
    depth = w_in.shape[0]
    assert d == D_MODEL and seq % TC_GLA == 0 and seq % T_ATT == 0 and seq % TM_MERGE == 0
    x2 = x.reshape(batch * seq, d)
    bias = _bias_tiles(rel_bias)
    for l in range(depth):
        lam_init = 0.8 - 0.6 * math.exp(-0.3 * l)
        proj = _in_proj(x2, ln_mix_pre[l][None, :], _reorder_w_in(w_in[l]))
        w_decay = jnp.pad(gla_w_decay[l], ((0, Z_PAD - GLA_GATE_RANK), (0, 0))).astype(BF16)
        y_gla = _gla(proj, w_decay, gla_b_decay[l][None, :], gla_norm[l][None, :], batch, seq)
        y_diff = _diff(proj, diff_lambda[l], bias, diff_norm[l][:, None], lam_init, batch, seq)
        x2 = _merge(x2, proj, y_gla, y_diff, pool_w[l].astype(BF16), pool_scale[l][None, :],
                    w_branch[l].astype(BF16), w_o[l].astype(BF16), ln_mix_post[l][None, :], seq)
        x2 = _ffn(x2, ln_ffn_pre[l][None, :], ffn_w_gate[l].astype(BF16), ffn_w_up[l].astype(BF16),
                  ffn_w_down[l].astype(BF16), ln_ffn_post[l][None, :])
    return x2.reshape(batch, seq, d)
```

```python
import functools
import math

import numpy as np
import jax
import jax.numpy as jnp
from jax import lax
from jax.experimental import pallas as pl
from jax.experimental.pallas import tpu as pltpu

F32 = jnp.float32
BF16 = jnp.bfloat16

D_MODEL = 1024
POOL_WINDOWS = (2, 4, 8, 16)
POOL_GROUP_DIM = 128
POOL_DIM = 512
GLA_HEADS = 4
GLA_DK = 64
GLA_DV = 128
GLA_QK_DIM = 256
GLA_V_DIM = 512
GLA_GATE_RANK = 16
GLA_TAU = 16.0
GLA_CHUNK = 64
DIFF_HEADS = 4
DIFF_DK = 64
DIFF_DV = 128
REL_BUCKETS = 32
REL_MAX_DIST = 128
BRANCH_DIM = 512
FFN_DIM = 2816
RMS_EPS = 1e-6
IN_SPLITS = (512, 256, 256, 512, 512, 16, 512, 512, 512, 3072)

LANES = 128
VMEM_LIMIT = 56 * 1024 * 1024

C_GATE = 0
C_U = 3072
C_GQ = 3584
C_GK = 3840
C_GV = 4096
C_GO = 4608
C_DQ = 5120
C_DK = 5632
C_DV = 6144
C_Z = 6656
Z_PAD = LANES
PROJ_DIM = C_Z + Z_PAD

TM_PROJ = 512
TM_MERGE = 256
TM_FFN = 512
TC_GLA = 512
T_ATT = 256
POOL_HALO = 16


def _rms(x, gain):
    ms = jnp.mean(x * x, axis=-1, keepdims=True)
    return x * lax.rsqrt(ms + RMS_EPS) * gain


def _nt_dot(a, b):
    return lax.dot_general(a, b, (((1,), (1,)), ((), ())), preferred_element_type=F32)


def _tn_dot(a, b):
    return lax.dot_general(a, b, (((0,), (0,)), ((), ())), preferred_element_type=F32)


def _const_spec(shape):
    nd = len(shape)
    return pl.BlockSpec(shape, lambda *_: (0,) * nd, pipeline_mode=pl.Buffered(1))


def _in_proj_kernel(x_ref, g_ref, w_ref, o_ref):
    h = _rms(x_ref[...], g_ref[...]).astype(BF16)
    step = 1024
    for c in range(0, PROJ_DIM, step):
        w = min(step, PROJ_DIM - c)
        o_ref[:, c:c + w] = jnp.dot(h, w_ref[:, c:c + w], preferred_element_type=F32).astype(BF16)


def _in_proj(x2, gain, w):
    n = x2.shape[0]
    return pl.pallas_call(
        _in_proj_kernel,
        out_shape=jax.ShapeDtypeStruct((n, PROJ_DIM), BF16),
        grid=(n // TM_PROJ,),
        in_specs=[
            pl.BlockSpec((TM_PROJ, D_MODEL), lambda i: (i, 0)),
            _const_spec((1, D_MODEL)),
            _const_spec((D_MODEL, PROJ_DIM)),
        ],
        out_specs=pl.BlockSpec((TM_PROJ, PROJ_DIM), lambda i: (i, 0)),
        compiler_params=pltpu.CompilerParams(
            dimension_semantics=("parallel",), vmem_limit_bytes=VMEM_LIMIT),
        name="in_proj",
    )(x2, gain, w)


def _log_sigmoid(x):
    return jnp.minimum(x, 0.0) - jnp.log1p(jnp.exp(-jnp.abs(x)))


def _gla_kernel(q_ref, k_ref, v_ref, go_ref, z_ref, wd_ref, bd_ref, gn_ref, y_ref, st_ref, la_ref):
    @pl.when(pl.program_id(1) == 0)
    def _():
        st_ref[...] = jnp.zeros_like(st_ref)

    zl = jnp.dot(z_ref[...], wd_ref[...], preferred_element_type=F32) + bd_ref[...]
    la_ref[...] = _log_sigmoid(zl) / GLA_TAU

    row = lax.broadcasted_iota(jnp.int32, (GLA_CHUNK, GLA_CHUNK), 0)
    col = lax.broadcasted_iota(jnp.int32, (GLA_CHUNK, GLA_CHUNK), 1)
    tril = row >= col
    tril_bf = jnp.where(tril, 1.0, 0.0).astype(BF16)
    lane = lax.broadcasted_iota(jnp.int32, (1, LANES), 1)
    head_mask = (jnp.where(lane < GLA_DK, 1.0, 0.0), jnp.where(lane >= GLA_DK, 1.0, 0.0))
    gn = gn_ref[...]

    def chunk(c, carry):
        r = pl.ds(pl.multiple_of(c * GLA_CHUNK, GLA_CHUNK), GLA_CHUNK)
        la = la_ref[r, :]
        hi = la.astype(BF16)
        r1 = la - hi.astype(F32)
        mid = r1.astype(BF16)
        lo = (r1 - mid.astype(F32)).astype(BF16)
        cs = jnp.dot(tril_bf, jnp.concatenate([hi, mid, lo], axis=1), preferred_element_type=F32)
        cum = cs[:, :GLA_QK_DIM] + cs[:, GLA_QK_DIM:2 * GLA_QK_DIM] + cs[:, 2 * GLA_QK_DIM:]
        cum_last = cum[GLA_CHUNK - 1:GLA_CHUNK, :]
        q = q_ref[r, :].astype(F32)
        k = k_ref[r, :].astype(F32)
        q_dec = (q * (GLA_DK ** -0.5) * jnp.exp(cum)).astype(BF16)
        k_dec = k * jnp.exp(-cum)
        k_end = k * jnp.exp(cum_last - cum)
        decay = jnp.exp(cum_last)
        for h in range(GLA_HEADS):
            pair = slice((h // 2) * LANES, (h // 2 + 1) * LANES)
            vcols = slice(h * GLA_DV, (h + 1) * GLA_DV)
            hm = head_mask[h % 2]
            q_p = q_dec[:, pair]
            k_dec_h = (k_dec[:, pair] * hm).astype(BF16)
            k_end_h = (k_end[:, pair] * hm).astype(BF16)
            v_h = v_ref[r, vcols]
            scores = jnp.where(tril, _nt_dot(q_p, k_dec_h), 0.0).astype(BF16)
            st = st_ref[h]
            o = jnp.dot(scores, v_h, preferred_element_type=F32) + _nt_dot(q_p, st.astype(BF16))
            st_ref[h] = st * decay[:, pair] + _tn_dot(v_h, k_end_h)
            g = go_ref[r, vcols].astype(F32)
            y_ref[r, vcols] = (_rms(o, gn) * (g * jax.nn.sigmoid(g))).astype(BF16)
        return carry

    lax.fori_loop(0, TC_GLA // GLA_CHUNK, chunk, 0)


def _gla(proj, w_decay, b_decay, norm_gain, batch, seq):
    n = proj.shape[0]
    tps = seq // TC_GLA
    row = lambda b, t: b * tps + t
    return pl.pallas_call(
        _gla_kernel,
        out_shape=jax.ShapeDtypeStruct((n, GLA_V_DIM), BF16),
        grid=(batch, tps),
        in_specs=[
            pl.BlockSpec((TC_GLA, GLA_QK_DIM), lambda b, t: (row(b, t), C_GQ // GLA_QK_DIM)),
            pl.BlockSpec((TC_GLA, GLA_QK_DIM), lambda b, t: (row(b, t), C_GK // GLA_QK_DIM)),
            pl.BlockSpec((TC_GLA, GLA_V_DIM), lambda b, t: (row(b, t), C_GV // GLA_V_DIM)),
            pl.BlockSpec((TC_GLA, GLA_V_DIM), lambda b, t: (row(b, t), C_GO // GLA_V_DIM)),
            pl.BlockSpec((TC_GLA, Z_PAD), lambda b, t: (row(b, t), C_Z // Z_PAD)),
            _const_spec((Z_PAD, GLA_QK_DIM)),
            _const_spec((1, GLA_QK_DIM)),
            _const_spec((1, GLA_DV)),
        ],
        out_specs=pl.BlockSpec((TC_GLA, GLA_V_DIM), lambda b, t: (row(b, t), 0)),
        scratch_shapes=[
            pltpu.VMEM((GLA_HEADS, GLA_DV, LANES), F32),
            pltpu.VMEM((TC_GLA, GLA_QK_DIM), F32),
        ],
        compiler_params=pltpu.CompilerParams(
            dimension_semantics=("parallel", "arbitrary"), vmem_limit_bytes=VMEM_LIMIT),
        name="gla",
    )(proj, proj, proj, proj, proj, w_decay, b_decay, norm_gain)


LOG2E = 1.4426950408889634


def _diff_kernel(lam_ref, q_ref, k_ref, v_ref, bias_ref, gn_ref, y_ref,
                 qs_ref, s0_ref, s1_ref, m_ref, l_ref, acc_ref, *, lam_init):
    i = pl.program_id(1)
    t = T_ATT
    lane = lax.broadcasted_iota(jnp.int32, (1, LANES), 1)
    for h in range(DIFF_HEADS):
        q = q_ref[:, h * LANES:(h + 1) * LANES].astype(F32) * (DIFF_DK ** -0.5 * LOG2E)
        qs_ref[h, 0:t, :] = jnp.where(lane < DIFF_DK, q, 0.0).astype(BF16)
        qs_ref[h, t:2 * t, :] = jnp.where(lane >= DIFF_DK, q, 0.0).astype(BF16)
    m_ref[...] = jnp.full_like(m_ref, -1e30)
    l_ref[...] = jnp.zeros_like(l_ref)
    acc_ref[...] = jnp.zeros_like(acc_ref)

    def rows(j):
        return pl.ds(pl.multiple_of(j * t, t), t)

    def head_cols(h):
        return slice(h * LANES, (h + 1) * LANES)

    def logits(j, s_ref):
        for h in range(DIFF_HEADS):
            s_ref[h] = _nt_dot(k_ref[rows(j), head_cols(h)], qs_ref[h])

    def softmax_pv(j, s_ref, bias_cols):
        scaled = []
        for h in range(DIFF_HEADS):
            s = s_ref[h]
            if bias_cols is not None:
                bias = bias_ref[h, :, bias_cols]
                s = s + jnp.concatenate([bias, bias], axis=1)
            m_old = m_ref[h]
            m_new = jnp.maximum(m_old, jnp.max(s, axis=0, keepdims=True))
            alpha = jnp.exp2(m_old - m_new)
            p = jnp.exp2(s - m_new)
            l_ref[h] = alpha * l_ref[h] + jnp.sum(p, axis=0, keepdims=True)
            m_ref[h] = m_new
            scaled.append((alpha, p.astype(BF16)))
        for h, (alpha, p) in enumerate(scaled):
            acc_ref[h] = alpha * acc_ref[h] + _tn_dot(v_ref[rows(j), head_cols(h)], p)

    n_far = jnp.maximum(i - 1, 0)
    peeled = n_far % 2

    @pl.when(i % 2 == 0)
    def _():
        logits(0, s0_ref)

    @pl.when(i % 2 == 1)
    def _():
        logits(0, s1_ref)

    @pl.when(peeled == 1)
    def _():
        logits(1, s1_ref)
        softmax_pv(0, s0_ref, None)

    def far_pair(jj, carry):
        j = peeled + 2 * jj
        logits(j + 1, s0_ref)
        softmax_pv(j, s1_ref, None)
        logits(j + 2, s1_ref)
        softmax_pv(j + 1, s0_ref, None)
        return carry

    lax.fori_loop(0, n_far // 2, far_pair, 0)

    @pl.when(i >= 1)
    def _():
        logits(i, s0_ref)
        softmax_pv(i - 1, s1_ref, slice(t, 2 * t))

    softmax_pv(i, s0_ref, slice(0, t))

    lam_p = lam_ref[...]
    lam = (jnp.exp(jnp.sum(lam_p[0:1] * lam_p[1:2], axis=1, keepdims=True))
           - jnp.exp(jnp.sum(lam_p[2:3] * lam_p[3:4], axis=1, keepdims=True)) + lam_init)
    for h in range(DIFF_HEADS):
        acc = acc_ref[h]
        l = l_ref[h]
        o = acc[:, :t] / l[:, :t] - lam * (acc[:, t:] / l[:, t:])
        ms = jnp.mean(o * o, axis=0, keepdims=True)
        o = o * lax.rsqrt(ms + RMS_EPS) * gn_ref[...] * (1.0 - lam_init)
        y_ref[:, h * DIFF_DV:(h + 1) * DIFF_DV] = o.T.astype(BF16)


def _diff(proj, lam, bias, norm_gain_col, lam_init, batch, seq):
    n = proj.shape[0]
    nq = seq // T_ATT
    hw = DIFF_HEADS * LANES
    return pl.pallas_call(
        functools.partial(_diff_kernel, lam_init=lam_init),
        out_shape=jax.ShapeDtypeStruct((n, DIFF_HEADS * DIFF_DV), BF16),
        grid=(batch, nq),
        in_specs=[
            _const_spec((4, DIFF_DK)),
            pl.BlockSpec((T_ATT, hw), lambda b, i: (b * nq + i, C_DQ // hw)),
            pl.BlockSpec((seq, hw), lambda b, i: (b, C_DK // hw)),
            pl.BlockSpec((seq, hw), lambda b, i: (b, C_DV // hw)),
            _const_spec((DIFF_HEADS, T_ATT, 2 * T_ATT)),
            _const_spec((DIFF_DV, 1)),
        ],
        out_specs=pl.BlockSpec((T_ATT, DIFF_HEADS * DIFF_DV), lambda b, i: (b * nq + i, 0)),
        scratch_shapes=[
            pltpu.VMEM((DIFF_HEADS, 2 * T_ATT, LANES), BF16),
            pltpu.VMEM((DIFF_HEADS, T_ATT, 2 * T_ATT), F32),
            pltpu.VMEM((DIFF_HEADS, T_ATT, 2 * T_ATT), F32),
            pltpu.VMEM((DIFF_HEADS, 1, 2 * T_ATT), F32),
            pltpu.VMEM((DIFF_HEADS, 1, 2 * T_ATT), F32),
            pltpu.VMEM((DIFF_HEADS, DIFF_DV, 2 * T_ATT), F32),
        ],
        compiler_params=pltpu.CompilerParams(
            dimension_semantics=("parallel", "arbitrary"), vmem_limit_bytes=VMEM_LIMIT),
        name="diff_attn",
    )(lam, proj, proj, proj, bias, norm_gain_col)


def _t5_bucket(rel):
    n = jnp.maximum(rel, 0)
    max_exact = REL_BUCKETS // 2
    nf = jnp.maximum(n, 1).astype(F32)
    large = max_exact + (jnp.log(nf / max_exact) / math.log(REL_MAX_DIST / max_exact)
                         * (REL_BUCKETS - max_exact)).astype(jnp.int32)
    large = jnp.minimum(large, REL_BUCKETS - 1)
    return jnp.where(n < max_exact, n, large)


def _bias_tiles(rel_bias):
    assert T_ATT + 1 >= REL_MAX_DIST
    t = T_ATT
    table = rel_bias.astype(F32)
    near = (table[_t5_bucket(jnp.arange(2 * t, dtype=jnp.int32))] - table[REL_BUCKETS - 1]) * LOG2E
    by_rel = jnp.concatenate([near, jnp.full((t, DIFF_HEADS), -jnp.inf, F32)], axis=0).T
    period = 3 * t
    flat = jnp.tile(by_rel, (1, t))[:, :t * (period - 1)]
    return flat.reshape(DIFF_HEADS, t, period - 1)[:, :, :2 * t]


def _merge_kernel(x_ref, g0_ref, g1_ref, g2_ref, u_ref, halo_ref, ygla_ref, ydiff_ref,
                  pw_ref, ps_ref, wb_ref, wo_ref, gpost_ref, o_ref, ucat_ref, *, tiles_per_seq):
    i = pl.program_id(0)
    tm = TM_MERGE
    t_in_seq = i % tiles_per_seq
    halo = halo_ref[...].astype(F32)
    ucat_ref[0:POOL_HALO, :] = jnp.where(t_in_seq == 0, 0.0, halo)
    ucat_ref[POOL_HALO:, :] = u_ref[...].astype(F32)
    pos = t_in_seq * tm + lax.broadcasted_iota(jnp.int32, (tm, 1), 0)
    mixed = []
    for g, w in enumerate(POOL_WINDOWS):
        cols = slice(g * POOL_GROUP_DIM, (g + 1) * POOL_GROUP_DIM)
        cur = ucat_ref[POOL_HALO:POOL_HALO + tm, cols]
        win = cur
        for s in range(1, w):
            win = win + ucat_ref[POOL_HALO - s:POOL_HALO - s + tm, cols]
        cnt = jnp.minimum(pos + 1, w).astype(F32)
        pooled = win / cnt - cur
        mixed.append(jnp.dot(pooled.astype(BF16), pw_ref[g], preferred_element_type=F32))
    y_pool = (jnp.concatenate(mixed, axis=1) * ps_ref[...]).astype(BF16)

    merged = jax.nn.sigmoid(g0_ref[...].astype(F32)) * jnp.dot(y_pool, wb_ref[0], preferred_element_type=F32)
    merged = merged + jax.nn.sigmoid(g1_ref[...].astype(F32)) * jnp.dot(
        ygla_ref[...], wb_ref[1], preferred_element_type=F32)
    merged = merged + jax.nn.sigmoid(g2_ref[...].astype(F32)) * jnp.dot(
        ydiff_ref[...], wb_ref[2], preferred_element_type=F32)
    out = jnp.dot(merged.astype(BF16), wo_ref[...], preferred_element_type=F32)
    o_ref[...] = x_ref[...] + _rms(out, gpost_ref[...])


def _merge(x2, proj, y_gla, y_diff, pool_w, pool_scale, w_branch, w_o, gpost, seq):
    n = x2.shape[0]
    tm = TM_MERGE
    halo_blocks = tm // POOL_HALO
    return pl.pallas_call(
        functools.partial(_merge_kernel, tiles_per_seq=seq // tm),
        out_shape=jax.ShapeDtypeStruct((n, D_MODEL), F32),
        grid=(n // tm,),
        in_specs=[
            pl.BlockSpec((tm, D_MODEL), lambda i: (i, 0)),
            pl.BlockSpec((tm, D_MODEL), lambda i: (i, C_GATE // D_MODEL + 0)),
            pl.BlockSpec((tm, D_MODEL), lambda i: (i, C_GATE // D_MODEL + 1)),
            pl.BlockSpec((tm, D_MODEL), lambda i: (i, C_GATE // D_MODEL + 2)),
            pl.BlockSpec((tm, POOL_DIM), lambda i: (i, C_U // POOL_DIM)),
            pl.BlockSpec((POOL_HALO, POOL_DIM),
                         lambda i: (jnp.maximum(i * halo_blocks - 1, 0), C_U // POOL_DIM)),
            pl.BlockSpec((tm, BRANCH_DIM), lambda i: (i, 0)),
            pl.BlockSpec((tm, BRANCH_DIM), lambda i: (i, 0)),
            _const_spec((len(POOL_WINDOWS), POOL_GROUP_DIM, POOL_GROUP_DIM)),
            _const_spec((1, POOL_DIM)),
            _const_spec((3, BRANCH_DIM, D_MODEL)),
            _const_spec((D_MODEL, D_MODEL)),
            _const_spec((1, D_MODEL)),
        ],
        out_specs=pl.BlockSpec((tm, D_MODEL), lambda i: (i, 0)),
        scratch_shapes=[pltpu.VMEM((tm + POOL_HALO, POOL_DIM), F32)],
        compiler_params=pltpu.CompilerParams(
            dimension_semantics=("parallel",), vmem_limit_bytes=VMEM_LIMIT),
        name="merge",
    )(x2, proj, proj, proj, proj, proj, y_gla, y_diff, pool_w, pool_scale, w_branch, w_o, gpost)


def _ffn_kernel(x_ref, gpre_ref, wg_ref, wu_ref, wd_ref, gpost_ref, o_ref):
    x = x_ref[...]
    h = _rms(x, gpre_ref[...]).astype(BF16)
    g = jnp.dot(h, wg_ref[...], preferred_element_type=F32)
    u = jnp.dot(h, wu_ref[...], preferred_element_type=F32)
    a = (g * jax.nn.sigmoid(g) * u).astype(BF16)
    f = jnp.dot(a, wd_ref[...], preferred_element_type=F32)
    o_ref[...] = x + _rms(f, gpost_ref[...])


def _ffn(x2, gpre, wg, wu, wd, gpost):
    n = x2.shape[0]
    tm = TM_FFN
    return pl.pallas_call(
        _ffn_kernel,
        out_shape=jax.ShapeDtypeStruct((n, D_MODEL), F32),
        grid=(n // tm,),
        in_specs=[
            pl.BlockSpec((tm, D_MODEL), lambda i: (i, 0)),
            _const_spec((1, D_MODEL)),
            _const_spec((D_MODEL, FFN_DIM)),
            _const_spec((D_MODEL, FFN_DIM)),
            _const_spec((FFN_DIM, D_MODEL)),
            _const_spec((1, D_MODEL)),
        ],
        out_specs=pl.BlockSpec((tm, D_MODEL), lambda i: (i, 0)),
        compiler_params=pltpu.CompilerParams(
            dimension_semantics=("parallel",), vmem_limit_bytes=VMEM_LIMIT),
        name="ffn",
    )(x2, gpre, wg, wu, wd, gpost)


def _reorder_w_in(w):
    pts = [int(p) for p in np.cumsum(IN_SPLITS)[:-1]]
    u, gq, gk, gv, go, gz, dq, dk, dv, gates = jnp.split(w, pts, axis=1)
    pad = jnp.zeros((w.shape[0], Z_PAD - GLA_GATE_RANK), w.dtype)
    return jnp.concatenate([gates, u, gq, gk, gv, go, dq, dk, dv, gz, pad], axis=1).astype(BF16)


def kernel(x, rel_bias, ln_mix_pre, w_in, pool_w, pool_scale, gla_w_decay, gla_b_decay, gla_norm,
           diff_lambda, diff_norm, w_branch, w_o, ln_mix_post, ln_ffn_pre, ffn_w_gate, ffn_w_up,
           ffn_w_down, ln_ffn_post):
    batch, seq, d = x.shape
    depth = w_in.shape[0]
    assert d == D_MODEL and seq % TC_GLA == 0 and seq % T_ATT == 0 and seq % TM_MERGE == 0
    x2 = x.reshape(batch * seq, d)
    bias = _bias_tiles(rel_bias)
    for l in range(depth):
        lam_init = 0.8 - 0.6 * math.exp(-0.3 * l)
        proj = _in_proj(x2, ln_mix_pre[l][None, :], _reorder_w_in(w_in[l]))
        w_decay = jnp.pad(gla_w_decay[l], ((0, Z_PAD - GLA_GATE_RANK), (0, 0))).astype(BF16)
        y_gla = _gla(proj, w_decay, gla_b_decay[l][None, :], gla_norm[l][None, :], batch, seq)
        y_diff = _diff(proj, diff_lambda[l], bias, diff_norm[l][:, None], lam_init, batch, seq)
        x2 = _merge(x2, proj, y_gla, y_diff, pool_w[l].astype(BF16), pool_scale[l][None, :],
                    w_branch[l].astype(BF16), w_o[l].astype(BF16), ln_mix_post[l][None, :], seq)
        x2 = _ffn(x2, ln_ffn_pre[l][None, :], ffn_w_gate[l].astype(BF16), ffn_w_up[l].astype(BF16),
                  ffn_w_down[l].astype(BF16), ln_ffn_post[l][None, :])
    return x2.reshape(batch, seq, d)
```

```python
import functools
import math

import numpy as np
import jax
import jax.numpy as jnp
from jax import lax
from jax.experimental import pallas as pl
from jax.experimental.pallas import tpu as pltpu

F32 = jnp.float32
BF16 = jnp.bfloat16

D_MODEL = 1024
POOL_WINDOWS = (2, 4, 8, 16)
POOL_GROUP_DIM = 128
POOL_DIM = 512
GLA_HEADS = 4
GLA_DK = 64
GLA_DV = 128
GLA_QK_DIM = 256
GLA_V_DIM = 512
GLA_GATE_RANK = 16
GLA_TAU = 16.0
GLA_CHUNK = 64
DIFF_HEADS = 4
DIFF_DK = 64
DIFF_DV = 128
REL_BUCKETS = 32
REL_MAX_DIST = 128
BRANCH_DIM = 512
FFN_DIM = 2816
RMS_EPS = 1e-6
IN_SPLITS = (512, 256, 256, 512, 512, 16, 512, 512, 512, 3072)

LANES = 128
VMEM_LIMIT = 56 * 1024 * 1024

C_GATE = 0
C_U = 3072
C_GQ = 3584
C_GK = 3840
C_GV = 4096
C_GO = 4608
C_DQ = 5120
C_DK = 5632
C_DV = 6144
C_Z = 6656
Z_PAD = LANES
PROJ_DIM = C_Z + Z_PAD

TM_PROJ = 512
TM_MERGE = 256
TM_FFN = 512
TC_GLA = 512
T_ATT = 256
POOL_HALO = 16


def _rms(x, gain):
    ms = jnp.mean(x * x, axis=-1, keepdims=True)
    return x * lax.rsqrt(ms + RMS_EPS) * gain


def _nt_dot(a, b):
    return lax.dot_general(a, b, (((1,), (1,)), ((), ())), preferred_element_type=F32)


def _tn_dot(a, b):
    return lax.dot_general(a, b, (((0,), (0,)), ((), ())), preferred_element_type=F32)


def _const_spec(shape):
    nd = len(shape)
    return pl.BlockSpec(shape, lambda *_: (0,) * nd, pipeline_mode=pl.Buffered(1))


def _in_proj_kernel(x_ref, g_ref, w_ref, o_ref):
    h = _rms(x_ref[...], g_ref[...]).astype(BF16)
    step = 1024
    for c in range(0, PROJ_DIM, step):
        w = min(step, PROJ_DIM - c)
        o_ref[:, c:c + w] = jnp.dot(h, w_ref[:, c:c + w], preferred_element_type=F32).astype(BF16)


def _in_proj(x2, gain, w):
    n = x2.shape[0]
    return pl.pallas_call(
        _in_proj_kernel,
        out_shape=jax.ShapeDtypeStruct((n, PROJ_DIM), BF16),
        grid=(n // TM_PROJ,),
        in_specs=[
            pl.BlockSpec((TM_PROJ, D_MODEL), lambda i: (i, 0)),
            _const_spec((1, D_MODEL)),
            _const_spec((D_MODEL, PROJ_DIM)),
        ],
        out_specs=pl.BlockSpec((TM_PROJ, PROJ_DIM), lambda i: (i, 0)),
        compiler_params=pltpu.CompilerParams(
            dimension_semantics=("parallel",), vmem_limit_bytes=VMEM_LIMIT),
        name="in_proj",
    )(x2, gain, w)


def _log_sigmoid(x):
    return jnp.minimum(x, 0.0) - jnp.log1p(jnp.exp(-jnp.abs(x)))


def _gla_kernel(q_ref, k_ref, v_ref, go_ref, z_ref, wd_ref, bd_ref, gn_ref, y_ref, st_ref, la_ref):
    @pl.when(pl.program_id(1) == 0)
    def _():
        st_ref[...] = jnp.zeros_like(st_ref)

    zl = jnp.dot(z_ref[...], wd_ref[...], preferred_element_type=F32) + bd_ref[...]
    la_ref[...] = _log_sigmoid(zl) / GLA_TAU

    row = lax.broadcasted_iota(jnp.int32, (GLA_CHUNK, GLA_CHUNK), 0)
    col = lax.broadcasted_iota(jnp.int32, (GLA_CHUNK, GLA_CHUNK), 1)
    tril = row >= col
    tril_bf = jnp.where(tril, 1.0, 0.0).astype(BF16)
    lane = lax.broadcasted_iota(jnp.int32, (1, LANES), 1)
    head_mask = (jnp.where(lane < GLA_DK, 1.0, 0.0), jnp.where(lane >= GLA_DK, 1.0, 0.0))
    gn = gn_ref[...]

    def chunk(c, carry):
        r = pl.ds(pl.multiple_of(c * GLA_CHUNK, GLA_CHUNK), GLA_CHUNK)
        la = la_ref[r, :]
        hi = la.astype(BF16)
        r1 = la - hi.astype(F32)
        mid = r1.astype(BF16)
        lo = (r1 - mid.astype(F32)).astype(BF16)
        cs = jnp.dot(tril_bf, jnp.concatenate([hi, mid, lo], axis=1), preferred_element_type=F32)
        cum = cs[:, :GLA_QK_DIM] + cs[:, GLA_QK_DIM:2 * GLA_QK_DIM] + cs[:, 2 * GLA_QK_DIM:]
        cum_last = cum[GLA_CHUNK - 1:GLA_CHUNK, :]
        q = q_ref[r, :].astype(F32)
        k = k_ref[r, :].astype(F32)
        q_dec = (q * (GLA_DK ** -0.5) * jnp.exp(cum)).astype(BF16)
        k_dec = k * jnp.exp(-cum)
        k_end = k * jnp.exp(cum_last - cum)
        decay = jnp.exp(cum_last)
        for h in range(GLA_HEADS):
            pair = slice((h // 2) * LANES, (h // 2 + 1) * LANES)
            vcols = slice(h * GLA_DV, (h + 1) * GLA_DV)
            hm = head_mask[h % 2]
            q_p = q_dec[:, pair]
            k_dec_h = (k_dec[:, pair] * hm).astype(BF16)
            k_end_h = (k_end[:, pair] * hm).astype(BF16)
            v_h = v_ref[r, vcols]
            scores = jnp.where(tril, _nt_dot(q_p, k_dec_h), 0.0).astype(BF16)
            st = st_ref[h]
            o = jnp.dot(scores, v_h, preferred_element_type=F32) + _nt_dot(q_p, st.astype(BF16))
            st_ref[h] = st * decay[:, pair] + _tn_dot(v_h, k_end_h)
            g = go_ref[r, vcols].astype(F32)
            y_ref[r, vcols] = (_rms(o, gn) * (g * jax.nn.sigmoid(g))).astype(BF16)
        return carry

    lax.fori_loop(0, TC_GLA // GLA_CHUNK, chunk, 0)


def _gla(proj, w_decay, b_decay, norm_gain, batch, seq):
    n = proj.shape[0]
    tps = seq // TC_GLA
    row = lambda b, t: b * tps + t
    return pl.pallas_call(
        _gla_kernel,
        out_shape=jax.ShapeDtypeStruct((n, GLA_V_DIM), BF16),
        grid=(batch, tps),
        in_specs=[
            pl.BlockSpec((TC_GLA, GLA_QK_DIM), lambda b, t: (row(b, t), C_GQ // GLA_QK_DIM)),
            pl.BlockSpec((TC_GLA, GLA_QK_DIM), lambda b, t: (row(b, t), C_GK // GLA_QK_DIM)),
            pl.BlockSpec((TC_GLA, GLA_V_DIM), lambda b, t: (row(b, t), C_GV // GLA_V_DIM)),
            pl.BlockSpec((TC_GLA, GLA_V_DIM), lambda b, t: (row(b, t), C_GO // GLA_V_DIM)),
            pl.BlockSpec((TC_GLA, Z_PAD), lambda b, t: (row(b, t), C_Z // Z_PAD)),
            _const_spec((Z_PAD, GLA_QK_DIM)),
            _const_spec((1, GLA_QK_DIM)),
            _const_spec((1, GLA_DV)),
        ],
        out_specs=pl.BlockSpec((TC_GLA, GLA_V_DIM), lambda b, t: (row(b, t), 0)),
        scratch_shapes=[
            pltpu.VMEM((GLA_HEADS, GLA_DV, LANES), F32),
            pltpu.VMEM((TC_GLA, GLA_QK_DIM), F32),
        ],
        compiler_params=pltpu.CompilerParams(
            dimension_semantics=("parallel", "arbitrary"), vmem_limit_bytes=VMEM_LIMIT),
        name="gla",
    )(proj, proj, proj, proj, proj, w_decay, b_decay, norm_gain)


ONES_ROWS = 16
LOG2E = 1.4426950408889634


def _diff_kernel(lam_ref, q_ref, k_ref, v_ref, bias_ref, gn_ref, y_ref,
                 qs_ref, vt_ref, s0_ref, s1_ref, mx0_ref, mx1_ref, m_ref, acc_ref, *, lam_init):
    i = pl.program_id(1)
    t = T_ATT
    lane = lax.broadcasted_iota(jnp.int32, (1, LANES), 1)
    for h in range(DIFF_HEADS):
        q = q_ref[:, h * LANES:(h + 1) * LANES].astype(F32) * (DIFF_DK ** -0.5 * LOG2E)
        qs_ref[h, 0:t, :] = jnp.where(lane < DIFF_DK, q, 0.0).astype(BF16)
        qs_ref[h, t:2 * t, :] = jnp.where(lane >= DIFF_DK, q, 0.0).astype(BF16)

    def rows(j):
        return pl.ds(pl.multiple_of(j * t, t), t)

    def head_cols(h):
        return slice(h * LANES, (h + 1) * LANES)

    m_ref[...] = jnp.full_like(m_ref, -1e30)
    acc_ref[...] = jnp.zeros_like(acc_ref)

    @pl.when(i == 0)
    def _():
        vt_ref[:, :, DIFF_DV:, :] = jnp.ones((DIFF_HEADS, vt_ref.shape[1], ONES_ROWS, t), BF16)

        def transpose_block(jb, carry):
            for h in range(DIFF_HEADS):
                vt_ref[h, jb, 0:DIFF_DV, :] = v_ref[rows(jb), head_cols(h)].T
            return carry

        lax.fori_loop(0, vt_ref.shape[1], transpose_block, 0)

    def logits(j, buf):
        s_ref, mx_ref = buf
        for h in range(DIFF_HEADS):
            s = _nt_dot(k_ref[rows(j), head_cols(h)], qs_ref[h])
            s_ref[h] = s
            mx_ref[h] = jnp.max(s, axis=0, keepdims=True)

    def softmax_pv(j, buf, bias_cols):
        s_ref, mx_ref = buf
        for h in range(DIFF_HEADS):
            s = s_ref[h]
            if bias_cols is None:
                s_max = mx_ref[h]
            else:
                bias = bias_ref[h, :, bias_cols]
                s = s + jnp.concatenate([bias, bias], axis=1)
                s_max = jnp.max(s, axis=0, keepdims=True)
            m_old = m_ref[h]
            m_new = jnp.maximum(m_old, s_max)
            alpha = jnp.exp2(m_old - m_new)
            p = jnp.exp2(s - m_new).astype(BF16)
            m_ref[h] = m_new
            acc_ref[h] = alpha * acc_ref[h] + jnp.dot(vt_ref[h, j], p, preferred_element_type=F32)

    s0 = (s0_ref, mx0_ref)
    s1 = (s1_ref, mx1_ref)
    n_far = jnp.maximum(i - 1, 0)
    peeled = n_far % 2

    @pl.when(i % 2 == 0)
    def _():
        logits(0, s0)

    @pl.when(i % 2 == 1)
    def _():
        logits(0, s1)

    @pl.when(peeled == 1)
    def _():
        logits(1, s1)
        softmax_pv(0, s0, None)

    def far_pair(jj, carry):
        j = peeled + 2 * jj
        logits(j + 1, s0)
        softmax_pv(j, s1, None)
        logits(j + 2, s1)
        softmax_pv(j + 1, s0, None)
        return carry

    lax.fori_loop(0, n_far // 2, far_pair, 0)

    @pl.when(i >= 1)
    def _():
        logits(i, s0)
        softmax_pv(i - 1, s1, slice(t, 2 * t))

    softmax_pv(i, s0, slice(0, t))

    lam_p = lam_ref[...]
    lam = (jnp.exp(jnp.sum(lam_p[0:1] * lam_p[1:2], axis=1, keepdims=True))
           - jnp.exp(jnp.sum(lam_p[2:3] * lam_p[3:4], axis=1, keepdims=True)) + lam_init)
    for h in range(DIFF_HEADS):
        acc = acc_ref[h, 0:DIFF_DV, :]
        inv_l = 1.0 / acc_ref[h, DIFF_DV:DIFF_DV + 1, :]
        o = acc[:, :t] * inv_l[:, :t] - lam * (acc[:, t:] * inv_l[:, t:])
        ms = jnp.mean(o * o, axis=0, keepdims=True)
        o = o * lax.rsqrt(ms + RMS_EPS) * gn_ref[...] * (1.0 - lam_init)
        y_ref[:, h * DIFF_DV:(h + 1) * DIFF_DV] = o.T.astype(BF16)


def _diff(proj, lam, bias, norm_gain_col, lam_init, batch, seq):
    n = proj.shape[0]
    nq = seq // T_ATT
    hw = DIFF_HEADS * LANES
    return pl.pallas_call(
        functools.partial(_diff_kernel, lam_init=lam_init),
        out_shape=jax.ShapeDtypeStruct((n, DIFF_HEADS * DIFF_DV), BF16),
        grid=(batch, nq),
        in_specs=[
            _const_spec((4, DIFF_DK)),
            pl.BlockSpec((T_ATT, hw), lambda b, i: (b * nq + i, C_DQ // hw)),
            pl.BlockSpec((seq, hw), lambda b, i: (b, C_DK // hw), pipeline_mode=pl.Buffered(1)),
            pl.BlockSpec((seq, hw), lambda b, i: (b, C_DV // hw), pipeline_mode=pl.Buffered(1)),
            _const_spec((DIFF_HEADS, T_ATT, 2 * T_ATT)),
            _const_spec((DIFF_DV, 1)),
        ],
        out_specs=pl.BlockSpec((T_ATT, DIFF_HEADS * DIFF_DV), lambda b, i: (b * nq + i, 0)),
        scratch_shapes=[
            pltpu.VMEM((DIFF_HEADS, 2 * T_ATT, LANES), BF16),
            pltpu.VMEM((DIFF_HEADS, nq, DIFF_DV + ONES_ROWS, T_ATT), BF16),
            pltpu.VMEM((DIFF_HEADS, T_ATT, 2 * T_ATT), F32),
            pltpu.VMEM((DIFF_HEADS, T_ATT, 2 * T_ATT), F32),
            pltpu.VMEM((DIFF_HEADS, 1, 2 * T_ATT), F32),
            pltpu.VMEM((DIFF_HEADS, 1, 2 * T_ATT), F32),
            pltpu.VMEM((DIFF_HEADS, 1, 2 * T_ATT), F32),
            pltpu.VMEM((DIFF_HEADS, DIFF_DV + ONES_ROWS, 2 * T_ATT), F32),
        ],
        compiler_params=pltpu.CompilerParams(
            dimension_semantics=("parallel", "arbitrary"), vmem_limit_bytes=VMEM_LIMIT),
        name="diff_attn",
    )(lam, proj, proj, proj, bias, norm_gain_col)


def _t5_bucket(rel):
    n = jnp.maximum(rel, 0)
    max_exact = REL_BUCKETS // 2
    nf = jnp.maximum(n, 1).astype(F32)
    large = max_exact + (jnp.log(nf / max_exact) / math.log(REL_MAX_DIST / max_exact)
                         * (REL_BUCKETS - max_exact)).astype(jnp.int32)
    large = jnp.minimum(large, REL_BUCKETS - 1)
    return jnp.where(n < max_exact, n, large)


def _bias_tiles(rel_bias):
    assert T_ATT + 1 >= REL_MAX_DIST
    t = T_ATT
    table = rel_bias.astype(F32)
    near = (table[_t5_bucket(jnp.arange(2 * t, dtype=jnp.int32))] - table[REL_BUCKETS - 1]) * LOG2E
    by_rel = jnp.concatenate([near, jnp.full((t, DIFF_HEADS), -jnp.inf, F32)], axis=0).T
    period = 3 * t
    flat = jnp.tile(by_rel, (1, t))[:, :t * (period - 1)]
    return flat.reshape(DIFF_HEADS, t, period - 1)[:, :, :2 * t]


def _merge_kernel(x_ref, g0_ref, g1_ref, g2_ref, u_ref, halo_ref, ygla_ref, ydiff_ref,
                  pw_ref, ps_ref, wb_ref, wo_ref, gpost_ref, o_ref, ucat_ref, *, tiles_per_seq):
    i = pl.program_id(0)
    tm = TM_MERGE
    t_in_seq = i % tiles_per_seq
    halo = halo_ref[...].astype(F32)
    ucat_ref[0:POOL_HALO, :] = jnp.where(t_in_seq == 0, 0.0, halo)
    ucat_ref[POOL_HALO:, :] = u_ref[...].astype(F32)
    pos = t_in_seq * tm + lax.broadcasted_iota(jnp.int32, (tm, 1), 0)
    mixed = []
    for g, w in enumerate(POOL_WINDOWS):
        cols = slice(g * POOL_GROUP_DIM, (g + 1) * POOL_GROUP_DIM)
        cur = ucat_ref[POOL_HALO:POOL_HALO + tm, cols]
        win = cur
        for s in range(1, w):
            win = win + ucat_ref[POOL_HALO - s:POOL_HALO - s + tm, cols]
        cnt = jnp.minimum(pos + 1, w).astype(F32)
        pooled = win / cnt - cur
        mixed.append(jnp.dot(pooled.astype(BF16), pw_ref[g], preferred_element_type=F32))
    y_pool = (jnp.concatenate(mixed, axis=1) * ps_ref[...]).astype(BF16)

    merged = jax.nn.sigmoid(g0_ref[...].astype(F32)) * jnp.dot(y_pool, wb_ref[0], preferred_element_type=F32)
    merged = merged + jax.nn.sigmoid(g1_ref[...].astype(F32)) * jnp.dot(
        ygla_ref[...], wb_ref[1], preferred_element_type=F32)
    merged = merged + jax.nn.sigmoid(g2_ref[...].astype(F32)) * jnp.dot(
        ydiff_ref[...], wb_ref[2], preferred_element_type=F32)
    out = jnp.dot(merged.astype(BF16), wo_ref[...], preferred_element_type=F32)
    o_ref[...] = x_ref[...] + _rms(out, gpost_ref[...])


def _merge(x2, proj, y_gla, y_diff, pool_w, pool_scale, w_branch, w_o, gpost, seq):
    n = x2.shape[0]
    tm = TM_MERGE
    halo_blocks = tm // POOL_HALO
    return pl.pallas_call(
        functools.partial(_merge_kernel, tiles_per_seq=seq // tm),
        out_shape=jax.ShapeDtypeStruct((n, D_MODEL), F32),
        grid=(n // tm,),
        in_specs=[
            pl.BlockSpec((tm, D_MODEL), lambda i: (i, 0)),
            pl.BlockSpec((tm, D_MODEL), lambda i: (i, C_GATE // D_MODEL + 0)),
            pl.BlockSpec((tm, D_MODEL), lambda i: (i, C_GATE // D_MODEL + 1)),
            pl.BlockSpec((tm, D_MODEL), lambda i: (i, C_GATE // D_MODEL + 2)),
            pl.BlockSpec((tm, POOL_DIM), lambda i: (i, C_U // POOL_DIM)),
            pl.BlockSpec((POOL_HALO, POOL_DIM),
                         lambda i: (jnp.maximum(i * halo_blocks - 1, 0), C_U // POOL_DIM)),
            pl.BlockSpec((tm, BRANCH_DIM), lambda i: (i, 0)),
            pl.BlockSpec((tm, BRANCH_DIM), lambda i: (i, 0)),
            _const_spec((len(POOL_WINDOWS), POOL_GROUP_DIM, POOL_GROUP_DIM)),
            _const_spec((1, POOL_DIM)),
            _const_spec((3, BRANCH_DIM, D_MODEL)),
            _const_spec((D_MODEL, D_MODEL)),
            _const_spec((1, D_MODEL)),
        ],
        out_specs=pl.BlockSpec((tm, D_MODEL), lambda i: (i, 0)),
        scratch_shapes=[pltpu.VMEM((tm + POOL_HALO, POOL_DIM), F32)],
        compiler_params=pltpu.CompilerParams(
            dimension_semantics=("parallel",), vmem_limit_bytes=VMEM_LIMIT),
        name="merge",
    )(x2, proj, proj, proj, proj, proj, y_gla, y_diff, pool_w, pool_scale, w_branch, w_o, gpost)


def _ffn_kernel(x_ref, gpre_ref, wg_ref, wu_ref, wd_ref, gpost_ref, o_ref):
    x = x_ref[...]
    h = _rms(x, gpre_ref[...]).astype(BF16)
    g = jnp.dot(h, wg_ref[...], preferred_element_type=F32)
    u = jnp.dot(h, wu_ref[...], preferred_element_type=F32)
    a = (g * jax.nn.sigmoid(g) * u).astype(BF16)
    f = jnp.dot(a, wd_ref[...], preferred_element_type=F32)
    o_ref[...] = x + _rms(f, gpost_ref[...])


def _ffn(x2, gpre, wg, wu, wd, gpost):
    n = x2.shape[0]
    tm = TM_FFN
    return pl.pallas_call(
        _ffn_kernel,
        out_shape=jax.ShapeDtypeStruct((n, D_MODEL), F32),
        grid=(n // tm,),
        in_specs=[
            pl.BlockSpec((tm, D_MODEL), lambda i: (i, 0)),
            _const_spec((1, D_MODEL)),
            _const_spec((D_MODEL, FFN_DIM)),
            _const_spec((D_MODEL, FFN_DIM)),
            _const_spec((FFN_DIM, D_MODEL)),
            _const_spec((1, D_MODEL)),
        ],
        out_specs=pl.BlockSpec((tm, D_MODEL), lambda i: (i, 0)),
        compiler_params=pltpu.CompilerParams(
            dimension_semantics=("parallel",), vmem_limit_bytes=VMEM_LIMIT),
        name="ffn",
    )(x2, gpre, wg, wu, wd, gpost)


def _reorder_w_in(w):
    pts = [int(p) for p in np.cumsum(IN_SPLITS)[:-1]]
    u, gq, gk, gv, go, gz, dq, dk, dv, gates = jnp.split(w, pts, axis=1)
    pad = jnp.zeros((w.shape[0], Z_PAD - GLA_GATE_RANK), w.dtype)
    return jnp.concatenate([gates, u, gq, gk, gv, go, dq, dk, dv, gz, pad], axis=1).astype(BF16)


def kernel(x, rel_bias, ln_mix_pre, w_in, pool_w, pool_scale, gla_w_decay, gla_b_decay, gla_norm,
           diff_lambda, diff_norm, w_branch, w_o, ln_mix_post, ln_ffn_pre, ffn_w_gate, ffn_w_up,
           ffn_w_down, ln_ffn_post):
    batch, seq, d = x.shape
    depth = w_in.shape[0]
    assert d == D_MODEL and seq % TC_GLA == 0 and seq % T_ATT == 0 and seq % TM_MERGE == 0
    x2 = x.reshape(batch * seq, d)
    bias = _bias_tiles(rel_bias)
    for l in range(depth):
        lam_init = 0.8 - 0.6 * math.exp(-0.3 * l)
        proj = _in_proj(x2, ln_mix_pre[l][None, :], _reorder_w_in(w_in[l]))
        w_decay = jnp.pad(gla_w_decay[l], ((0, Z_PAD - GLA_GATE_RANK), (0, 0))).astype(BF16)
        y_gla = _gla(proj, w_decay, gla_b_decay[l][None, :], gla_norm[l][None, :], batch, seq)
        y_diff = _diff(proj, diff_lambda[l], bias, diff_norm[l][:, None], lam_init, batch, seq)
        x2 = _merge(x2, proj, y_gla, y_diff, pool_w[l].astype(BF16), pool_scale[l][None, :],
                    w_branch[l].astype(BF16), w_o[l].astype(BF16), ln_mix_post[l][None, :], seq)
        x2 = _ffn(x2, ln_ffn_pre[l][None, :], ffn_w_gate[l].astype(BF16), ffn_w_up[l].astype(BF16),
                  ffn_w_down[l].astype(BF16), ln_ffn_post[l][None, :])
    return x2.reshape(batch, seq, d)
```

```python
import functools
import math

import jax
import jax.numpy as jnp
from jax import lax
from jax.experimental import pallas as pl
from jax.experimental.pallas import tpu as pltpu

F32 = jnp.float32
BF16 = jnp.bfloat16

D_MODEL = 1024
POOL_WINDOWS = (2, 4, 8, 16)
POOL_GROUP_DIM = 128
POOL_DIM = 512
GLA_HEADS = 4
GLA_DK = 64
GLA_DV = 128
GLA_QK_DIM = 256
GLA_V_DIM = 512
GLA_GATE_RANK = 16
GLA_TAU = 16.0
GLA_CHUNK = 64
DIFF_HEADS = 4
DIFF_DK = 64
DIFF_DV = 128
REL_BUCKETS = 32
REL_MAX_DIST = 128
BRANCH_DIM = 512
FFN_DIM = 2816
RMS_EPS = 1e-6
IN_SPLITS = (512, 256, 256, 512, 512, 16, 512, 512, 512, 3072)

LANES = 128
VMEM_LIMIT = 56 * 1024 * 1024

C_GATE = 0
C_U = 3072
C_GQ = 3584
C_GK = 3840
C_GV = 4096
C_GO = 4608
C_DQ = 5120
C_DK = 5632
C_DV = 6144
C_Z = 6656
Z_PAD = LANES
PROJ_DIM = C_Z + Z_PAD

TM_PROJ = 512
TM_MERGE = 256
TM_FFN = 512
TC_GLA = 512
GLA_BLOCK = 256
T_ATT = 256
POOL_HALO = 16


def _rms(x, gain):
    ms = jnp.mean(x * x, axis=-1, keepdims=True)
    return x * lax.rsqrt(ms + RMS_EPS) * gain


def _nt_dot(a, b):
    return lax.dot_general(a, b, (((1,), (1,)), ((), ())), preferred_element_type=F32)


def _tn_dot(a, b):
    return lax.dot_general(a, b, (((0,), (0,)), ((), ())), preferred_element_type=F32)


def _const_spec(shape):
    nd = len(shape)
    return pl.BlockSpec(shape, lambda *_: (0,) * nd, pipeline_mode=pl.Buffered(1))


W_HEAD = C_DQ - C_U
W_TAIL = C_Z - C_DQ + C_U
N_DIFF = C_Z - C_DQ


def _in_proj_kernel(x_ref, g_ref, wh_ref, wz_ref, wt_ref, o_ref):
    h = _rms(x_ref[...], g_ref[...]).astype(BF16)
    step = 1024
    segments = ((wt_ref, N_DIFF, W_TAIL - N_DIFF, C_GATE), (wh_ref, 0, W_HEAD, C_U),
                (wt_ref, 0, N_DIFF, C_DQ), (wz_ref, 0, Z_PAD, C_Z))
    for w_ref, src, width, dst in segments:
        for c in range(0, width, step):
            w = min(step, width - c)
            o_ref[:, dst + c:dst + c + w] = jnp.dot(
                h, w_ref[:, src + c:src + c + w], preferred_element_type=F32).astype(BF16)


def _in_proj(x2, gain, w_head, w_z, w_tail):
    n = x2.shape[0]
    return pl.pallas_call(
        _in_proj_kernel,
        out_shape=jax.ShapeDtypeStruct((n, PROJ_DIM), BF16),
        grid=(n // TM_PROJ,),
        in_specs=[
            pl.BlockSpec((TM_PROJ, D_MODEL), lambda i: (i, 0)),
            _const_spec((1, D_MODEL)),
            _const_spec((D_MODEL, W_HEAD)),
            _const_spec((D_MODEL, Z_PAD)),
            _const_spec((D_MODEL, W_TAIL)),
        ],
        out_specs=pl.BlockSpec((TM_PROJ, PROJ_DIM), lambda i: (i, 0)),
        compiler_params=pltpu.CompilerParams(
            dimension_semantics=("parallel",), vmem_limit_bytes=VMEM_LIMIT),
        name="in_proj",
    )(x2, gain, w_head, w_z, w_tail)


def _log_sigmoid(x):
    return jnp.minimum(x, 0.0) - jnp.log1p(jnp.exp(-jnp.abs(x)))


def _gla_kernel(q_ref, k_ref, v_ref, go_ref, z_ref, wd_ref, bd_ref, gn_ref, y_ref, st_ref):
    n_chunks = TC_GLA // GLA_CHUNK
    chunks_per_block = GLA_BLOCK // GLA_CHUNK

    @pl.when(pl.program_id(1) == 0)
    def _():
        st_ref[...] = jnp.zeros_like(st_ref)

    def pair(h):
        return slice((h // 2) * LANES, (h // 2 + 1) * LANES)

    def vcols(h):
        return slice(h * GLA_DV, (h + 1) * GLA_DV)

    def crow(c):
        return slice(c * GLA_CHUNK, (c + 1) * GLA_CHUNK)

    zl = jnp.dot(z_ref[...], wd_ref[...], preferred_element_type=F32) + bd_ref[...]
    la = _log_sigmoid(zl) / GLA_TAU

    hi = la.astype(BF16)
    r1 = la - hi.astype(F32)
    mid = r1.astype(BF16)
    lo = (r1 - mid.astype(F32)).astype(BF16)
    pieces = jnp.concatenate([hi, mid, lo], axis=1)
    row = lax.broadcasted_iota(jnp.int32, (GLA_CHUNK, GLA_CHUNK), 0)
    col = lax.broadcasted_iota(jnp.int32, (GLA_CHUNK, GLA_CHUNK), 1)
    tril_bf = jnp.where(row >= col, 1.0, 0.0).astype(BF16)
    cums, totals, decay = [], [], []
    for c in range(n_chunks):
        cs = jnp.dot(tril_bf, pieces[crow(c), :], preferred_element_type=F32)
        cum_c = cs[:, :GLA_QK_DIM] + cs[:, GLA_QK_DIM:2 * GLA_QK_DIM] + cs[:, 2 * GLA_QK_DIM:]
        last = cum_c[GLA_CHUNK - 1:GLA_CHUNK, :]
        cums.append(cum_c)
        totals.append(jnp.broadcast_to(last, (GLA_CHUNK, GLA_QK_DIM)))
        decay.append(jnp.exp(last))
    cum = jnp.concatenate(cums, axis=0)
    total = jnp.concatenate(totals, axis=0)

    q = q_ref[...].astype(F32)
    k = k_ref[...].astype(F32)
    q_dec = (q * (GLA_DK ** -0.5) * jnp.exp(cum)).astype(BF16)
    k_dec = k * jnp.exp(-cum)
    k_end = k * jnp.exp(total - cum)
    lane = lax.broadcasted_iota(jnp.int32, (1, LANES), 1)
    head_mask = (jnp.where(lane < GLA_DK, 1.0, 0.0), jnp.where(lane >= GLA_DK, 1.0, 0.0))
    k_dec_h = [(k_dec[:, pair(h)] * head_mask[h % 2]).astype(BF16) for h in range(GLA_HEADS)]
    k_end_h = [(k_end[:, pair(h)] * head_mask[h % 2]).astype(BF16) for h in range(GLA_HEADS)]

    state_in = [[None] * GLA_HEADS for _ in range(n_chunks)]
    for h in range(GLA_HEADS):
        st = st_ref[h]
        for c in range(n_chunks):
            state_in[c][h] = st.astype(BF16)
            st = st * decay[c][:, pair(h)] + _tn_dot(v_ref[crow(c), vcols(h)], k_end_h[h][crow(c), :])
        st_ref[h] = st

    row = lax.broadcasted_iota(jnp.int32, (GLA_BLOCK, GLA_BLOCK), 0)
    col = lax.broadcasted_iota(jnp.int32, (GLA_BLOCK, GLA_BLOCK), 1)
    visible = (row >= col) & (row // GLA_CHUNK == col // GLA_CHUNK)
    gn = gn_ref[...]
    for b in range(TC_GLA // GLA_BLOCK):
        rb = slice(b * GLA_BLOCK, (b + 1) * GLA_BLOCK)
        for h in range(GLA_HEADS):
            scores = jnp.where(visible, _nt_dot(q_dec[rb, pair(h)], k_dec_h[h][rb, :]), 0.0).astype(BF16)
            o = jnp.dot(scores, v_ref[rb, vcols(h)], preferred_element_type=F32)
            o = o + jnp.concatenate(
                [_nt_dot(q_dec[crow(c), pair(h)], state_in[c][h])
                 for c in range(b * chunks_per_block, (b + 1) * chunks_per_block)], axis=0)
            g = go_ref[rb, vcols(h)].astype(F32)
            y_ref[rb, vcols(h)] = (_rms(o, gn) * (g * jax.nn.sigmoid(g))).astype(BF16)


def _gla(proj, w_decay, b_decay, norm_gain, batch, seq):
    n = proj.shape[0]
    tps = seq // TC_GLA
    row = lambda b, t: b * tps + t
    return pl.pallas_call(
        _gla_kernel,
        out_shape=jax.ShapeDtypeStruct((n, GLA_V_DIM), BF16),
        grid=(batch, tps),
        in_specs=[
            pl.BlockSpec((TC_GLA, GLA_QK_DIM), lambda b, t: (row(b, t), C_GQ // GLA_QK_DIM)),
            pl.BlockSpec((TC_GLA, GLA_QK_DIM), lambda b, t: (row(b, t), C_GK // GLA_QK_DIM)),
            pl.BlockSpec((TC_GLA, GLA_V_DIM), lambda b, t: (row(b, t), C_GV // GLA_V_DIM)),
            pl.BlockSpec((TC_GLA, GLA_V_DIM), lambda b, t: (row(b, t), C_GO // GLA_V_DIM)),
            pl.BlockSpec((TC_GLA, Z_PAD), lambda b, t: (row(b, t), C_Z // Z_PAD)),
            _const_spec((Z_PAD, GLA_QK_DIM)),
            _const_spec((1, GLA_QK_DIM)),
            _const_spec((1, GLA_DV)),
        ],
        out_specs=pl.BlockSpec((TC_GLA, GLA_V_DIM), lambda b, t: (row(b, t), 0)),
        scratch_shapes=[pltpu.VMEM((GLA_HEADS, GLA_DV, LANES), F32)],
        compiler_params=pltpu.CompilerParams(
            dimension_semantics=("parallel", "arbitrary"), vmem_limit_bytes=VMEM_LIMIT),
        name="gla",
    )(proj, proj, proj, proj, proj, w_decay, b_decay, norm_gain)


ONES_ROWS = 16
LOG2E = 1.4426950408889634


def _diff_kernel(lam_ref, q_ref, k_ref, v_ref, bias_ref, gn_ref, y_ref,
                 qs_ref, vt_ref, s0_ref, s1_ref, mx0_ref, mx1_ref, m_ref, acc_ref, *, lam_init):
    i = pl.program_id(1)
    t = T_ATT
    lane = lax.broadcasted_iota(jnp.int32, (1, LANES), 1)
    for h in range(DIFF_HEADS):
        q = q_ref[:, h * LANES:(h + 1) * LANES].astype(F32) * (DIFF_DK ** -0.5 * LOG2E)
        qs_ref[h, 0:t, :] = jnp.where(lane < DIFF_DK, q, 0.0).astype(BF16)
        qs_ref[h, t:2 * t, :] = jnp.where(lane >= DIFF_DK, q, 0.0).astype(BF16)

    def rows(j):
        return pl.ds(pl.multiple_of(j * t, t), t)

    def head_cols(h):
        return slice(h * LANES, (h + 1) * LANES)

    m_ref[...] = jnp.full_like(m_ref, -1e30)
    acc_ref[...] = jnp.zeros_like(acc_ref)

    @pl.when(i == 0)
    def _():
        vt_ref[:, :, DIFF_DV:, :] = jnp.ones((DIFF_HEADS, vt_ref.shape[1], ONES_ROWS, t), BF16)

        def transpose_block(jb, carry):
            for h in range(DIFF_HEADS):
                vt_ref[h, jb, 0:DIFF_DV, :] = v_ref[rows(jb), head_cols(h)].T
            return carry

        lax.fori_loop(0, vt_ref.shape[1], transpose_block, 0)

    def logits(j, buf):
        s_ref, mx_ref = buf
        for h in range(DIFF_HEADS):
            s = _nt_dot(k_ref[rows(j), head_cols(h)], qs_ref[h])
            s_ref[h] = s
            mx_ref[h] = jnp.max(s, axis=0, keepdims=True)

    def softmax_pv(j, buf, bias_cols):
        s_ref, mx_ref = buf
        for h in range(DIFF_HEADS):
            s = s_ref[h]
            if bias_cols is None:
                s_max = mx_ref[h]
            else:
                bias = bias_ref[h, :, bias_cols]
                s = s + jnp.concatenate([bias, bias], axis=1)
                s_max = jnp.max(s, axis=0, keepdims=True)
            m_old = m_ref[h]
            m_new = jnp.maximum(m_old, s_max)
            alpha = jnp.exp2(m_old - m_new)
            p = jnp.exp2(s - m_new).astype(BF16)
            m_ref[h] = m_new
            acc_ref[h] = alpha * acc_ref[h] + jnp.dot(vt_ref[h, j], p, preferred_element_type=F32)

    s0 = (s0_ref, mx0_ref)
    s1 = (s1_ref, mx1_ref)
    n_far = jnp.maximum(i - 1, 0)
    peeled = n_far % 2

    @pl.when(i % 2 == 0)
    def _():
        logits(0, s0)

    @pl.when(i % 2 == 1)
    def _():
        logits(0, s1)

    @pl.when(peeled == 1)
    def _():
        logits(1, s1)
        softmax_pv(0, s0, None)

    def far_pair(jj, carry):
        j = peeled + 2 * jj
        logits(j + 1, s0)
        softmax_pv(j, s1, None)
        logits(j + 2, s1)
        softmax_pv(j + 1, s0, None)
        return carry

    lax.fori_loop(0, n_far // 2, far_pair, 0)

    @pl.when(i >= 1)
    def _():
        logits(i, s0)
        softmax_pv(i - 1, s1, slice(t, 2 * t))

    softmax_pv(i, s0, slice(0, t))

    lam_p = lam_ref[...]
    lam = (jnp.exp(jnp.sum(lam_p[0:1] * lam_p[1:2], axis=1, keepdims=True))
           - jnp.exp(jnp.sum(lam_p[2:3] * lam_p[3:4], axis=1, keepdims=True)) + lam_init)
    for h in range(DIFF_HEADS):
        acc = acc_ref[h, 0:DIFF_DV, :]
        inv_l = 1.0 / acc_ref[h, DIFF_DV:DIFF_DV + 1, :]
        o = acc[:, :t] * inv_l[:, :t] - lam * (acc[:, t:] * inv_l[:, t:])
        ms = jnp.mean(o * o, axis=0, keepdims=True)
        o = o * lax.rsqrt(ms + RMS_EPS) * gn_ref[...] * (1.0 - lam_init)
        y_ref[:, h * DIFF_DV:(h + 1) * DIFF_DV] = o.T.astype(BF16)


def _diff(proj, lam, bias, norm_gain_col, lam_init, batch, seq):
    n = proj.shape[0]
    nq = seq // T_ATT
    hw = DIFF_HEADS * LANES
    return pl.pallas_call(
        functools.partial(_diff_kernel, lam_init=lam_init),
        out_shape=jax.ShapeDtypeStruct((n, DIFF_HEADS * DIFF_DV), BF16),
        grid=(batch, nq),
        in_specs=[
            _const_spec((4, DIFF_DK)),
            pl.BlockSpec((T_ATT, hw), lambda b, i: (b * nq + i, C_DQ // hw)),
            pl.BlockSpec((seq, hw), lambda b, i: (b, C_DK // hw), pipeline_mode=pl.Buffered(1)),
            pl.BlockSpec((seq, hw), lambda b, i: (b, C_DV // hw), pipeline_mode=pl.Buffered(1)),
            _const_spec((DIFF_HEADS, T_ATT, 2 * T_ATT)),
            _const_spec((DIFF_DV, 1)),
        ],
        out_specs=pl.BlockSpec((T_ATT, DIFF_HEADS * DIFF_DV), lambda b, i: (b * nq + i, 0)),
        scratch_shapes=[
            pltpu.VMEM((DIFF_HEADS, 2 * T_ATT, LANES), BF16),
            pltpu.VMEM((DIFF_HEADS, nq, DIFF_DV + ONES_ROWS, T_ATT), BF16),
            pltpu.VMEM((DIFF_HEADS, T_ATT, 2 * T_ATT), F32),
            pltpu.VMEM((DIFF_HEADS, T_ATT, 2 * T_ATT), F32),
            pltpu.VMEM((DIFF_HEADS, 1, 2 * T_ATT), F32),
            pltpu.VMEM((DIFF_HEADS, 1, 2 * T_ATT), F32),
            pltpu.VMEM((DIFF_HEADS, 1, 2 * T_ATT), F32),
            pltpu.VMEM((DIFF_HEADS, DIFF_DV + ONES_ROWS, 2 * T_ATT), F32),
        ],
        compiler_params=pltpu.CompilerParams(
            dimension_semantics=("parallel", "arbitrary"), vmem_limit_bytes=VMEM_LIMIT),
        name="diff_attn",
    )(lam, proj, proj, proj, bias, norm_gain_col)


def _t5_bucket(rel):
    n = jnp.maximum(rel, 0)
    max_exact = REL_BUCKETS // 2
    nf = jnp.maximum(n, 1).astype(F32)
    large = max_exact + (jnp.log(nf / max_exact) / math.log(REL_MAX_DIST / max_exact)
                         * (REL_BUCKETS - max_exact)).astype(jnp.int32)
    large = jnp.minimum(large, REL_BUCKETS - 1)
    return jnp.where(n < max_exact, n, large)


def _bias_tiles(rel_bias):
    assert T_ATT + 1 >= REL_MAX_DIST
    t = T_ATT
    table = rel_bias.astype(F32)
    near = (table[_t5_bucket(jnp.arange(2 * t, dtype=jnp.int32))] - table[REL_BUCKETS - 1]) * LOG2E
    by_rel = jnp.concatenate([near, jnp.full((t, DIFF_HEADS), -jnp.inf, F32)], axis=0).T
    period = 3 * t
    flat = jnp.tile(by_rel, (1, t))[:, :t * (period - 1)]
    return flat.reshape(DIFF_HEADS, t, period - 1)[:, :, :2 * t]


def _merge_kernel(x_ref, g0_ref, g1_ref, g2_ref, u_ref, halo_ref, ygla_ref, ydiff_ref,
                  pw_ref, ps_ref, wb_ref, wo_ref, gpost_ref, o_ref, ucat_ref, *, tiles_per_seq):
    i = pl.program_id(0)
    tm = TM_MERGE
    t_in_seq = i % tiles_per_seq
    halo = halo_ref[...].astype(F32)
    ucat_ref[0:POOL_HALO, :] = jnp.where(t_in_seq == 0, 0.0, halo)
    ucat_ref[POOL_HALO:, :] = u_ref[...].astype(F32)
    pos = t_in_seq * tm + lax.broadcasted_iota(jnp.int32, (tm, 1), 0)
    mixed = []
    for g, w in enumerate(POOL_WINDOWS):
        cols = slice(g * POOL_GROUP_DIM, (g + 1) * POOL_GROUP_DIM)
        cur = ucat_ref[POOL_HALO:POOL_HALO + tm, cols]
        win = cur
        for s in range(1, w):
            win = win + ucat_ref[POOL_HALO - s:POOL_HALO - s + tm, cols]
        cnt = jnp.minimum(pos + 1, w).astype(F32)
        pooled = win / cnt - cur
        mixed.append(jnp.dot(pooled.astype(BF16), pw_ref[g], preferred_element_type=F32))
    y_pool = (jnp.concatenate(mixed, axis=1) * ps_ref[...]).astype(BF16)

    merged = jax.nn.sigmoid(g0_ref[...].astype(F32)) * jnp.dot(y_pool, wb_ref[0], preferred_element_type=F32)
    merged = merged + jax.nn.sigmoid(g1_ref[...].astype(F32)) * jnp.dot(
        ygla_ref[...], wb_ref[1], preferred_element_type=F32)
    merged = merged + jax.nn.sigmoid(g2_ref[...].astype(F32)) * jnp.dot(
        ydiff_ref[...], wb_ref[2], preferred_element_type=F32)
    out = jnp.dot(merged.astype(BF16), wo_ref[...], preferred_element_type=F32)
    o_ref[...] = x_ref[...] + _rms(out, gpost_ref[...])


def _merge(x2, proj, y_gla, y_diff, pool_w, pool_scale, w_branch, w_o, gpost, seq):
    n = x2.shape[0]
    tm = TM_MERGE
    halo_blocks = tm // POOL_HALO
    return pl.pallas_call(
        functools.partial(_merge_kernel, tiles_per_seq=seq // tm),
        out_shape=jax.ShapeDtypeStruct((n, D_MODEL), F32),
        grid=(n // tm,),
        in_specs=[
            pl.BlockSpec((tm, D_MODEL), lambda i: (i, 0)),
            pl.BlockSpec((tm, D_MODEL), lambda i: (i, C_GATE // D_MODEL + 0)),
            pl.BlockSpec((tm, D_MODEL), lambda i: (i, C_GATE // D_MODEL + 1)),
            pl.BlockSpec((tm, D_MODEL), lambda i: (i, C_GATE // D_MODEL + 2)),
            pl.BlockSpec((tm, POOL_DIM), lambda i: (i, C_U // POOL_DIM)),
            pl.BlockSpec((POOL_HALO, POOL_DIM),
                         lambda i: (jnp.maximum(i * halo_blocks - 1, 0), C_U // POOL_DIM)),
            pl.BlockSpec((tm, BRANCH_DIM), lambda i: (i, 0)),
            pl.BlockSpec((tm, BRANCH_DIM), lambda i: (i, 0)),
            _const_spec((len(POOL_WINDOWS), POOL_GROUP_DIM, POOL_GROUP_DIM)),
            _const_spec((1, POOL_DIM)),
            _const_spec((3, BRANCH_DIM, D_MODEL)),
            _const_spec((D_MODEL, D_MODEL)),
            _const_spec((1, D_MODEL)),
        ],
        out_specs=pl.BlockSpec((tm, D_MODEL), lambda i: (i, 0)),
        scratch_shapes=[pltpu.VMEM((tm + POOL_HALO, POOL_DIM), F32)],
        compiler_params=pltpu.CompilerParams(
            dimension_semantics=("parallel",), vmem_limit_bytes=VMEM_LIMIT),
        name="merge",
    )(x2, proj, proj, proj, proj, proj, y_gla, y_diff, pool_w, pool_scale, w_branch, w_o, gpost)


def _ffn_kernel(x_ref, gpre_ref, wg_ref, wu_ref, wd_ref, gpost_ref, o_ref):
    x = x_ref[...]
    h = _rms(x, gpre_ref[...]).astype(BF16)
    g = jnp.dot(h, wg_ref[...], preferred_element_type=F32)
    u = jnp.dot(h, wu_ref[...], preferred_element_type=F32)
    a = (g * jax.nn.sigmoid(g) * u).astype(BF16)
    f = jnp.dot(a, wd_ref[...], preferred_element_type=F32)
    o_ref[...] = x + _rms(f, gpost_ref[...])


def _ffn(x2, gpre, wg, wu, wd, gpost):
    n = x2.shape[0]
    tm = TM_FFN
    return pl.pallas_call(
        _ffn_kernel,
        out_shape=jax.ShapeDtypeStruct((n, D_MODEL), F32),
        grid=(n // tm,),
        in_specs=[
            pl.BlockSpec((tm, D_MODEL), lambda i: (i, 0)),
            _const_spec((1, D_MODEL)),
            _const_spec((D_MODEL, FFN_DIM)),
            _const_spec((D_MODEL, FFN_DIM)),
            _const_spec((FFN_DIM, D_MODEL)),
            _const_spec((1, D_MODEL)),
        ],
        out_specs=pl.BlockSpec((tm, D_MODEL), lambda i: (i, 0)),
        compiler_params=pltpu.CompilerParams(
            dimension_semantics=("parallel",), vmem_limit_bytes=VMEM_LIMIT),
        name="ffn",
    )(x2, gpre, wg, wu, wd, gpost)


def _split_w_in(w):
    z0 = int(sum(IN_SPLITS[:5]))
    assert z0 == W_HEAD and w.shape[1] - z0 - GLA_GATE_RANK == W_TAIL
    w_z = jnp.pad(w[:, z0:z0 + GLA_GATE_RANK], ((0, 0), (0, Z_PAD - GLA_GATE_RANK)))
    return w[:, :z0].astype(BF16), w_z.astype(BF16), w[:, z0 + GLA_GATE_RANK:].astype(BF16)


def kernel(x, rel_bias, ln_mix_pre, w_in, pool_w, pool_scale, gla_w_decay, gla_b_decay, gla_norm,
           diff_lambda, diff_norm, w_branch, w_o, ln_mix_post, ln_ffn_pre, ffn_w_gate, ffn_w_up,
           ffn_w_down, ln_ffn_post):
    batch, seq, d = x.shape
    depth = w_in.shape[0]
    assert d == D_MODEL and seq % TC_GLA == 0 and seq % T_ATT == 0 and seq % TM_MERGE == 0
    x2 = x.reshape(batch * seq, d)
    bias = _bias_tiles(rel_bias)
    for l in range(depth):
        lam_init = 0.8 - 0.6 * math.exp(-0.3 * l)
        proj = _in_proj(x2, ln_mix_pre[l][None, :], *_split_w_in(w_in[l]))
        w_decay = jnp.pad(gla_w_decay[l], ((0, Z_PAD - GLA_GATE_RANK), (0, 0))).astype(BF16)
        y_gla = _gla(proj, w_decay, gla_b_decay[l][None, :], gla_norm[l][None, :], batch, seq)
        y_diff = _diff(proj, diff_lambda[l], bias, diff_norm[l][:, None], lam_init, batch, seq)
        x2 = _merge(x2, proj, y_gla, y_diff, pool_w[l].astype(BF16), pool_scale[l][None, :],
                    w_branch[l].astype(BF16), w_o[l].astype(BF16), ln_mix_post[l][None, :], seq)
        x2 = _ffn(x2, ln_ffn_pre[l][None, :], ffn_w_gate[l].astype(BF16), ffn_w_up[l].astype(BF16),
                  ffn_w_down[l].astype(BF16), ln_ffn_post[l][None, :])
    return x2.reshape(batch, seq, d)
```

```python
import functools
import math

import jax
import jax.numpy as jnp
from jax import lax
from jax.experimental import pallas as pl
from jax.experimental.pallas import tpu as pltpu

F32 = jnp.float32
BF16 = jnp.bfloat16

D_MODEL = 1024
POOL_WINDOWS = (2, 4, 8, 16)
POOL_GROUP_DIM = 128
POOL_DIM = 512
GLA_HEADS = 4
GLA_DK = 64
GLA_DV = 128
GLA_QK_DIM = 256
GLA_V_DIM = 512
GLA_GATE_RANK = 16
GLA_TAU = 16.0
GLA_CHUNK = 64
DIFF_HEADS = 4
DIFF_DK = 64
DIFF_DV = 128
REL_BUCKETS = 32
REL_MAX_DIST = 128
BRANCH_DIM = 512
FFN_DIM = 2816
RMS_EPS = 1e-6
IN_SPLITS = (512, 256, 256, 512, 512, 16, 512, 512, 512, 3072)

LANES = 128
VMEM_LIMIT = 56 * 1024 * 1024

C_GATE = 0
C_U = 3072
C_GQ = 3584
C_GK = 3840
C_GV = 4096
C_GO = 4608
C_DQ = 5120
C_DK = 5632
C_DV = 6144
C_Z = 6656
Z_PAD = LANES
PROJ_DIM = C_Z + Z_PAD

TM_PROJ = 512
TM_MERGE = 256
TM_FFN = 512
TC_GLA = 512
GLA_BLOCK = 256
T_ATT = 256
POOL_HALO = 16


def _rms(x, gain):
    ms = jnp.mean(x * x, axis=-1, keepdims=True)
    return x * lax.rsqrt(ms + RMS_EPS) * gain


def _nt_dot(a, b):
    return lax.dot_general(a, b, (((1,), (1,)), ((), ())), preferred_element_type=F32)


def _tn_dot(a, b):
    return lax.dot_general(a, b, (((0,), (0,)), ((), ())), preferred_element_type=F32)


def _const_spec(shape):
    nd = len(shape)
    return pl.BlockSpec(shape, lambda *_: (0,) * nd, pipeline_mode=pl.Buffered(1))


def _layer_spec(layer, shape):
    nd = len(shape)
    return pl.BlockSpec((None,) + tuple(shape), lambda *_: (layer,) + (0,) * nd, pipeline_mode=pl.Buffered(1))


W_HEAD = C_DQ - C_U
W_TAIL = C_Z - C_DQ + C_U
N_DIFF = C_Z - C_DQ


def _in_proj_kernel(x_ref, g_ref, wh_ref, wz_ref, wt_ref, o_ref):
    h = _rms(x_ref[...], g_ref[...]).astype(BF16)
    step = 1024
    segments = ((wt_ref, N_DIFF, W_TAIL - N_DIFF, C_GATE), (wh_ref, 0, W_HEAD, C_U),
                (wt_ref, 0, N_DIFF, C_DQ), (wz_ref, 0, Z_PAD, C_Z))
    for w_ref, src, width, dst in segments:
        for c in range(0, width, step):
            w = min(step, width - c)
            o_ref[:, dst + c:dst + c + w] = jnp.dot(
                h, w_ref[:, src + c:src + c + w], preferred_element_type=F32).astype(BF16)


def _in_proj(layer, x2, gain, w_head, w_z, w_tail):
    n = x2.shape[0]
    return pl.pallas_call(
        _in_proj_kernel,
        out_shape=jax.ShapeDtypeStruct((n, PROJ_DIM), BF16),
        grid=(n // TM_PROJ,),
        in_specs=[
            pl.BlockSpec((TM_PROJ, D_MODEL), lambda i: (i, 0)),
            _layer_spec(layer, (1, D_MODEL)),
            _layer_spec(layer, (D_MODEL, W_HEAD)),
            _layer_spec(layer, (D_MODEL, Z_PAD)),
            _layer_spec(layer, (D_MODEL, W_TAIL)),
        ],
        out_specs=pl.BlockSpec((TM_PROJ, PROJ_DIM), lambda i: (i, 0)),
        compiler_params=pltpu.CompilerParams(
            dimension_semantics=("parallel",), vmem_limit_bytes=VMEM_LIMIT),
        name="in_proj",
    )(x2, gain, w_head, w_z, w_tail)


def _log_sigmoid(x):
    return jnp.minimum(x, 0.0) - jnp.log1p(jnp.exp(-jnp.abs(x)))


def _gla_kernel(q_ref, k_ref, v_ref, go_ref, z_ref, wd_ref, bd_ref, gn_ref, y_ref, st_ref):
    n_chunks = TC_GLA // GLA_CHUNK
    chunks_per_block = GLA_BLOCK // GLA_CHUNK

    @pl.when(pl.program_id(1) == 0)
    def _():
        st_ref[...] = jnp.zeros_like(st_ref)

    def pair(h):
        return slice((h // 2) * LANES, (h // 2 + 1) * LANES)

    def vcols(h):
        return slice(h * GLA_DV, (h + 1) * GLA_DV)

    def crow(c):
        return slice(c * GLA_CHUNK, (c + 1) * GLA_CHUNK)

    zl = jnp.dot(z_ref[...], wd_ref[...], preferred_element_type=F32) + bd_ref[...]
    la = _log_sigmoid(zl) / GLA_TAU

    hi = la.astype(BF16)
    r1 = la - hi.astype(F32)
    mid = r1.astype(BF16)
    lo = (r1 - mid.astype(F32)).astype(BF16)
    pieces = jnp.concatenate([hi, mid, lo], axis=1)
    row = lax.broadcasted_iota(jnp.int32, (GLA_CHUNK, GLA_CHUNK), 0)
    col = lax.broadcasted_iota(jnp.int32, (GLA_CHUNK, GLA_CHUNK), 1)
    tril_bf = jnp.where(row >= col, 1.0, 0.0).astype(BF16)
    cums, totals, decay = [], [], []
    for c in range(n_chunks):
        cs = jnp.dot(tril_bf, pieces[crow(c), :], preferred_element_type=F32)
        cum_c = cs[:, :GLA_QK_DIM] + cs[:, GLA_QK_DIM:2 * GLA_QK_DIM] + cs[:, 2 * GLA_QK_DIM:]
        last = cum_c[GLA_CHUNK - 1:GLA_CHUNK, :]
        cums.append(cum_c)
        totals.append(jnp.broadcast_to(last, (GLA_CHUNK, GLA_QK_DIM)))
        decay.append(jnp.exp(last))
    cum = jnp.concatenate(cums, axis=0)
    total = jnp.concatenate(totals, axis=0)

    q = q_ref[...].astype(F32)
    k = k_ref[...].astype(F32)
    q_dec = (q * (GLA_DK ** -0.5) * jnp.exp(cum)).astype(BF16)
    k_dec = k * jnp.exp(-cum)
    k_end = k * jnp.exp(total - cum)
    lane = lax.broadcasted_iota(jnp.int32, (1, LANES), 1)
    head_mask = (jnp.where(lane < GLA_DK, 1.0, 0.0), jnp.where(lane >= GLA_DK, 1.0, 0.0))
    k_dec_h = [(k_dec[:, pair(h)] * head_mask[h % 2]).astype(BF16) for h in range(GLA_HEADS)]
    k_end_h = [(k_end[:, pair(h)] * head_mask[h % 2]).astype(BF16) for h in range(GLA_HEADS)]

    state_in = [[None] * GLA_HEADS for _ in range(n_chunks)]
    for h in range(GLA_HEADS):
        st = st_ref[h]
        for c in range(n_chunks):
            state_in[c][h] = st.astype(BF16)
            st = st * decay[c][:, pair(h)] + _tn_dot(v_ref[crow(c), vcols(h)], k_end_h[h][crow(c), :])
        st_ref[h] = st

    row = lax.broadcasted_iota(jnp.int32, (GLA_BLOCK, GLA_BLOCK), 0)
    col = lax.broadcasted_iota(jnp.int32, (GLA_BLOCK, GLA_BLOCK), 1)
    visible = (row >= col) & (row // GLA_CHUNK == col // GLA_CHUNK)
    gn = gn_ref[...]
    for b in range(TC_GLA // GLA_BLOCK):
        rb = slice(b * GLA_BLOCK, (b + 1) * GLA_BLOCK)
        for h in range(GLA_HEADS):
            scores = jnp.where(visible, _nt_dot(q_dec[rb, pair(h)], k_dec_h[h][rb, :]), 0.0).astype(BF16)
            o = jnp.dot(scores, v_ref[rb, vcols(h)], preferred_element_type=F32)
            o = o + jnp.concatenate(
                [_nt_dot(q_dec[crow(c), pair(h)], state_in[c][h])
                 for c in range(b * chunks_per_block, (b + 1) * chunks_per_block)], axis=0)
            g = go_ref[rb, vcols(h)].astype(F32)
            y_ref[rb, vcols(h)] = (_rms(o, gn) * (g * jax.nn.sigmoid(g))).astype(BF16)


def _gla(layer, proj, w_decay, b_decay, norm_gain, batch, seq):
    n = proj.shape[0]
    tps = seq // TC_GLA
    row = lambda b, t: b * tps + t
    return pl.pallas_call(
        _gla_kernel,
        out_shape=jax.ShapeDtypeStruct((n, GLA_V_DIM), BF16),
        grid=(batch, tps),
        in_specs=[
            pl.BlockSpec((TC_GLA, GLA_QK_DIM), lambda b, t: (row(b, t), C_GQ // GLA_QK_DIM)),
            pl.BlockSpec((TC_GLA, GLA_QK_DIM), lambda b, t: (row(b, t), C_GK // GLA_QK_DIM)),
            pl.BlockSpec((TC_GLA, GLA_V_DIM), lambda b, t: (row(b, t), C_GV // GLA_V_DIM)),
            pl.BlockSpec((TC_GLA, GLA_V_DIM), lambda b, t: (row(b, t), C_GO // GLA_V_DIM)),
            pl.BlockSpec((TC_GLA, Z_PAD), lambda b, t: (row(b, t), C_Z // Z_PAD)),
            _layer_spec(layer, (Z_PAD, GLA_QK_DIM)),
            _layer_spec(layer, (1, GLA_QK_DIM)),
            _layer_spec(layer, (1, GLA_DV)),
        ],
        out_specs=pl.BlockSpec((TC_GLA, GLA_V_DIM), lambda b, t: (row(b, t), 0)),
        scratch_shapes=[pltpu.VMEM((GLA_HEADS, GLA_DV, LANES), F32)],
        compiler_params=pltpu.CompilerParams(
            dimension_semantics=("parallel", "arbitrary"), vmem_limit_bytes=VMEM_LIMIT),
        name="gla",
    )(proj, proj, proj, proj, proj, w_decay, b_decay, norm_gain)


ONES_ROWS = 16
LOG2E = 1.4426950408889634


def _diff_kernel(lam_ref, q_ref, k_ref, v_ref, bias_ref, gn_ref, y_ref,
                 qs_ref, vt_ref, s0_ref, s1_ref, mx0_ref, mx1_ref, m_ref, acc_ref, *, lam_init):
    i = pl.program_id(1)
    t = T_ATT
    lane = lax.broadcasted_iota(jnp.int32, (1, LANES), 1)
    for h in range(DIFF_HEADS):
        q = q_ref[:, h * LANES:(h + 1) * LANES].astype(F32) * (DIFF_DK ** -0.5 * LOG2E)
        qs_ref[h, 0:t, :] = jnp.where(lane < DIFF_DK, q, 0.0).astype(BF16)
        qs_ref[h, t:2 * t, :] = jnp.where(lane >= DIFF_DK, q, 0.0).astype(BF16)

    def rows(j):
        return pl.ds(pl.multiple_of(j * t, t), t)

    def head_cols(h):
        return slice(h * LANES, (h + 1) * LANES)

    m_ref[...] = jnp.full_like(m_ref, -1e30)
    acc_ref[...] = jnp.zeros_like(acc_ref)

    @pl.when(i == 0)
    def _():
        vt_ref[:, :, DIFF_DV:, :] = jnp.ones((DIFF_HEADS, vt_ref.shape[1], ONES_ROWS, t), BF16)

        def transpose_block(jb, carry):
            for h in range(DIFF_HEADS):
                vt_ref[h, jb, 0:DIFF_DV, :] = v_ref[rows(jb), head_cols(h)].T
            return carry

        lax.fori_loop(0, vt_ref.shape[1], transpose_block, 0)

    def logits(j, buf):
        s_ref, mx_ref = buf
        for h in range(DIFF_HEADS):
            s = _nt_dot(k_ref[rows(j), head_cols(h)], qs_ref[h])
            s_ref[h] = s
            mx_ref[h] = jnp.max(s, axis=0, keepdims=True)

    def softmax_pv(j, buf, bias_cols):
        s_ref, mx_ref = buf
        for h in range(DIFF_HEADS):
            s = s_ref[h]
            if bias_cols is None:
                s_max = mx_ref[h]
            else:
                bias = bias_ref[h, :, bias_cols]
                s = s + jnp.concatenate([bias, bias], axis=1)
                s_max = jnp.max(s, axis=0, keepdims=True)
            m_old = m_ref[h]
            m_new = jnp.maximum(m_old, s_max)
            alpha = jnp.exp2(m_old - m_new)
            p = jnp.exp2(s - m_new).astype(BF16)
            m_ref[h] = m_new
            acc_ref[h] = alpha * acc_ref[h] + jnp.dot(vt_ref[h, j], p, preferred_element_type=F32)

    s0 = (s0_ref, mx0_ref)
    s1 = (s1_ref, mx1_ref)
    n_far = jnp.maximum(i - 1, 0)
    peeled = n_far % 2

    @pl.when(i % 2 == 0)
    def _():
        logits(0, s0)

    @pl.when(i % 2 == 1)
    def _():
        logits(0, s1)

    @pl.when(peeled == 1)
    def _():
        logits(1, s1)
        softmax_pv(0, s0, None)

    def far_pair(jj, carry):
        j = peeled + 2 * jj
        logits(j + 1, s0)
        softmax_pv(j, s1, None)
        logits(j + 2, s1)
        softmax_pv(j + 1, s0, None)
        return carry

    lax.fori_loop(0, n_far // 2, far_pair, 0)

    @pl.when(i >= 1)
    def _():
        logits(i, s0)
        softmax_pv(i - 1, s1, slice(t, 2 * t))

    softmax_pv(i, s0, slice(0, t))

    lam_p = lam_ref[...]
    lam = (jnp.exp(jnp.sum(lam_p[0:1] * lam_p[1:2], axis=1, keepdims=True))
           - jnp.exp(jnp.sum(lam_p[2:3] * lam_p[3:4], axis=1, keepdims=True)) + lam_init)
    for h in range(DIFF_HEADS):
        acc = acc_ref[h, 0:DIFF_DV, :]
        inv_l = 1.0 / acc_ref[h, DIFF_DV:DIFF_DV + 1, :]
        o = acc[:, :t] * inv_l[:, :t] - lam * (acc[:, t:] * inv_l[:, t:])
        ms = jnp.mean(o * o, axis=0, keepdims=True)
        o = o * lax.rsqrt(ms + RMS_EPS) * gn_ref[...] * (1.0 - lam_init)
        y_ref[:, h * DIFF_DV:(h + 1) * DIFF_DV] = o.T.astype(BF16)


def _diff(layer, proj, lam, bias, norm_gain_col, lam_init, batch, seq):
    n = proj.shape[0]
    nq = seq // T_ATT
    hw = DIFF_HEADS * LANES
    return pl.pallas_call(
        functools.partial(_diff_kernel, lam_init=lam_init),
        out_shape=jax.ShapeDtypeStruct((n, DIFF_HEADS * DIFF_DV), BF16),
        grid=(batch, nq),
        in_specs=[
            _layer_spec(layer, (4, DIFF_DK)),
            pl.BlockSpec((T_ATT, hw), lambda b, i: (b * nq + i, C_DQ // hw)),
            pl.BlockSpec((seq, hw), lambda b, i: (b, C_DK // hw), pipeline_mode=pl.Buffered(1)),
            pl.BlockSpec((seq, hw), lambda b, i: (b, C_DV // hw), pipeline_mode=pl.Buffered(1)),
            _const_spec((DIFF_HEADS, T_ATT, 2 * T_ATT)),
            _layer_spec(layer, (DIFF_DV, 1)),
        ],
        out_specs=pl.BlockSpec((T_ATT, DIFF_HEADS * DIFF_DV), lambda b, i: (b * nq + i, 0)),
        scratch_shapes=[
            pltpu.VMEM((DIFF_HEADS, 2 * T_ATT, LANES), BF16),
            pltpu.VMEM((DIFF_HEADS, nq, DIFF_DV + ONES_ROWS, T_ATT), BF16),
            pltpu.VMEM((DIFF_HEADS, T_ATT, 2 * T_ATT), F32),
            pltpu.VMEM((DIFF_HEADS, T_ATT, 2 * T_ATT), F32),
            pltpu.VMEM((DIFF_HEADS, 1, 2 * T_ATT), F32),
            pltpu.VMEM((DIFF_HEADS, 1, 2 * T_ATT), F32),
            pltpu.VMEM((DIFF_HEADS, 1, 2 * T_ATT), F32),
            pltpu.VMEM((DIFF_HEADS, DIFF_DV + ONES_ROWS, 2 * T_ATT), F32),
        ],
        compiler_params=pltpu.CompilerParams(
            dimension_semantics=("parallel", "arbitrary"), vmem_limit_bytes=VMEM_LIMIT),
        name="diff_attn",
    )(lam, proj, proj, proj, bias, norm_gain_col)


def _t5_bucket(rel):
    n = jnp.maximum(rel, 0)
    max_exact = REL_BUCKETS // 2
    nf = jnp.maximum(n, 1).astype(F32)
    large = max_exact + (jnp.log(nf / max_exact) / math.log(REL_MAX_DIST / max_exact)
                         * (REL_BUCKETS - max_exact)).astype(jnp.int32)
    large = jnp.minimum(large, REL_BUCKETS - 1)
    return jnp.where(n < max_exact, n, large)


def _bias_tiles(rel_bias):
    assert T_ATT + 1 >= REL_MAX_DIST
    t = T_ATT
    table = rel_bias.astype(F32)
    near = (table[_t5_bucket(jnp.arange(2 * t, dtype=jnp.int32))] - table[REL_BUCKETS - 1]) * LOG2E
    by_rel = jnp.concatenate([near, jnp.full((t, DIFF_HEADS), -jnp.inf, F32)], axis=0).T
    period = 3 * t
    flat = jnp.tile(by_rel, (1, t))[:, :t * (period - 1)]
    return flat.reshape(DIFF_HEADS, t, period - 1)[:, :, :2 * t]


def _merge_kernel(x_ref, g0_ref, g1_ref, g2_ref, u_ref, halo_ref, ygla_ref, ydiff_ref,
                  pw_ref, ps_ref, wb_ref, wo_ref, gpost_ref, o_ref, ucat_ref, *, tiles_per_seq):
    i = pl.program_id(0)
    tm = TM_MERGE
    t_in_seq = i % tiles_per_seq
    halo = halo_ref[...].astype(F32)
    ucat_ref[0:POOL_HALO, :] = jnp.where(t_in_seq == 0, 0.0, halo)
    ucat_ref[POOL_HALO:, :] = u_ref[...].astype(F32)
    pos = t_in_seq * tm + lax.broadcasted_iota(jnp.int32, (tm, 1), 0)
    mixed = []
    for g, w in enumerate(POOL_WINDOWS):
        cols = slice(g * POOL_GROUP_DIM, (g + 1) * POOL_GROUP_DIM)
        cur = ucat_ref[POOL_HALO:POOL_HALO + tm, cols]
        win = cur
        for s in range(1, w):
            win = win + ucat_ref[POOL_HALO - s:POOL_HALO - s + tm, cols]
        cnt = jnp.minimum(pos + 1, w).astype(F32)
        pooled = win / cnt - cur
        mixed.append(jnp.dot(pooled.astype(BF16), pw_ref[g], preferred_element_type=F32))
    y_pool = (jnp.concatenate(mixed, axis=1) * ps_ref[...]).astype(BF16)

    merged = jax.nn.sigmoid(g0_ref[...].astype(F32)) * jnp.dot(y_pool, wb_ref[0], preferred_element_type=F32)
    merged = merged + jax.nn.sigmoid(g1_ref[...].astype(F32)) * jnp.dot(
        ygla_ref[...], wb_ref[1], preferred_element_type=F32)
    merged = merged + jax.nn.sigmoid(g2_ref[...].astype(F32)) * jnp.dot(
        ydiff_ref[...], wb_ref[2], preferred_element_type=F32)
    out = jnp.dot(merged.astype(BF16), wo_ref[...], preferred_element_type=F32)
    o_ref[...] = x_ref[...] + _rms(out, gpost_ref[...])


def _merge(layer, x2, proj, y_gla, y_diff, pool_w, pool_scale, w_branch, w_o, gpost, seq):
    n = x2.shape[0]
    tm = TM_MERGE
    halo_blocks = tm // POOL_HALO
    return pl.pallas_call(
        functools.partial(_merge_kernel, tiles_per_seq=seq // tm),
        out_shape=jax.ShapeDtypeStruct((n, D_MODEL), F32),
        grid=(n // tm,),
        in_specs=[
            pl.BlockSpec((tm, D_MODEL), lambda i: (i, 0)),
            pl.BlockSpec((tm, D_MODEL), lambda i: (i, C_GATE // D_MODEL + 0)),
            pl.BlockSpec((tm, D_MODEL), lambda i: (i, C_GATE // D_MODEL + 1)),
            pl.BlockSpec((tm, D_MODEL), lambda i: (i, C_GATE // D_MODEL + 2)),
            pl.BlockSpec((tm, POOL_DIM), lambda i: (i, C_U // POOL_DIM)),
            pl.BlockSpec((POOL_HALO, POOL_DIM),
                         lambda i: (jnp.maximum(i * halo_blocks - 1, 0), C_U // POOL_DIM)),
            pl.BlockSpec((tm, BRANCH_DIM), lambda i: (i, 0)),
            pl.BlockSpec((tm, BRANCH_DIM), lambda i: (i, 0)),
            _layer_spec(layer, (len(POOL_WINDOWS), POOL_GROUP_DIM, POOL_GROUP_DIM)),
            _layer_spec(layer, (1, POOL_DIM)),
            _layer_spec(layer, (3, BRANCH_DIM, D_MODEL)),
            _layer_spec(layer, (D_MODEL, D_MODEL)),
            _layer_spec(layer, (1, D_MODEL)),
        ],
        out_specs=pl.BlockSpec((tm, D_MODEL), lambda i: (i, 0)),
        scratch_shapes=[pltpu.VMEM((tm + POOL_HALO, POOL_DIM), F32)],
        compiler_params=pltpu.CompilerParams(
            dimension_semantics=("parallel",), vmem_limit_bytes=VMEM_LIMIT),
        name="merge",
    )(x2, proj, proj, proj, proj, proj, y_gla, y_diff, pool_w, pool_scale, w_branch, w_o, gpost)


def _ffn_kernel(x_ref, gpre_ref, wg_ref, wu_ref, wd_ref, gpost_ref, o_ref):
    x = x_ref[...]
    h = _rms(x, gpre_ref[...]).astype(BF16)
    g = jnp.dot(h, wg_ref[...], preferred_element_type=F32)
    u = jnp.dot(h, wu_ref[...], preferred_element_type=F32)
    a = (g * jax.nn.sigmoid(g) * u).astype(BF16)
    f = jnp.dot(a, wd_ref[...], preferred_element_type=F32)
    o_ref[...] = x + _rms(f, gpost_ref[...])


def _ffn(layer, x2, gpre, wg, wu, wd, gpost):
    n = x2.shape[0]
    tm = TM_FFN
    return pl.pallas_call(
        _ffn_kernel,
        out_shape=jax.ShapeDtypeStruct((n, D_MODEL), F32),
        grid=(n // tm,),
        in_specs=[
            pl.BlockSpec((tm, D_MODEL), lambda i: (i, 0)),
            _layer_spec(layer, (1, D_MODEL)),
            _layer_spec(layer, (D_MODEL, FFN_DIM)),
            _layer_spec(layer, (D_MODEL, FFN_DIM)),
            _layer_spec(layer, (FFN_DIM, D_MODEL)),
            _layer_spec(layer, (1, D_MODEL)),
        ],
        out_specs=pl.BlockSpec((tm, D_MODEL), lambda i: (i, 0)),
        compiler_params=pltpu.CompilerParams(
            dimension_semantics=("parallel",), vmem_limit_bytes=VMEM_LIMIT),
        name="ffn",
    )(x2, gpre, wg, wu, wd, gpost)


def _split_w_in(w):
    z0 = int(sum(IN_SPLITS[:5]))
    assert z0 == W_HEAD and w.shape[-1] - z0 - GLA_GATE_RANK == W_TAIL
    w_z = jnp.pad(w[..., z0:z0 + GLA_GATE_RANK], ((0, 0), (0, 0), (0, Z_PAD - GLA_GATE_RANK)))
    return w[..., :z0].astype(BF16), w_z.astype(BF16), w[..., z0 + GLA_GATE_RANK:].astype(BF16)


def kernel(x, rel_bias, ln_mix_pre, w_in, pool_w, pool_scale, gla_w_decay, gla_b_decay, gla_norm,
           diff_lambda, diff_norm, w_branch, w_o, ln_mix_post, ln_ffn_pre, ffn_w_gate, ffn_w_up,
           ffn_w_down, ln_ffn_post):
    batch, seq, d = x.shape
    depth = w_in.shape[0]
    assert d == D_MODEL and seq % TC_GLA == 0 and seq % T_ATT == 0 and seq % TM_MERGE == 0
    x2 = x.reshape(batch * seq, d)
    bias = _bias_tiles(rel_bias)
    row = lambda p: p[:, None, :]
    w_head, w_z, w_tail = _split_w_in(w_in)
    w_decay = jnp.pad(gla_w_decay, ((0, 0), (0, Z_PAD - GLA_GATE_RANK), (0, 0))).astype(BF16)
    pool_w, w_branch, w_o = pool_w.astype(BF16), w_branch.astype(BF16), w_o.astype(BF16)
    ffn_w_gate, ffn_w_up, ffn_w_down = ffn_w_gate.astype(BF16), ffn_w_up.astype(BF16), ffn_w_down.astype(BF16)
    for l in range(depth):
        lam_init = 0.8 - 0.6 * math.exp(-0.3 * l)
        proj = _in_proj(l, x2, row(ln_mix_pre), w_head, w_z, w_tail)
        y_gla = _gla(l, proj, w_decay, row(gla_b_decay), row(gla_norm), batch, seq)
        y_diff = _diff(l, proj, diff_lambda, bias, diff_norm[:, :, None], lam_init, batch, seq)
        x2 = _merge(l, x2, proj, y_gla, y_diff, pool_w, row(pool_scale), w_branch, w_o, row(ln_mix_post), seq)
        x2 = _ffn(l, x2, row(ln_ffn_pre), ffn_w_gate, ffn_w_up, ffn_w_down, row(ln_ffn_post))
    return x2.reshape(batch, seq, d)
```

```python
import functools
import math

import jax
import jax.numpy as jnp
from jax import lax
from jax.experimental import pallas as pl
from jax.experimental.pallas import tpu as pltpu

F32 = jnp.float32
BF16 = jnp.bfloat16

D_MODEL = 1024
POOL_WINDOWS = (2, 4, 8, 16)
POOL_GROUP_DIM = 128
POOL_DIM = 512
GLA_HEADS = 4
GLA_DK = 64
GLA_DV = 128
GLA_QK_DIM = 256
GLA_V_DIM = 512
GLA_GATE_RANK = 16
GLA_TAU = 16.0
GLA_CHUNK = 64
DIFF_HEADS = 4
DIFF_DK = 64
DIFF_DV = 128
REL_BUCKETS = 32
REL_MAX_DIST = 128
BRANCH_DIM = 512
FFN_DIM = 2816
RMS_EPS = 1e-6
IN_SPLITS = (512, 256, 256, 512, 512, 16, 512, 512, 512, 3072)

LANES = 128
VMEM_LIMIT = 56 * 1024 * 1024

C_GATE = 0
C_U = 3072
C_GQ = 3584
C_GK = 3840
C_GV = 4096
C_GO = 4608
C_DQ = 5120
C_DK = 5632
C_DV = 6144
C_Z = 6656
Z_PAD = LANES
PROJ_DIM = C_Z + Z_PAD

TM_PROJ = 512
TM_MERGE = 256
TM_FFN = 512
TC_GLA = 512
GLA_BLOCK = 256
T_ATT = 256
POOL_HALO = 16


def _rms(x, gain):
    ms = jnp.mean(x * x, axis=-1, keepdims=True)
    return x * lax.rsqrt(ms + RMS_EPS) * gain


def _nt_dot(a, b):
    return lax.dot_general(a, b, (((1,), (1,)), ((), ())), preferred_element_type=F32)


def _tn_dot(a, b):
    return lax.dot_general(a, b, (((0,), (0,)), ((), ())), preferred_element_type=F32)


def _const_spec(shape):
    nd = len(shape)
    return pl.BlockSpec(shape, lambda *_: (0,) * nd, pipeline_mode=pl.Buffered(1))


def _layer_spec(layer, shape):
    nd = len(shape)
    return pl.BlockSpec((None,) + tuple(shape), lambda *_: (layer,) + (0,) * nd, pipeline_mode=pl.Buffered(1))


W_HEAD = C_DQ - C_U
W_TAIL = C_Z - C_DQ + C_U
N_DIFF = C_Z - C_DQ


def _in_proj_kernel(x_ref, g_ref, wh_ref, wz_ref, wt_ref, o_ref):
    h = _rms(x_ref[...], g_ref[...]).astype(BF16)
    step = 1024
    segments = ((wt_ref, N_DIFF, W_TAIL - N_DIFF, C_GATE), (wh_ref, 0, W_HEAD, C_U),
                (wt_ref, 0, N_DIFF, C_DQ), (wz_ref, 0, Z_PAD, C_Z))
    for w_ref, src, width, dst in segments:
        for c in range(0, width, step):
            w = min(step, width - c)
            o_ref[:, dst + c:dst + c + w] = jnp.dot(
                h, w_ref[:, src + c:src + c + w], preferred_element_type=F32).astype(BF16)


def _in_proj(layer, x2, gain, w_head, w_z, w_tail):
    n = x2.shape[0]
    return pl.pallas_call(
        _in_proj_kernel,
        out_shape=jax.ShapeDtypeStruct((n, PROJ_DIM), BF16),
        grid=(n // TM_PROJ,),
        in_specs=[
            pl.BlockSpec((TM_PROJ, D_MODEL), lambda i: (i, 0)),
            _layer_spec(layer, (1, D_MODEL)),
            _layer_spec(layer, (D_MODEL, W_HEAD)),
            _layer_spec(layer, (D_MODEL, Z_PAD)),
            _layer_spec(layer, (D_MODEL, W_TAIL)),
        ],
        out_specs=pl.BlockSpec((TM_PROJ, PROJ_DIM), lambda i: (i, 0)),
        compiler_params=pltpu.CompilerParams(
            dimension_semantics=("parallel",), vmem_limit_bytes=VMEM_LIMIT),
        name="in_proj",
    )(x2, gain, w_head, w_z, w_tail)


def _log_sigmoid(x):
    return jnp.minimum(x, 0.0) - jnp.log1p(jnp.exp(-jnp.abs(x)))


def _gla_kernel(q_ref, k_ref, v_ref, go_ref, z_ref, wd_ref, bd_ref, gn_ref, y_ref, st_ref):
    n_chunks = TC_GLA // GLA_CHUNK
    chunks_per_block = GLA_BLOCK // GLA_CHUNK

    @pl.when(pl.program_id(1) == 0)
    def _():
        st_ref[...] = jnp.zeros_like(st_ref)

    def pair(h):
        return slice((h // 2) * LANES, (h // 2 + 1) * LANES)

    def vcols(h):
        return slice(h * GLA_DV, (h + 1) * GLA_DV)

    def crow(c):
        return slice(c * GLA_CHUNK, (c + 1) * GLA_CHUNK)

    zl = jnp.dot(z_ref[...], wd_ref[...], preferred_element_type=F32) + bd_ref[...]
    la = _log_sigmoid(zl) / GLA_TAU

    hi = la.astype(BF16)
    r1 = la - hi.astype(F32)
    mid = r1.astype(BF16)
    lo = (r1 - mid.astype(F32)).astype(BF16)
    pieces = jnp.concatenate([hi, mid, lo], axis=1)
    row = lax.broadcasted_iota(jnp.int32, (GLA_CHUNK, GLA_CHUNK), 0)
    col = lax.broadcasted_iota(jnp.int32, (GLA_CHUNK, GLA_CHUNK), 1)
    tril_bf = jnp.where(row >= col, 1.0, 0.0).astype(BF16)
    cums, totals, decay = [], [], []
    for c in range(n_chunks):
        cs = jnp.dot(tril_bf, pieces[crow(c), :], preferred_element_type=F32)
        cum_c = cs[:, :GLA_QK_DIM] + cs[:, GLA_QK_DIM:2 * GLA_QK_DIM] + cs[:, 2 * GLA_QK_DIM:]
        last = cum_c[GLA_CHUNK - 1:GLA_CHUNK, :]
        cums.append(cum_c)
        totals.append(jnp.broadcast_to(last, (GLA_CHUNK, GLA_QK_DIM)))
        decay.append(jnp.exp(last))
    cum = jnp.concatenate(cums, axis=0)
    total = jnp.concatenate(totals, axis=0)

    q = q_ref[...].astype(F32)
    k = k_ref[...].astype(F32)
    q_dec = (q * (GLA_DK ** -0.5) * jnp.exp(cum)).astype(BF16)
    k_dec = k * jnp.exp(-cum)
    k_end = k * jnp.exp(total - cum)
    lane = lax.broadcasted_iota(jnp.int32, (1, LANES), 1)
    head_mask = (jnp.where(lane < GLA_DK, 1.0, 0.0), jnp.where(lane >= GLA_DK, 1.0, 0.0))
    k_dec_h = [(k_dec[:, pair(h)] * head_mask[h % 2]).astype(BF16) for h in range(GLA_HEADS)]
    k_end_h = [(k_end[:, pair(h)] * head_mask[h % 2]).astype(BF16) for h in range(GLA_HEADS)]

    state_in = [[None] * GLA_HEADS for _ in range(n_chunks)]
    for h in range(GLA_HEADS):
        st = st_ref[h]
        for c in range(n_chunks):
            state_in[c][h] = st.astype(BF16)
            st = st * decay[c][:, pair(h)] + _tn_dot(v_ref[crow(c), vcols(h)], k_end_h[h][crow(c), :])
        st_ref[h] = st

    row = lax.broadcasted_iota(jnp.int32, (GLA_BLOCK, GLA_BLOCK), 0)
    col = lax.broadcasted_iota(jnp.int32, (GLA_BLOCK, GLA_BLOCK), 1)
    visible = (row >= col) & (row // GLA_CHUNK == col // GLA_CHUNK)
    gn = gn_ref[...]
    for b in range(TC_GLA // GLA_BLOCK):
        rb = slice(b * GLA_BLOCK, (b + 1) * GLA_BLOCK)
        for h in range(GLA_HEADS):
            scores = jnp.where(visible, _nt_dot(q_dec[rb, pair(h)], k_dec_h[h][rb, :]), 0.0).astype(BF16)
            o = jnp.dot(scores, v_ref[rb, vcols(h)], preferred_element_type=F32)
            o = o + jnp.concatenate(
                [_nt_dot(q_dec[crow(c), pair(h)], state_in[c][h])
                 for c in range(b * chunks_per_block, (b + 1) * chunks_per_block)], axis=0)
            g = go_ref[rb, vcols(h)].astype(F32)
            y_ref[rb, vcols(h)] = (_rms(o, gn) * (g * jax.nn.sigmoid(g))).astype(BF16)


def _gla(layer, proj, w_decay, b_decay, norm_gain, batch, seq):
    n = proj.shape[0]
    tps = seq // TC_GLA
    row = lambda b, t: b * tps + t
    return pl.pallas_call(
        _gla_kernel,
        out_shape=jax.ShapeDtypeStruct((n, GLA_V_DIM), BF16),
        grid=(batch, tps),
        in_specs=[
            pl.BlockSpec((TC_GLA, GLA_QK_DIM), lambda b, t: (row(b, t), C_GQ // GLA_QK_DIM)),
            pl.BlockSpec((TC_GLA, GLA_QK_DIM), lambda b, t: (row(b, t), C_GK // GLA_QK_DIM)),
            pl.BlockSpec((TC_GLA, GLA_V_DIM), lambda b, t: (row(b, t), C_GV // GLA_V_DIM)),
            pl.BlockSpec((TC_GLA, GLA_V_DIM), lambda b, t: (row(b, t), C_GO // GLA_V_DIM)),
            pl.BlockSpec((TC_GLA, Z_PAD), lambda b, t: (row(b, t), C_Z // Z_PAD)),
            _layer_spec(layer, (Z_PAD, GLA_QK_DIM)),
            _layer_spec(layer, (1, GLA_QK_DIM)),
            _layer_spec(layer, (1, GLA_DV)),
        ],
        out_specs=pl.BlockSpec((TC_GLA, GLA_V_DIM), lambda b, t: (row(b, t), 0)),
        scratch_shapes=[pltpu.VMEM((GLA_HEADS, GLA_DV, LANES), F32)],
        compiler_params=pltpu.CompilerParams(
            dimension_semantics=("parallel", "arbitrary"), vmem_limit_bytes=VMEM_LIMIT),
        name="gla",
    )(proj, proj, proj, proj, proj, w_decay, b_decay, norm_gain)


ONES_ROWS = 16
LOG2E = 1.4426950408889634


def _diff_kernel(lam_ref, q_ref, k_ref, v_ref, bias_ref, gn_ref, y_ref,
                 qs_ref, vt_ref, s0_ref, s1_ref, mx0_ref, mx1_ref, m_ref, acc_ref, *, lam_init):
    i = pl.program_id(1)
    t = T_ATT
    lane = lax.broadcasted_iota(jnp.int32, (1, LANES), 1)
    for h in range(DIFF_HEADS):
        q = q_ref[:, h * LANES:(h + 1) * LANES].astype(F32) * (DIFF_DK ** -0.5 * LOG2E)
        qs_ref[h, 0:t, :] = jnp.where(lane < DIFF_DK, q, 0.0).astype(BF16)
        qs_ref[h, t:2 * t, :] = jnp.where(lane >= DIFF_DK, q, 0.0).astype(BF16)

    def rows(j):
        return pl.ds(pl.multiple_of(j * t, t), t)

    def head_cols(h):
        return slice(h * LANES, (h + 1) * LANES)

    m_ref[...] = jnp.full_like(m_ref, -1e30)
    acc_ref[...] = jnp.zeros_like(acc_ref)

    @pl.when(i == 0)
    def _():
        vt_ref[:, :, DIFF_DV:, :] = jnp.ones((DIFF_HEADS, vt_ref.shape[1], ONES_ROWS, t), BF16)

        def transpose_block(jb, carry):
            for h in range(DIFF_HEADS):
                vt_ref[h, jb, 0:DIFF_DV, :] = v_ref[rows(jb), head_cols(h)].T
            return carry

        lax.fori_loop(0, vt_ref.shape[1], transpose_block, 0)

    def logits(j, buf):
        s_ref, mx_ref = buf
        for h in range(DIFF_HEADS):
            s = _nt_dot(k_ref[rows(j), head_cols(h)], qs_ref[h])
            s_ref[h] = s
            mx_ref[h] = jnp.max(s, axis=0, keepdims=True)

    def softmax_pv(j, buf, bias_cols):
        s_ref, mx_ref = buf
        for h in range(DIFF_HEADS):
            s = s_ref[h]
            if bias_cols is None:
                s_max = mx_ref[h]
            else:
                bias = bias_ref[h, :, bias_cols]
                s = s + jnp.concatenate([bias, bias], axis=1)
                s_max = jnp.max(s, axis=0, keepdims=True)
            m_old = m_ref[h]
            m_new = jnp.maximum(m_old, s_max)
            alpha = jnp.exp2(m_old - m_new)
            p = jnp.exp2(s - m_new).astype(BF16)
            m_ref[h] = m_new
            acc_ref[h] = alpha * acc_ref[h] + jnp.dot(vt_ref[h, j], p, preferred_element_type=F32)

    s0 = (s0_ref, mx0_ref)
    s1 = (s1_ref, mx1_ref)
    n_far = jnp.maximum(i - 1, 0)
    peeled = n_far % 2

    @pl.when(i % 2 == 0)
    def _():
        logits(0, s0)

    @pl.when(i % 2 == 1)
    def _():
        logits(0, s1)

    @pl.when(peeled == 1)
    def _():
        logits(1, s1)
        softmax_pv(0, s0, None)

    def far_pair(j):
        logits(j + 1, s0)
        softmax_pv(j, s1, None)
        logits(j + 2, s1)
        softmax_pv(j + 1, s0, None)

    n_pairs = n_far // 2
    peeled_pair = n_pairs % 2

    @pl.when(peeled_pair == 1)
    def _():
        far_pair(peeled)

    def far_quad(jj, carry):
        j = peeled + 2 * peeled_pair + 4 * jj
        far_pair(j)
        far_pair(j + 2)
        return carry

    lax.fori_loop(0, n_pairs // 2, far_quad, 0)

    @pl.when(i >= 1)
    def _():
        logits(i, s0)
        softmax_pv(i - 1, s1, slice(t, 2 * t))

    softmax_pv(i, s0, slice(0, t))

    lam_p = lam_ref[...]
    lam = (jnp.exp(jnp.sum(lam_p[0:1] * lam_p[1:2], axis=1, keepdims=True))
           - jnp.exp(jnp.sum(lam_p[2:3] * lam_p[3:4], axis=1, keepdims=True)) + lam_init)
    for h in range(DIFF_HEADS):
        acc = acc_ref[h, 0:DIFF_DV, :]
        inv_l = 1.0 / acc_ref[h, DIFF_DV:DIFF_DV + 1, :]
        o = acc[:, :t] * inv_l[:, :t] - lam * (acc[:, t:] * inv_l[:, t:])
        ms = jnp.mean(o * o, axis=0, keepdims=True)
        o = o * lax.rsqrt(ms + RMS_EPS) * gn_ref[...] * (1.0 - lam_init)
        y_ref[:, h * DIFF_DV:(h + 1) * DIFF_DV] = o.T.astype(BF16)


def _diff(layer, proj, lam, bias, norm_gain_col, lam_init, batch, seq):
    n = proj.shape[0]
    nq = seq // T_ATT
    hw = DIFF_HEADS * LANES
    return pl.pallas_call(
        functools.partial(_diff_kernel, lam_init=lam_init),
        out_shape=jax.ShapeDtypeStruct((n, DIFF_HEADS * DIFF_DV), BF16),
        grid=(batch, nq),
        in_specs=[
            _layer_spec(layer, (4, DIFF_DK)),
            pl.BlockSpec((T_ATT, hw), lambda b, i: (b * nq + i, C_DQ // hw)),
            pl.BlockSpec((seq, hw), lambda b, i: (b, C_DK // hw), pipeline_mode=pl.Buffered(1)),
            pl.BlockSpec((seq, hw), lambda b, i: (b, C_DV // hw), pipeline_mode=pl.Buffered(1)),
            _const_spec((DIFF_HEADS, T_ATT, 2 * T_ATT)),
            _layer_spec(layer, (DIFF_DV, 1)),
        ],
        out_specs=pl.BlockSpec((T_ATT, DIFF_HEADS * DIFF_DV), lambda b, i: (b * nq + i, 0)),
        scratch_shapes=[
            pltpu.VMEM((DIFF_HEADS, 2 * T_ATT, LANES), BF16),
            pltpu.VMEM((DIFF_HEADS, nq, DIFF_DV + ONES_ROWS, T_ATT), BF16),
            pltpu.VMEM((DIFF_HEADS, T_ATT, 2 * T_ATT), F32),
            pltpu.VMEM((DIFF_HEADS, T_ATT, 2 * T_ATT), F32),
            pltpu.VMEM((DIFF_HEADS, 1, 2 * T_ATT), F32),
            pltpu.VMEM((DIFF_HEADS, 1, 2 * T_ATT), F32),
            pltpu.VMEM((DIFF_HEADS, 1, 2 * T_ATT), F32),
            pltpu.VMEM((DIFF_HEADS, DIFF_DV + ONES_ROWS, 2 * T_ATT), F32),
        ],
        compiler_params=pltpu.CompilerParams(
            dimension_semantics=("parallel", "arbitrary"), vmem_limit_bytes=VMEM_LIMIT),
        name="diff_attn",
    )(lam, proj, proj, proj, bias, norm_gain_col)


def _t5_bucket(rel):
    n = jnp.maximum(rel, 0)
    max_exact = REL_BUCKETS // 2
    nf = jnp.maximum(n, 1).astype(F32)
    large = max_exact + (jnp.log(nf / max_exact) / math.log(REL_MAX_DIST / max_exact)
                         * (REL_BUCKETS - max_exact)).astype(jnp.int32)
    large = jnp.minimum(large, REL_BUCKETS - 1)
    return jnp.where(n < max_exact, n, large)


def _bias_tiles(rel_bias):
    assert T_ATT + 1 >= REL_MAX_DIST
    t = T_ATT
    table = rel_bias.astype(F32)
    near = (table[_t5_bucket(jnp.arange(2 * t, dtype=jnp.int32))] - table[REL_BUCKETS - 1]) * LOG2E
    by_rel = jnp.concatenate([near, jnp.full((t, DIFF_HEADS), -jnp.inf, F32)], axis=0).T
    period = 3 * t
    flat = jnp.tile(by_rel, (1, t))[:, :t * (period - 1)]
    return flat.reshape(DIFF_HEADS, t, period - 1)[:, :, :2 * t]


def _merge_kernel(x_ref, g0_ref, g1_ref, g2_ref, u_ref, halo_ref, ygla_ref, ydiff_ref,
                  pw_ref, ps_ref, wb_ref, wo_ref, gpost_ref, o_ref, ucat_ref, *, tiles_per_seq):
    i = pl.program_id(0)
    tm = TM_MERGE
    t_in_seq = i % tiles_per_seq
    halo = halo_ref[...].astype(F32)
    ucat_ref[0:POOL_HALO, :] = jnp.where(t_in_seq == 0, 0.0, halo)
    ucat_ref[POOL_HALO:, :] = u_ref[...].astype(F32)
    pos = t_in_seq * tm + lax.broadcasted_iota(jnp.int32, (tm, 1), 0)
    mixed = []
    for g, w in enumerate(POOL_WINDOWS):
        cols = slice(g * POOL_GROUP_DIM, (g + 1) * POOL_GROUP_DIM)
        cur = ucat_ref[POOL_HALO:POOL_HALO + tm, cols]
        win = cur
        for s in range(1, w):
            win = win + ucat_ref[POOL_HALO - s:POOL_HALO - s + tm, cols]
        cnt = jnp.minimum(pos + 1, w).astype(F32)
        pooled = win / cnt - cur
        mixed.append(jnp.dot(pooled.astype(BF16), pw_ref[g], preferred_element_type=F32))
    y_pool = (jnp.concatenate(mixed, axis=1) * ps_ref[...]).astype(BF16)

    merged = jax.nn.sigmoid(g0_ref[...].astype(F32)) * jnp.dot(y_pool, wb_ref[0], preferred_element_type=F32)
    merged = merged + jax.nn.sigmoid(g1_ref[...].astype(F32)) * jnp.dot(
        ygla_ref[...], wb_ref[1], preferred_element_type=F32)
    merged = merged + jax.nn.sigmoid(g2_ref[...].astype(F32)) * jnp.dot(
        ydiff_ref[...], wb_ref[2], preferred_element_type=F32)
    out = jnp.dot(merged.astype(BF16), wo_ref[...], preferred_element_type=F32)
    o_ref[...] = x_ref[...] + _rms(out, gpost_ref[...])


def _merge(layer, x2, proj, y_gla, y_diff, pool_w, pool_scale, w_branch, w_o, gpost, seq):
    n = x2.shape[0]
    tm = TM_MERGE
    halo_blocks = tm // POOL_HALO
    return pl.pallas_call(
        functools.partial(_merge_kernel, tiles_per_seq=seq // tm),
        out_shape=jax.ShapeDtypeStruct((n, D_MODEL), F32),
        grid=(n // tm,),
        in_specs=[
            pl.BlockSpec((tm, D_MODEL), lambda i: (i, 0)),
            pl.BlockSpec((tm, D_MODEL), lambda i: (i, C_GATE // D_MODEL + 0)),
            pl.BlockSpec((tm, D_MODEL), lambda i: (i, C_GATE // D_MODEL + 1)),
            pl.BlockSpec((tm, D_MODEL), lambda i: (i, C_GATE // D_MODEL + 2)),
            pl.BlockSpec((tm, POOL_DIM), lambda i: (i, C_U // POOL_DIM)),
            pl.BlockSpec((POOL_HALO, POOL_DIM),
                         lambda i: (jnp.maximum(i * halo_blocks - 1, 0), C_U // POOL_DIM)),
            pl.BlockSpec((tm, BRANCH_DIM), lambda i: (i, 0)),
            pl.BlockSpec((tm, BRANCH_DIM), lambda i: (i, 0)),
            _layer_spec(layer, (len(POOL_WINDOWS), POOL_GROUP_DIM, POOL_GROUP_DIM)),
            _layer_spec(layer, (1, POOL_DIM)),
            _layer_spec(layer, (3, BRANCH_DIM, D_MODEL)),
            _layer_spec(layer, (D_MODEL, D_MODEL)),
            _layer_spec(layer, (1, D_MODEL)),
        ],
        out_specs=pl.BlockSpec((tm, D_MODEL), lambda i: (i, 0)),
        scratch_shapes=[pltpu.VMEM((tm + POOL_HALO, POOL_DIM), F32)],
        compiler_params=pltpu.CompilerParams(
            dimension_semantics=("parallel",), vmem_limit_bytes=VMEM_LIMIT),
        name="merge",
    )(x2, proj, proj, proj, proj, proj, y_gla, y_diff, pool_w, pool_scale, w_branch, w_o, gpost)


def _ffn_kernel(x_ref, gpre_ref, wg_ref, wu_ref, wd_ref, gpost_ref, o_ref):
    x = x_ref[...]
    h = _rms(x, gpre_ref[...]).astype(BF16)
    g = jnp.dot(h, wg_ref[...], preferred_element_type=F32)
    u = jnp.dot(h, wu_ref[...], preferred_element_type=F32)
    a = (g * jax.nn.sigmoid(g) * u).astype(BF16)
    f = jnp.dot(a, wd_ref[...], preferred_element_type=F32)
    o_ref[...] = x + _rms(f, gpost_ref[...])


def _ffn(layer, x2, gpre, wg, wu, wd, gpost):
    n = x2.shape[0]
    tm = TM_FFN
    return pl.pallas_call(
        _ffn_kernel,
        out_shape=jax.ShapeDtypeStruct((n, D_MODEL), F32),
        grid=(n // tm,),
        in_specs=[
            pl.BlockSpec((tm, D_MODEL), lambda i: (i, 0)),
            _layer_spec(layer, (1, D_MODEL)),
            _layer_spec(layer, (D_MODEL, FFN_DIM)),
            _layer_spec(layer, (D_MODEL, FFN_DIM)),
            _layer_spec(layer, (FFN_DIM, D_MODEL)),
            _layer_spec(layer, (1, D_MODEL)),
        ],
        out_specs=pl.BlockSpec((tm, D_MODEL), lambda i: (i, 0)),
        compiler_params=pltpu.CompilerParams(
            dimension_semantics=("parallel",), vmem_limit_bytes=VMEM_LIMIT),
        name="ffn",
    )(x2, gpre, wg, wu, wd, gpost)


def _split_w_in(w):
    z0 = int(sum(IN_SPLITS[:5]))
    assert z0 == W_HEAD and w.shape[-1] - z0 - GLA_GATE_RANK == W_TAIL
    w_z = jnp.pad(w[..., z0:z0 + GLA_GATE_RANK], ((0, 0), (0, 0), (0, Z_PAD - GLA_GATE_RANK)))
    return w[..., :z0].astype(BF16), w_z.astype(BF16), w[..., z0 + GLA_GATE_RANK:].astype(BF16)


def kernel(x, rel_bias, ln_mix_pre, w_in, pool_w, pool_scale, gla_w_decay, gla_b_decay, gla_norm,
           diff_lambda, diff_norm, w_branch, w_o, ln_mix_post, ln_ffn_pre, ffn_w_gate, ffn_w_up,
           ffn_w_down, ln_ffn_post):
    batch, seq, d = x.shape
    depth = w_in.shape[0]
    assert d == D_MODEL and seq % TC_GLA == 0 and seq % T_ATT == 0 and seq % TM_MERGE == 0
    x2 = x.reshape(batch * seq, d)
    bias = _bias_tiles(rel_bias)
    row = lambda p: p[:, None, :]
    w_head, w_z, w_tail = _split_w_in(w_in)
    w_decay = jnp.pad(gla_w_decay, ((0, 0), (0, Z_PAD - GLA_GATE_RANK), (0, 0))).astype(BF16)
    pool_w, w_branch, w_o = pool_w.astype(BF16), w_branch.astype(BF16), w_o.astype(BF16)
    ffn_w_gate, ffn_w_up, ffn_w_down = ffn_w_gate.astype(BF16), ffn_w_up.astype(BF16), ffn_w_down.astype(BF16)
    for l in range(depth):
        lam_init = 0.8 - 0.6 * math.exp(-0.3 * l)
        proj = _in_proj(l, x2, row(ln_mix_pre), w_head, w_z, w_tail)
        y_gla = _gla(l, proj, w_decay, row(gla_b_decay), row(gla_norm), batch, seq)
        y_diff = _diff(l, proj, diff_lambda, bias, diff_norm[:, :, None], lam_init, batch, seq)
        x2 = _merge(l, x2, proj, y_gla, y_diff, pool_w, row(pool_scale), w_branch, w_o, row(ln_mix_post), seq)
        x2 = _ffn(l, x2, row(ln_ffn_pre), ffn_w_gate, ffn_w_up, ffn_w_down, row(ln_ffn_post))
    return x2.reshape(batch, seq, d)
```

```python
import functools
import math

import jax
import jax.numpy as jnp
from jax import lax
from jax.experimental import pallas as pl
from jax.experimental.pallas import tpu as pltpu

F32 = jnp.float32
BF16 = jnp.bfloat16

D_MODEL = 1024
POOL_WINDOWS = (2, 4, 8, 16)
POOL_GROUP_DIM = 128
POOL_DIM = 512
GLA_HEADS = 4
GLA_DK = 64
GLA_DV = 128
GLA_QK_DIM = 256
GLA_V_DIM = 512
GLA_GATE_RANK = 16
GLA_TAU = 16.0
GLA_CHUNK = 64
DIFF_HEADS = 4
DIFF_DK = 64
DIFF_DV = 128
REL_BUCKETS = 32
REL_MAX_DIST = 128
BRANCH_DIM = 512
FFN_DIM = 2816
RMS_EPS = 1e-6
IN_SPLITS = (512, 256, 256, 512, 512, 16, 512, 512, 512, 3072)

LANES = 128
VMEM_LIMIT = 56 * 1024 * 1024

C_GATE = 0
C_U = 3072
C_GQ = 3584
C_GK = 3840
C_GV = 4096
C_GO = 4608
C_DQ = 5120
C_DK = 5632
C_DV = 6144
C_Z = 6656
Z_PAD = LANES
PROJ_DIM = C_Z + Z_PAD

TM_PROJ = 512
TM_MERGE = 512
TM_FFN = 512
TC_GLA = 512
GLA_BLOCK = 256
T_ATT = 256
POOL_HALO = 16


def _rms(x, gain):
    ms = jnp.mean(x * x, axis=-1, keepdims=True)
    return x * lax.rsqrt(ms + RMS_EPS) * gain


def _nt_dot(a, b):
    return lax.dot_general(a, b, (((1,), (1,)), ((), ())), preferred_element_type=F32)


def _tn_dot(a, b):
    return lax.dot_general(a, b, (((0,), (0,)), ((), ())), preferred_element_type=F32)


def _const_spec(shape):
    nd = len(shape)
    return pl.BlockSpec(shape, lambda *_: (0,) * nd, pipeline_mode=pl.Buffered(1))


def _layer_spec(layer, shape):
    nd = len(shape)
    return pl.BlockSpec((None,) + tuple(shape), lambda *_: (layer,) + (0,) * nd, pipeline_mode=pl.Buffered(1))


W_HEAD = C_DQ - C_U
W_TAIL = C_Z - C_DQ + C_U
N_DIFF = C_Z - C_DQ


def _in_proj_kernel(x_ref, g_ref, wh_ref, wz_ref, wt_ref, o_ref):
    h = _rms(x_ref[...], g_ref[...]).astype(BF16)
    step = 1024
    segments = ((wt_ref, N_DIFF, W_TAIL - N_DIFF, C_GATE), (wh_ref, 0, W_HEAD, C_U),
                (wt_ref, 0, N_DIFF, C_DQ), (wz_ref, 0, Z_PAD, C_Z))
    for w_ref, src, width, dst in segments:
        for c in range(0, width, step):
            w = min(step, width - c)
            o_ref[:, dst + c:dst + c + w] = jnp.dot(
                h, w_ref[:, src + c:src + c + w], preferred_element_type=F32).astype(BF16)


def _in_proj(layer, x2, gain, w_head, w_z, w_tail):
    n = x2.shape[0]
    return pl.pallas_call(
        _in_proj_kernel,
        out_shape=jax.ShapeDtypeStruct((n, PROJ_DIM), BF16),
        grid=(n // TM_PROJ,),
        in_specs=[
            pl.BlockSpec((TM_PROJ, D_MODEL), lambda i: (i, 0)),
            _layer_spec(layer, (1, D_MODEL)),
            _layer_spec(layer, (D_MODEL, W_HEAD)),
            _layer_spec(layer, (D_MODEL, Z_PAD)),
            _layer_spec(layer, (D_MODEL, W_TAIL)),
        ],
        out_specs=pl.BlockSpec((TM_PROJ, PROJ_DIM), lambda i: (i, 0)),
        compiler_params=pltpu.CompilerParams(
            dimension_semantics=("parallel",), vmem_limit_bytes=VMEM_LIMIT),
        name="in_proj",
    )(x2, gain, w_head, w_z, w_tail)


def _log_sigmoid(x):
    return jnp.minimum(x, 0.0) - jnp.log1p(jnp.exp(-jnp.abs(x)))


def _gla_kernel(q_ref, k_ref, v_ref, go_ref, z_ref, wd_ref, bd_ref, gn_ref, y_ref, st_ref):
    n_chunks = TC_GLA // GLA_CHUNK
    chunks_per_block = GLA_BLOCK // GLA_CHUNK

    @pl.when(pl.program_id(1) == 0)
    def _():
        st_ref[...] = jnp.zeros_like(st_ref)

    def pair(h):
        return slice((h // 2) * LANES, (h // 2 + 1) * LANES)

    def vcols(h):
        return slice(h * GLA_DV, (h + 1) * GLA_DV)

    def crow(c):
        return slice(c * GLA_CHUNK, (c + 1) * GLA_CHUNK)

    zl = jnp.dot(z_ref[...], wd_ref[...], preferred_element_type=F32) + bd_ref[...]
    la = _log_sigmoid(zl) / GLA_TAU

    hi = la.astype(BF16)
    r1 = la - hi.astype(F32)
    mid = r1.astype(BF16)
    lo = (r1 - mid.astype(F32)).astype(BF16)
    pieces = jnp.concatenate([hi, mid, lo], axis=1)
    row = lax.broadcasted_iota(jnp.int32, (GLA_CHUNK, GLA_CHUNK), 0)
    col = lax.broadcasted_iota(jnp.int32, (GLA_CHUNK, GLA_CHUNK), 1)
    tril_bf = jnp.where(row >= col, 1.0, 0.0).astype(BF16)
    cums, totals, decay = [], [], []
    for c in range(n_chunks):
        cs = jnp.dot(tril_bf, pieces[crow(c), :], preferred_element_type=F32)
        cum_c = cs[:, :GLA_QK_DIM] + cs[:, GLA_QK_DIM:2 * GLA_QK_DIM] + cs[:, 2 * GLA_QK_DIM:]
        last = cum_c[GLA_CHUNK - 1:GLA_CHUNK, :]
        cums.append(cum_c)
        totals.append(jnp.broadcast_to(last, (GLA_CHUNK, GLA_QK_DIM)))
        decay.append(jnp.exp(last))
    cum = jnp.concatenate(cums, axis=0)
    total = jnp.concatenate(totals, axis=0)

    q = q_ref[...].astype(F32)
    k = k_ref[...].astype(F32)
    q_dec = (q * (GLA_DK ** -0.5) * jnp.exp(cum)).astype(BF16)
    k_dec = k * jnp.exp(-cum)
    k_end = k * jnp.exp(total - cum)
    lane = lax.broadcasted_iota(jnp.int32, (1, LANES), 1)
    head_mask = (jnp.where(lane < GLA_DK, 1.0, 0.0), jnp.where(lane >= GLA_DK, 1.0, 0.0))
    k_dec_h = [(k_dec[:, pair(h)] * head_mask[h % 2]).astype(BF16) for h in range(GLA_HEADS)]
    k_end_h = [(k_end[:, pair(h)] * head_mask[h % 2]).astype(BF16) for h in range(GLA_HEADS)]

    state_in = [[None] * GLA_HEADS for _ in range(n_chunks)]
    for h in range(GLA_HEADS):
        st = st_ref[h]
        for c in range(n_chunks):
            state_in[c][h] = st.astype(BF16)
            st = st * decay[c][:, pair(h)] + _tn_dot(v_ref[crow(c), vcols(h)], k_end_h[h][crow(c), :])
        st_ref[h] = st

    row = lax.broadcasted_iota(jnp.int32, (GLA_BLOCK, GLA_BLOCK), 0)
    col = lax.broadcasted_iota(jnp.int32, (GLA_BLOCK, GLA_BLOCK), 1)
    visible = (row >= col) & (row // GLA_CHUNK == col // GLA_CHUNK)
    gn = gn_ref[...]
    for b in range(TC_GLA // GLA_BLOCK):
        rb = slice(b * GLA_BLOCK, (b + 1) * GLA_BLOCK)
        for h in range(GLA_HEADS):
            scores = jnp.where(visible, _nt_dot(q_dec[rb, pair(h)], k_dec_h[h][rb, :]), 0.0).astype(BF16)
            o = jnp.dot(scores, v_ref[rb, vcols(h)], preferred_element_type=F32)
            o = o + jnp.concatenate(
                [_nt_dot(q_dec[crow(c), pair(h)], state_in[c][h])
                 for c in range(b * chunks_per_block, (b + 1) * chunks_per_block)], axis=0)
            g = go_ref[rb, vcols(h)].astype(F32)
            y_ref[rb, vcols(h)] = (_rms(o, gn) * (g * jax.nn.sigmoid(g))).astype(BF16)


def _gla(layer, proj, w_decay, b_decay, norm_gain, batch, seq):
    n = proj.shape[0]
    tps = seq // TC_GLA
    row = lambda b, t: b * tps + t
    return pl.pallas_call(
        _gla_kernel,
        out_shape=jax.ShapeDtypeStruct((n, GLA_V_DIM), BF16),
        grid=(batch, tps),
        in_specs=[
            pl.BlockSpec((TC_GLA, GLA_QK_DIM), lambda b, t: (row(b, t), C_GQ // GLA_QK_DIM)),
            pl.BlockSpec((TC_GLA, GLA_QK_DIM), lambda b, t: (row(b, t), C_GK // GLA_QK_DIM)),
            pl.BlockSpec((TC_GLA, GLA_V_DIM), lambda b, t: (row(b, t), C_GV // GLA_V_DIM)),
            pl.BlockSpec((TC_GLA, GLA_V_DIM), lambda b, t: (row(b, t), C_GO // GLA_V_DIM)),
            pl.BlockSpec((TC_GLA, Z_PAD), lambda b, t: (row(b, t), C_Z // Z_PAD)),
            _layer_spec(layer, (Z_PAD, GLA_QK_DIM)),
            _layer_spec(layer, (1, GLA_QK_DIM)),
            _layer_spec(layer, (1, GLA_DV)),
        ],
        out_specs=pl.BlockSpec((TC_GLA, GLA_V_DIM), lambda b, t: (row(b, t), 0)),
        scratch_shapes=[pltpu.VMEM((GLA_HEADS, GLA_DV, LANES), F32)],
        compiler_params=pltpu.CompilerParams(
            dimension_semantics=("parallel", "arbitrary"), vmem_limit_bytes=VMEM_LIMIT),
        name="gla",
    )(proj, proj, proj, proj, proj, w_decay, b_decay, norm_gain)


ONES_ROWS = 16
LOG2E = 1.4426950408889634


def _diff_kernel(lam_ref, q_ref, k_ref, v_ref, bias_ref, gn_ref, y_ref,
                 qs_ref, vt_ref, s0_ref, s1_ref, mx0_ref, mx1_ref, m_ref, acc_ref, *, lam_init):
    i = pl.program_id(1)
    t = T_ATT
    lane = lax.broadcasted_iota(jnp.int32, (1, LANES), 1)
    for h in range(DIFF_HEADS):
        q = q_ref[:, h * LANES:(h + 1) * LANES].astype(F32) * (DIFF_DK ** -0.5 * LOG2E)
        qs_ref[h, 0:t, :] = jnp.where(lane < DIFF_DK, q, 0.0).astype(BF16)
        qs_ref[h, t:2 * t, :] = jnp.where(lane >= DIFF_DK, q, 0.0).astype(BF16)

    def rows(j):
        return pl.ds(pl.multiple_of(j * t, t), t)

    def head_cols(h):
        return slice(h * LANES, (h + 1) * LANES)

    m_ref[...] = jnp.full_like(m_ref, -1e30)
    acc_ref[...] = jnp.zeros_like(acc_ref)

    @pl.when(i == 0)
    def _():
        vt_ref[:, :, DIFF_DV:, :] = jnp.ones((DIFF_HEADS, vt_ref.shape[1], ONES_ROWS, t), BF16)

        def transpose_block(jb, carry):
            for h in range(DIFF_HEADS):
                vt_ref[h, jb, 0:DIFF_DV, :] = v_ref[rows(jb), head_cols(h)].T
            return carry

        lax.fori_loop(0, vt_ref.shape[1], transpose_block, 0)

    def logits(j, buf):
        s_ref, mx_ref = buf
        for h in range(DIFF_HEADS):
            s = _nt_dot(k_ref[rows(j), head_cols(h)], qs_ref[h])
            s_ref[h] = s
            mx_ref[h] = jnp.max(s, axis=0, keepdims=True)

    def softmax_pv(j, buf, bias_cols):
        s_ref, mx_ref = buf
        for h in range(DIFF_HEADS):
            s = s_ref[h]
            if bias_cols is None:
                s_max = mx_ref[h]
            else:
                bias = bias_ref[h, :, bias_cols]
                s = s + jnp.concatenate([bias, bias], axis=1)
                s_max = jnp.max(s, axis=0, keepdims=True)
            m_old = m_ref[h]
            m_new = jnp.maximum(m_old, s_max)
            alpha = jnp.exp2(m_old - m_new)
            p = jnp.exp2(s - m_new).astype(BF16)
            m_ref[h] = m_new
            acc_ref[h] = alpha * acc_ref[h] + jnp.dot(vt_ref[h, j], p, preferred_element_type=F32)

    s0 = (s0_ref, mx0_ref)
    s1 = (s1_ref, mx1_ref)
    n_far = jnp.maximum(i - 1, 0)
    peeled = n_far % 2

    @pl.when(i % 2 == 0)
    def _():
        logits(0, s0)

    @pl.when(i % 2 == 1)
    def _():
        logits(0, s1)

    @pl.when(peeled == 1)
    def _():
        logits(1, s1)
        softmax_pv(0, s0, None)

    def far_pair(j):
        logits(j + 1, s0)
        softmax_pv(j, s1, None)
        logits(j + 2, s1)
        softmax_pv(j + 1, s0, None)

    n_pairs = n_far // 2
    peeled_pair = n_pairs % 2

    @pl.when(peeled_pair == 1)
    def _():
        far_pair(peeled)

    def far_quad(jj, carry):
        j = peeled + 2 * peeled_pair + 4 * jj
        far_pair(j)
        far_pair(j + 2)
        return carry

    lax.fori_loop(0, n_pairs // 2, far_quad, 0)

    @pl.when(i >= 1)
    def _():
        logits(i, s0)
        softmax_pv(i - 1, s1, slice(t, 2 * t))

    softmax_pv(i, s0, slice(0, t))

    lam_p = lam_ref[...]
    lam = (jnp.exp(jnp.sum(lam_p[0:1] * lam_p[1:2], axis=1, keepdims=True))
           - jnp.exp(jnp.sum(lam_p[2:3] * lam_p[3:4], axis=1, keepdims=True)) + lam_init)
    for h in range(DIFF_HEADS):
        acc = acc_ref[h, 0:DIFF_DV, :]
        inv_l = 1.0 / acc_ref[h, DIFF_DV:DIFF_DV + 1, :]
        o = acc[:, :t] * inv_l[:, :t] - lam * (acc[:, t:] * inv_l[:, t:])
        ms = jnp.mean(o * o, axis=0, keepdims=True)
        o = o * lax.rsqrt(ms + RMS_EPS) * gn_ref[...] * (1.0 - lam_init)
        y_ref[:, h * DIFF_DV:(h + 1) * DIFF_DV] = o.T.astype(BF16)


def _diff(layer, proj, lam, bias, norm_gain_col, lam_init, batch, seq):
    n = proj.shape[0]
    nq = seq // T_ATT
    hw = DIFF_HEADS * LANES
    return pl.pallas_call(
        functools.partial(_diff_kernel, lam_init=lam_init),
        out_shape=jax.ShapeDtypeStruct((n, DIFF_HEADS * DIFF_DV), BF16),
        grid=(batch, nq),
        in_specs=[
            _layer_spec(layer, (4, DIFF_DK)),
            pl.BlockSpec((T_ATT, hw), lambda b, i: (b * nq + i, C_DQ // hw)),
            pl.BlockSpec((seq, hw), lambda b, i: (b, C_DK // hw), pipeline_mode=pl.Buffered(1)),
            pl.BlockSpec((seq, hw), lambda b, i: (b, C_DV // hw), pipeline_mode=pl.Buffered(1)),
            _const_spec((DIFF_HEADS, T_ATT, 2 * T_ATT)),
            _layer_spec(layer, (DIFF_DV, 1)),
        ],
        out_specs=pl.BlockSpec((T_ATT, DIFF_HEADS * DIFF_DV), lambda b, i: (b * nq + i, 0)),
        scratch_shapes=[
            pltpu.VMEM((DIFF_HEADS, 2 * T_ATT, LANES), BF16),
            pltpu.VMEM((DIFF_HEADS, nq, DIFF_DV + ONES_ROWS, T_ATT), BF16),
            pltpu.VMEM((DIFF_HEADS, T_ATT, 2 * T_ATT), F32),
            pltpu.VMEM((DIFF_HEADS, T_ATT, 2 * T_ATT), F32),
            pltpu.VMEM((DIFF_HEADS, 1, 2 * T_ATT), F32),
            pltpu.VMEM((DIFF_HEADS, 1, 2 * T_ATT), F32),
            pltpu.VMEM((DIFF_HEADS, 1, 2 * T_ATT), F32),
            pltpu.VMEM((DIFF_HEADS, DIFF_DV + ONES_ROWS, 2 * T_ATT), F32),
        ],
        compiler_params=pltpu.CompilerParams(
            dimension_semantics=("parallel", "arbitrary"), vmem_limit_bytes=VMEM_LIMIT),
        name="diff_attn",
    )(lam, proj, proj, proj, bias, norm_gain_col)


def _t5_bucket(rel):
    n = jnp.maximum(rel, 0)
    max_exact = REL_BUCKETS // 2
    nf = jnp.maximum(n, 1).astype(F32)
    large = max_exact + (jnp.log(nf / max_exact) / math.log(REL_MAX_DIST / max_exact)
                         * (REL_BUCKETS - max_exact)).astype(jnp.int32)
    large = jnp.minimum(large, REL_BUCKETS - 1)
    return jnp.where(n < max_exact, n, large)


def _bias_tiles(rel_bias):
    assert T_ATT + 1 >= REL_MAX_DIST
    t = T_ATT
    table = rel_bias.astype(F32)
    near = (table[_t5_bucket(jnp.arange(2 * t, dtype=jnp.int32))] - table[REL_BUCKETS - 1]) * LOG2E
    by_rel = jnp.concatenate([near, jnp.full((t, DIFF_HEADS), -jnp.inf, F32)], axis=0).T
    period = 3 * t
    flat = jnp.tile(by_rel, (1, t))[:, :t * (period - 1)]
    return flat.reshape(DIFF_HEADS, t, period - 1)[:, :, :2 * t]


def _merge_kernel(x_ref, g0_ref, g1_ref, g2_ref, u_ref, halo_ref, ygla_ref, ydiff_ref,
                  pw_ref, ps_ref, wb_ref, wo_ref, gpost_ref, o_ref, *, tiles_per_seq):
    i = pl.program_id(0)
    tm = TM_MERGE
    t_in_seq = i % tiles_per_seq
    halo = jnp.where(t_in_seq == 0, 0.0, halo_ref[...].astype(F32))
    ucat = jnp.concatenate([halo, u_ref[...].astype(F32)], axis=0)
    pos = t_in_seq * tm + lax.broadcasted_iota(jnp.int32, (tm, 1), 0)
    mixed = []
    for g, w in enumerate(POOL_WINDOWS):
        cols = slice(g * POOL_GROUP_DIM, (g + 1) * POOL_GROUP_DIM)
        win = ucat[:, cols]
        span = 1
        while span < w:
            win = win + pltpu.roll(win, span, axis=0)
            span *= 2
        cur = ucat[POOL_HALO:, cols]
        win = win[POOL_HALO:, :]
        cnt = jnp.minimum(pos + 1, w).astype(F32)
        pooled = win / cnt - cur
        mixed.append(jnp.dot(pooled.astype(BF16), pw_ref[g], preferred_element_type=F32))
    y_pool = (jnp.concatenate(mixed, axis=1) * ps_ref[...]).astype(BF16)

    merged = jax.nn.sigmoid(g0_ref[...].astype(F32)) * jnp.dot(y_pool, wb_ref[0], preferred_element_type=F32)
    merged = merged + jax.nn.sigmoid(g1_ref[...].astype(F32)) * jnp.dot(
        ygla_ref[...], wb_ref[1], preferred_element_type=F32)
    merged = merged + jax.nn.sigmoid(g2_ref[...].astype(F32)) * jnp.dot(
        ydiff_ref[...], wb_ref[2], preferred_element_type=F32)
    out = jnp.dot(merged.astype(BF16), wo_ref[...], preferred_element_type=F32)
    o_ref[...] = x_ref[...] + _rms(out, gpost_ref[...])


def _merge(layer, x2, proj, y_gla, y_diff, pool_w, pool_scale, w_branch, w_o, gpost, seq):
    n = x2.shape[0]
    tm = TM_MERGE
    halo_blocks = tm // POOL_HALO
    return pl.pallas_call(
        functools.partial(_merge_kernel, tiles_per_seq=seq // tm),
        out_shape=jax.ShapeDtypeStruct((n, D_MODEL), F32),
        grid=(n // tm,),
        in_specs=[
            pl.BlockSpec((tm, D_MODEL), lambda i: (i, 0)),
            pl.BlockSpec((tm, D_MODEL), lambda i: (i, C_GATE // D_MODEL + 0)),
            pl.BlockSpec((tm, D_MODEL), lambda i: (i, C_GATE // D_MODEL + 1)),
            pl.BlockSpec((tm, D_MODEL), lambda i: (i, C_GATE // D_MODEL + 2)),
            pl.BlockSpec((tm, POOL_DIM), lambda i: (i, C_U // POOL_DIM)),
            pl.BlockSpec((POOL_HALO, POOL_DIM),
                         lambda i: (jnp.maximum(i * halo_blocks - 1, 0), C_U // POOL_DIM)),
            pl.BlockSpec((tm, BRANCH_DIM), lambda i: (i, 0)),
            pl.BlockSpec((tm, BRANCH_DIM), lambda i: (i, 0)),
            _layer_spec(layer, (len(POOL_WINDOWS), POOL_GROUP_DIM, POOL_GROUP_DIM)),
            _layer_spec(layer, (1, POOL_DIM)),
            _layer_spec(layer, (3, BRANCH_DIM, D_MODEL)),
            _layer_spec(layer, (D_MODEL, D_MODEL)),
            _layer_spec(layer, (1, D_MODEL)),
        ],
        out_specs=pl.BlockSpec((tm, D_MODEL), lambda i: (i, 0)),
        compiler_params=pltpu.CompilerParams(
            dimension_semantics=("parallel",), vmem_limit_bytes=VMEM_LIMIT),
        name="merge",
    )(x2, proj, proj, proj, proj, proj, y_gla, y_diff, pool_w, pool_scale, w_branch, w_o, gpost)


def _ffn_kernel(x_ref, gpre_ref, wg_ref, wu_ref, wd_ref, gpost_ref, o_ref):
    x = x_ref[...]
    h = _rms(x, gpre_ref[...]).astype(BF16)
    g = jnp.dot(h, wg_ref[...], preferred_element_type=F32)
    u = jnp.dot(h, wu_ref[...], preferred_element_type=F32)
    a = (g * jax.nn.sigmoid(g) * u).astype(BF16)
    f = jnp.dot(a, wd_ref[...], preferred_element_type=F32)
    o_ref[...] = x + _rms(f, gpost_ref[...])


def _ffn(layer, x2, gpre, wg, wu, wd, gpost):
    n = x2.shape[0]
    tm = TM_FFN
    return pl.pallas_call(
        _ffn_kernel,
        out_shape=jax.ShapeDtypeStruct((n, D_MODEL), F32),
        grid=(n // tm,),
        in_specs=[
            pl.BlockSpec((tm, D_MODEL), lambda i: (i, 0)),
            _layer_spec(layer, (1, D_MODEL)),
            _layer_spec(layer, (D_MODEL, FFN_DIM)),
            _layer_spec(layer, (D_MODEL, FFN_DIM)),
            _layer_spec(layer, (FFN_DIM, D_MODEL)),
            _layer_spec(layer, (1, D_MODEL)),
        ],
        out_specs=pl.BlockSpec((tm, D_MODEL), lambda i: (i, 0)),
        compiler_params=pltpu.CompilerParams(
            dimension_semantics=("parallel",), vmem_limit_bytes=VMEM_LIMIT),
        name="ffn",
    )(x2, gpre, wg, wu, wd, gpost)


def _split_w_in(w):
    z0 = int(sum(IN_SPLITS[:5]))
    assert z0 == W_HEAD and w.shape[-1] - z0 - GLA_GATE_RANK == W_TAIL
    w = w.astype(BF16)
    w_z = jnp.pad(w[..., z0:z0 + GLA_GATE_RANK], ((0, 0), (0, 0), (0, Z_PAD - GLA_GATE_RANK)))
    return w[..., :z0], w_z, w[..., z0 + GLA_GATE_RANK:]


def kernel(x, rel_bias, ln_mix_pre, w_in, pool_w, pool_scale, gla_w_decay, gla_b_decay, gla_norm,
           diff_lambda, diff_norm, w_branch, w_o, ln_mix_post, ln_ffn_pre, ffn_w_gate, ffn_w_up,
           ffn_w_down, ln_ffn_post):
    batch, seq, d = x.shape
    depth = w_in.shape[0]
    assert d == D_MODEL and seq % TC_GLA == 0 and seq % T_ATT == 0 and seq % TM_MERGE == 0
    x2 = x.reshape(batch * seq, d)
    bias = _bias_tiles(rel_bias)
    row = lambda p: p[:, None, :]
    w_head, w_z, w_tail = _split_w_in(w_in)
    w_decay = jnp.pad(gla_w_decay, ((0, 0), (0, Z_PAD - GLA_GATE_RANK), (0, 0))).astype(BF16)
    pool_w, w_branch, w_o = pool_w.astype(BF16), w_branch.astype(BF16), w_o.astype(BF16)
    ffn_w_gate, ffn_w_up, ffn_w_down = ffn_w_gate.astype(BF16), ffn_w_up.astype(BF16), ffn_w_down.astype(BF16)
    for l in range(depth):
        lam_init = 0.8 - 0.6 * math.exp(-0.3 * l)
        proj = _in_proj(l, x2, row(ln_mix_pre), w_head, w_z, w_tail)
        y_gla = _gla(l, proj, w_decay, row(gla_b_decay), row(gla_norm), batch, seq)
        y_diff = _diff(l, proj, diff_lambda, bias, diff_norm[:, :, None], lam_init, batch, seq)
        x2 = _merge(l, x2, proj, y_gla, y_diff, pool_w, row(pool_scale), w_branch, w_o, row(ln_mix_post), seq)
        x2 = _ffn(l, x2, row(ln_ffn_pre), ffn_w_gate, ffn_w_up, ffn_w_down, row(ln_ffn_post))
    return x2.reshape(batch, seq, d)
```

```python
import functools
import math

import jax
import jax.numpy as jnp
from jax import lax
from jax.experimental import pallas as pl
from jax.experimental.pallas import tpu as pltpu

F32 = jnp.float32
BF16 = jnp.bfloat16

D_MODEL = 1024
POOL_WINDOWS = (2, 4, 8, 16)
POOL_GROUP_DIM = 128
POOL_DIM = 512
GLA_HEADS = 4
GLA_DK = 64
GLA_DV = 128
GLA_QK_DIM = 256
GLA_V_DIM = 512
GLA_GATE_RANK = 16
GLA_TAU = 16.0
GLA_CHUNK = 64
DIFF_HEADS = 4
DIFF_DK = 64
DIFF_DV = 128
REL_BUCKETS = 32
REL_MAX_DIST = 128
BRANCH_DIM = 512
FFN_DIM = 2816
RMS_EPS = 1e-6
IN_SPLITS = (512, 256, 256, 512, 512, 16, 512, 512, 512, 3072)

LANES = 128
VMEM_LIMIT = 56 * 1024 * 1024

C_GATE = 0
C_U = 3072
C_GQ = 3584
C_GK = 3840
C_GV = 4096
C_GO = 4608
C_DQ = 5120
C_DK = 5632
C_DV = 6144
C_Z = 6656
Z_PAD = LANES
PROJ_DIM = C_Z + Z_PAD

TM_PROJ = 512
TM_MERGE = 512
TM_FFN = 512
TC_GLA = 512
GLA_BLOCK = 256
T_ATT = 256
POOL_HALO = 16


def _rms(x, gain):
    ms = jnp.mean(x * x, axis=-1, keepdims=True)
    return x * lax.rsqrt(ms + RMS_EPS) * gain


def _nt_dot(a, b):
    return lax.dot_general(a, b, (((1,), (1,)), ((), ())), preferred_element_type=F32)


def _tn_dot(a, b):
    return lax.dot_general(a, b, (((0,), (0,)), ((), ())), preferred_element_type=F32)


def _const_spec(shape):
    nd = len(shape)
    return pl.BlockSpec(shape, lambda *_: (0,) * nd, pipeline_mode=pl.Buffered(1))


def _layer_spec(layer, shape):
    nd = len(shape)
    return pl.BlockSpec((None,) + tuple(shape), lambda *_: (layer,) + (0,) * nd, pipeline_mode=pl.Buffered(1))


W_HEAD = C_DQ - C_U
W_TAIL = C_Z - C_DQ + C_U
N_DIFF = C_Z - C_DQ


def _in_proj_kernel(x_ref, g_ref, wh_ref, wz_ref, wt_ref, o_ref):
    h = _rms(x_ref[...], g_ref[...]).astype(BF16)
    step = 1024
    segments = ((wt_ref, N_DIFF, W_TAIL - N_DIFF, C_GATE), (wh_ref, 0, W_HEAD, C_U),
                (wt_ref, 0, N_DIFF, C_DQ), (wz_ref, 0, Z_PAD, C_Z))
    for w_ref, src, width, dst in segments:
        for c in range(0, width, step):
            w = min(step, width - c)
            o_ref[:, dst + c:dst + c + w] = jnp.dot(
                h, w_ref[:, src + c:src + c + w], preferred_element_type=F32).astype(BF16)


def _in_proj(layer, x2, gain, w_head, w_z, w_tail):
    n = x2.shape[0]
    return pl.pallas_call(
        _in_proj_kernel,
        out_shape=jax.ShapeDtypeStruct((n, PROJ_DIM), BF16),
        grid=(n // TM_PROJ,),
        in_specs=[
            pl.BlockSpec((TM_PROJ, D_MODEL), lambda i: (i, 0)),
            _layer_spec(layer, (1, D_MODEL)),
            _layer_spec(layer, (D_MODEL, W_HEAD)),
            _layer_spec(layer, (D_MODEL, Z_PAD)),
            _layer_spec(layer, (D_MODEL, W_TAIL)),
        ],
        out_specs=pl.BlockSpec((TM_PROJ, PROJ_DIM), lambda i: (i, 0)),
        compiler_params=pltpu.CompilerParams(
            dimension_semantics=("parallel",), vmem_limit_bytes=VMEM_LIMIT),
        name="in_proj",
    )(x2, gain, w_head, w_z, w_tail)


def _log_sigmoid(x):
    return jnp.minimum(x, 0.0) - jnp.log1p(jnp.exp(-jnp.abs(x)))


def _gla_kernel(q_ref, k_ref, v_ref, go_ref, z_ref, wd_ref, bd_ref, gn_ref, y_ref, st_ref):
    n_chunks = TC_GLA // GLA_CHUNK
    chunks_per_block = GLA_BLOCK // GLA_CHUNK

    @pl.when(pl.program_id(1) == 0)
    def _():
        st_ref[...] = jnp.zeros_like(st_ref)

    def pair(h):
        return slice((h // 2) * LANES, (h // 2 + 1) * LANES)

    def vcols(h):
        return slice(h * GLA_DV, (h + 1) * GLA_DV)

    def crow(c):
        return slice(c * GLA_CHUNK, (c + 1) * GLA_CHUNK)

    zl = jnp.dot(z_ref[...], wd_ref[...], preferred_element_type=F32) + bd_ref[...]
    la = _log_sigmoid(zl) / GLA_TAU

    hi = la.astype(BF16)
    r1 = la - hi.astype(F32)
    mid = r1.astype(BF16)
    lo = (r1 - mid.astype(F32)).astype(BF16)
    pieces = jnp.concatenate([hi, mid, lo], axis=1)
    row = lax.broadcasted_iota(jnp.int32, (GLA_CHUNK, GLA_CHUNK), 0)
    col = lax.broadcasted_iota(jnp.int32, (GLA_CHUNK, GLA_CHUNK), 1)
    tril_bf = jnp.where(row >= col, 1.0, 0.0).astype(BF16)
    cums, totals, decay = [], [], []
    for c in range(n_chunks):
        cs = jnp.dot(tril_bf, pieces[crow(c), :], preferred_element_type=F32)
        cum_c = cs[:, :GLA_QK_DIM] + cs[:, GLA_QK_DIM:2 * GLA_QK_DIM] + cs[:, 2 * GLA_QK_DIM:]
        last = cum_c[GLA_CHUNK - 1:GLA_CHUNK, :]
        cums.append(cum_c)
        totals.append(jnp.broadcast_to(last, (GLA_CHUNK, GLA_QK_DIM)))
        decay.append(jnp.exp(last))
    cum = jnp.concatenate(cums, axis=0)
    total = jnp.concatenate(totals, axis=0)

    q = q_ref[...].astype(F32)
    k = k_ref[...].astype(F32)
    q_dec = (q * (GLA_DK ** -0.5) * jnp.exp(cum)).astype(BF16)
    k_dec = k * jnp.exp(-cum)
    k_end = k * jnp.exp(total - cum)
    lane = lax.broadcasted_iota(jnp.int32, (1, LANES), 1)
    head_mask = (jnp.where(lane < GLA_DK, 1.0, 0.0), jnp.where(lane >= GLA_DK, 1.0, 0.0))
    k_dec_h = [(k_dec[:, pair(h)] * head_mask[h % 2]).astype(BF16) for h in range(GLA_HEADS)]
    k_end_h = [(k_end[:, pair(h)] * head_mask[h % 2]).astype(BF16) for h in range(GLA_HEADS)]

    state_in = [[None] * GLA_HEADS for _ in range(n_chunks)]
    for h in range(GLA_HEADS):
        st = st_ref[h]
        for c in range(n_chunks):
            state_in[c][h] = st.astype(BF16)
            st = st * decay[c][:, pair(h)] + _tn_dot(v_ref[crow(c), vcols(h)], k_end_h[h][crow(c), :])
        st_ref[h] = st

    row = lax.broadcasted_iota(jnp.int32, (GLA_BLOCK, GLA_BLOCK), 0)
    col = lax.broadcasted_iota(jnp.int32, (GLA_BLOCK, GLA_BLOCK), 1)
    visible = (row >= col) & (row // GLA_CHUNK == col // GLA_CHUNK)
    gn = gn_ref[...]
    for b in range(TC_GLA // GLA_BLOCK):
        rb = slice(b * GLA_BLOCK, (b + 1) * GLA_BLOCK)
        for h in range(GLA_HEADS):
            scores = jnp.where(visible, _nt_dot(q_dec[rb, pair(h)], k_dec_h[h][rb, :]), 0.0).astype(BF16)
            o = jnp.dot(scores, v_ref[rb, vcols(h)], preferred_element_type=F32)
            o = o + jnp.concatenate(
                [_nt_dot(q_dec[crow(c), pair(h)], state_in[c][h])
                 for c in range(b * chunks_per_block, (b + 1) * chunks_per_block)], axis=0)
            g = go_ref[rb, vcols(h)].astype(F32)
            y_ref[rb, vcols(h)] = (_rms(o, gn) * (g * jax.nn.sigmoid(g))).astype(BF16)


def _gla(layer, proj, w_decay, b_decay, norm_gain, batch, seq):
    n = proj.shape[0]
    tps = seq // TC_GLA
    row = lambda b, t: b * tps + t
    return pl.pallas_call(
        _gla_kernel,
        out_shape=jax.ShapeDtypeStruct((n, GLA_V_DIM), BF16),
        grid=(batch, tps),
        in_specs=[
            pl.BlockSpec((TC_GLA, GLA_QK_DIM), lambda b, t: (row(b, t), C_GQ // GLA_QK_DIM)),
            pl.BlockSpec((TC_GLA, GLA_QK_DIM), lambda b, t: (row(b, t), C_GK // GLA_QK_DIM)),
            pl.BlockSpec((TC_GLA, GLA_V_DIM), lambda b, t: (row(b, t), C_GV // GLA_V_DIM)),
            pl.BlockSpec((TC_GLA, GLA_V_DIM), lambda b, t: (row(b, t), C_GO // GLA_V_DIM)),
            pl.BlockSpec((TC_GLA, Z_PAD), lambda b, t: (row(b, t), C_Z // Z_PAD)),
            _layer_spec(layer, (Z_PAD, GLA_QK_DIM)),
            _layer_spec(layer, (1, GLA_QK_DIM)),
            _layer_spec(layer, (1, GLA_DV)),
        ],
        out_specs=pl.BlockSpec((TC_GLA, GLA_V_DIM), lambda b, t: (row(b, t), 0)),
        scratch_shapes=[pltpu.VMEM((GLA_HEADS, GLA_DV, LANES), F32)],
        compiler_params=pltpu.CompilerParams(
            dimension_semantics=("parallel", "arbitrary"), vmem_limit_bytes=VMEM_LIMIT),
        name="gla",
    )(proj, proj, proj, proj, proj, w_decay, b_decay, norm_gain)


ONES_ROWS = 16
LOG2E = 1.4426950408889634


def _diff_kernel(lam_ref, q_ref, k_ref, v_ref, bias_ref, gn_ref, y_ref,
                 qs_ref, vt_ref, s0_ref, s1_ref, mx0_ref, mx1_ref, m_ref, acc_ref, *, lam_init):
    i = pl.program_id(1)
    t = T_ATT
    lane = lax.broadcasted_iota(jnp.int32, (1, LANES), 1)
    for h in range(DIFF_HEADS):
        q = q_ref[:, h * LANES:(h + 1) * LANES].astype(F32) * (DIFF_DK ** -0.5 * LOG2E)
        qs_ref[h, 0:t, :] = jnp.where(lane < DIFF_DK, q, 0.0).astype(BF16)
        qs_ref[h, t:2 * t, :] = jnp.where(lane >= DIFF_DK, q, 0.0).astype(BF16)

    def rows(j):
        return pl.ds(pl.multiple_of(j * t, t), t)

    def head_cols(h):
        return slice(h * LANES, (h + 1) * LANES)

    m_ref[...] = jnp.full_like(m_ref, -1e30)
    acc_ref[...] = jnp.zeros_like(acc_ref)

    @pl.when(i == 0)
    def _():
        vt_ref[:, :, DIFF_DV:, :] = jnp.ones((DIFF_HEADS, vt_ref.shape[1], ONES_ROWS, t), BF16)

        def transpose_block(jb, carry):
            for h in range(DIFF_HEADS):
                vt_ref[h, jb, 0:DIFF_DV, :] = v_ref[rows(jb), head_cols(h)].T
            return carry

        lax.fori_loop(0, vt_ref.shape[1], transpose_block, 0)

    def logits(j, buf):
        s_ref, mx_ref = buf
        for h in range(DIFF_HEADS):
            s = _nt_dot(k_ref[rows(j), head_cols(h)], qs_ref[h])
            s_ref[h] = s
            mx_ref[h] = jnp.max(s, axis=0, keepdims=True)

    def softmax_pv(j, buf, bias_cols):
        s_ref, mx_ref = buf
        for h in range(DIFF_HEADS):
            s = s_ref[h]
            if bias_cols is None:
                s_max = mx_ref[h]
            else:
                bias = bias_ref[h, :, bias_cols]
                s = s + jnp.concatenate([bias, bias], axis=1)
                s_max = jnp.max(s, axis=0, keepdims=True)
            m_old = m_ref[h]
            m_new = jnp.maximum(m_old, s_max)
            alpha = jnp.exp2(m_old - m_new)
            p = jnp.exp2(s - m_new).astype(BF16)
            m_ref[h] = m_new
            acc_ref[h] = alpha * acc_ref[h] + jnp.dot(vt_ref[h, j], p, preferred_element_type=F32)

    s0 = (s0_ref, mx0_ref)
    s1 = (s1_ref, mx1_ref)
    n_far = jnp.maximum(i - 1, 0)
    peeled = n_far % 2

    @pl.when(i % 2 == 0)
    def _():
        logits(0, s0)

    @pl.when(i % 2 == 1)
    def _():
        logits(0, s1)

    @pl.when(peeled == 1)
    def _():
        logits(1, s1)
        softmax_pv(0, s0, None)

    def far_pair(j):
        logits(j + 1, s0)
        softmax_pv(j, s1, None)
        logits(j + 2, s1)
        softmax_pv(j + 1, s0, None)

    n_pairs = n_far // 2
    peeled_pair = n_pairs % 2

    @pl.when(peeled_pair == 1)
    def _():
        far_pair(peeled)

    def far_quad(jj, carry):
        j = peeled + 2 * peeled_pair + 4 * jj
        far_pair(j)
        far_pair(j + 2)
        return carry

    lax.fori_loop(0, n_pairs // 2, far_quad, 0)

    @pl.when(i >= 1)
    def _():
        logits(i, s0)
        softmax_pv(i - 1, s1, slice(t, 2 * t))

    softmax_pv(i, s0, slice(0, t))

    lam_p = lam_ref[...]
    lam = (jnp.exp(jnp.sum(lam_p[0:1] * lam_p[1:2], axis=1, keepdims=True))
           - jnp.exp(jnp.sum(lam_p[2:3] * lam_p[3:4], axis=1, keepdims=True)) + lam_init)
    for h in range(DIFF_HEADS):
        acc = acc_ref[h, 0:DIFF_DV, :]
        inv_l = 1.0 / acc_ref[h, DIFF_DV:DIFF_DV + 1, :]
        o = acc[:, :t] * inv_l[:, :t] - lam * (acc[:, t:] * inv_l[:, t:])
        ms = jnp.mean(o * o, axis=0, keepdims=True)
        o = o * lax.rsqrt(ms + RMS_EPS) * gn_ref[...] * (1.0 - lam_init)
        y_ref[:, h * DIFF_DV:(h + 1) * DIFF_DV] = o.T.astype(BF16)


def _diff(layer, proj, lam, bias, norm_gain_col, lam_init, batch, seq):
    n = proj.shape[0]
    nq = seq // T_ATT
    hw = DIFF_HEADS * LANES
    return pl.pallas_call(
        functools.partial(_diff_kernel, lam_init=lam_init),
        out_shape=jax.ShapeDtypeStruct((n, DIFF_HEADS * DIFF_DV), BF16),
        grid=(batch, nq),
        in_specs=[
            _layer_spec(layer, (4, DIFF_DK)),
            pl.BlockSpec((T_ATT, hw), lambda b, i: (b * nq + i, C_DQ // hw)),
            pl.BlockSpec((seq, hw), lambda b, i: (b, C_DK // hw), pipeline_mode=pl.Buffered(1)),
            pl.BlockSpec((seq, hw), lambda b, i: (b, C_DV // hw), pipeline_mode=pl.Buffered(1)),
            _const_spec((DIFF_HEADS, T_ATT, 2 * T_ATT)),
            _layer_spec(layer, (DIFF_DV, 1)),
        ],
        out_specs=pl.BlockSpec((T_ATT, DIFF_HEADS * DIFF_DV), lambda b, i: (b * nq + i, 0)),
        scratch_shapes=[
            pltpu.VMEM((DIFF_HEADS, 2 * T_ATT, LANES), BF16),
            pltpu.VMEM((DIFF_HEADS, nq, DIFF_DV + ONES_ROWS, T_ATT), BF16),
            pltpu.VMEM((DIFF_HEADS, T_ATT, 2 * T_ATT), F32),
            pltpu.VMEM((DIFF_HEADS, T_ATT, 2 * T_ATT), F32),
            pltpu.VMEM((DIFF_HEADS, 1, 2 * T_ATT), F32),
            pltpu.VMEM((DIFF_HEADS, 1, 2 * T_ATT), F32),
            pltpu.VMEM((DIFF_HEADS, 1, 2 * T_ATT), F32),
            pltpu.VMEM((DIFF_HEADS, DIFF_DV + ONES_ROWS, 2 * T_ATT), F32),
        ],
        compiler_params=pltpu.CompilerParams(
            dimension_semantics=("parallel", "arbitrary"), vmem_limit_bytes=VMEM_LIMIT),
        name="diff_attn",
    )(lam, proj, proj, proj, bias, norm_gain_col)


def _t5_bucket(rel):
    n = jnp.maximum(rel, 0)
    max_exact = REL_BUCKETS // 2
    nf = jnp.maximum(n, 1).astype(F32)
    large = max_exact + (jnp.log(nf / max_exact) / math.log(REL_MAX_DIST / max_exact)
                         * (REL_BUCKETS - max_exact)).astype(jnp.int32)
    large = jnp.minimum(large, REL_BUCKETS - 1)
    return jnp.where(n < max_exact, n, large)


def _bias_tiles(rel_bias):
    assert T_ATT + 1 >= REL_MAX_DIST
    t = T_ATT
    table = rel_bias.astype(F32)
    near = (table[_t5_bucket(jnp.arange(2 * t, dtype=jnp.int32))] - table[REL_BUCKETS - 1]) * LOG2E
    by_rel = jnp.concatenate([near, jnp.full((t, DIFF_HEADS), -jnp.inf, F32)], axis=0).T
    period = 3 * t
    flat = jnp.tile(by_rel, (1, t))[:, :t * (period - 1)]
    return flat.reshape(DIFF_HEADS, t, period - 1)[:, :, :2 * t]


def _merge_kernel(x_ref, g0_ref, g1_ref, g2_ref, u_ref, halo_ref, ygla_ref, ydiff_ref,
                  pw_ref, ps_ref, wb_ref, wo_ref, gpost_ref, o_ref, *, tiles_per_seq):
    i = pl.program_id(0)
    tm = TM_MERGE
    t_in_seq = i % tiles_per_seq
    halo = jnp.where(t_in_seq == 0, 0.0, halo_ref[...].astype(F32))
    ucat = jnp.concatenate([halo, u_ref[...].astype(F32)], axis=0)
    pos = t_in_seq * tm + lax.broadcasted_iota(jnp.int32, (tm, 1), 0)
    mixed = []
    for g, w in enumerate(POOL_WINDOWS):
        cols = slice(g * POOL_GROUP_DIM, (g + 1) * POOL_GROUP_DIM)
        win = ucat[:, cols]
        span = 1
        while span < w:
            win = win + pltpu.roll(win, span, axis=0)
            span *= 2
        cur = ucat[POOL_HALO:, cols]
        win = win[POOL_HALO:, :]
        cnt = jnp.minimum(pos + 1, w).astype(F32)
        pooled = win / cnt - cur
        mixed.append(jnp.dot(pooled.astype(BF16), pw_ref[g], preferred_element_type=F32))
    y_pool = (jnp.concatenate(mixed, axis=1) * ps_ref[...]).astype(BF16)

    merged = jax.nn.sigmoid(g0_ref[...].astype(F32)) * jnp.dot(y_pool, wb_ref[0], preferred_element_type=F32)
    merged = merged + jax.nn.sigmoid(g1_ref[...].astype(F32)) * jnp.dot(
        ygla_ref[...], wb_ref[1], preferred_element_type=F32)
    merged = merged + jax.nn.sigmoid(g2_ref[...].astype(F32)) * jnp.dot(
        ydiff_ref[...], wb_ref[2], preferred_element_type=F32)
    out = jnp.dot(merged.astype(BF16), wo_ref[...], preferred_element_type=F32)
    o_ref[...] = x_ref[...] + _rms(out, gpost_ref[...])


def _merge(layer, x2, proj, y_gla, y_diff, pool_w, pool_scale, w_branch, w_o, gpost, seq):
    n = x2.shape[0]
    tm = TM_MERGE
    halo_blocks = tm // POOL_HALO
    return pl.pallas_call(
        functools.partial(_merge_kernel, tiles_per_seq=seq // tm),
        out_shape=jax.ShapeDtypeStruct((n, D_MODEL), F32),
        grid=(n // tm,),
        in_specs=[
            pl.BlockSpec((tm, D_MODEL), lambda i: (i, 0)),
            pl.BlockSpec((tm, D_MODEL), lambda i: (i, C_GATE // D_MODEL + 0)),
            pl.BlockSpec((tm, D_MODEL), lambda i: (i, C_GATE // D_MODEL + 1)),
            pl.BlockSpec((tm, D_MODEL), lambda i: (i, C_GATE // D_MODEL + 2)),
            pl.BlockSpec((tm, POOL_DIM), lambda i: (i, C_U // POOL_DIM)),
            pl.BlockSpec((POOL_HALO, POOL_DIM),
                         lambda i: (jnp.maximum(i * halo_blocks - 1, 0), C_U // POOL_DIM)),
            pl.BlockSpec((tm, BRANCH_DIM), lambda i: (i, 0)),
            pl.BlockSpec((tm, BRANCH_DIM), lambda i: (i, 0)),
            _layer_spec(layer, (len(POOL_WINDOWS), POOL_GROUP_DIM, POOL_GROUP_DIM)),
            _layer_spec(layer, (1, POOL_DIM)),
            _layer_spec(layer, (3, BRANCH_DIM, D_MODEL)),
            _layer_spec(layer, (D_MODEL, D_MODEL)),
            _layer_spec(layer, (1, D_MODEL)),
        ],
        out_specs=pl.BlockSpec((tm, D_MODEL), lambda i: (i, 0)),
        compiler_params=pltpu.CompilerParams(
            dimension_semantics=("parallel",), vmem_limit_bytes=VMEM_LIMIT),
        name="merge",
    )(x2, proj, proj, proj, proj, proj, y_gla, y_diff, pool_w, pool_scale, w_branch, w_o, gpost)


def _ffn_kernel(x_ref, gpre_ref, wg_ref, wu_ref, wd_ref, gpost_ref, o_ref):
    x = x_ref[...]
    h = _rms(x, gpre_ref[...]).astype(BF16)
    g = jnp.dot(h, wg_ref[...], preferred_element_type=F32)
    u = jnp.dot(h, wu_ref[...], preferred_element_type=F32)
    a = (g * jax.nn.sigmoid(g) * u).astype(BF16)
    f = jnp.dot(a, wd_ref[...], preferred_element_type=F32)
    o_ref[...] = x + _rms(f, gpost_ref[...])


def _ffn(layer, x2, gpre, wg, wu, wd, gpost):
    n = x2.shape[0]
    tm = TM_FFN
    return pl.pallas_call(
        _ffn_kernel,
        out_shape=jax.ShapeDtypeStruct((n, D_MODEL), F32),
        grid=(n // tm,),
        in_specs=[
            pl.BlockSpec((tm, D_MODEL), lambda i: (i, 0)),
            _layer_spec(layer, (1, D_MODEL)),
            _layer_spec(layer, (D_MODEL, FFN_DIM)),
            _layer_spec(layer, (D_MODEL, FFN_DIM)),
            _layer_spec(layer, (FFN_DIM, D_MODEL)),
            _layer_spec(layer, (1, D_MODEL)),
        ],
        out_specs=pl.BlockSpec((tm, D_MODEL), lambda i: (i, 0)),
        compiler_params=pltpu.CompilerParams(
            dimension_semantics=("parallel",), vmem_limit_bytes=VMEM_LIMIT),
        name="ffn",
    )(x2, gpre, wg, wu, wd, gpost)


def _split_w_in_kernel(w_ref, head_ref, z_ref, tail_ref):
    head_ref[...] = w_ref[:, :W_HEAD].astype(BF16)
    lane = lax.broadcasted_iota(jnp.int32, (1, Z_PAD), 1)
    z_ref[...] = jnp.where(lane < GLA_GATE_RANK, w_ref[:, W_HEAD:W_HEAD + Z_PAD], 0.0).astype(BF16)
    tail_ref[...] = w_ref[:, W_HEAD + GLA_GATE_RANK:].astype(BF16)


def _split_w_in(w):
    depth, d, n = w.shape
    assert sum(IN_SPLITS[:5]) == W_HEAD and n - W_HEAD - GLA_GATE_RANK == W_TAIL
    rows = 256
    spec = lambda width: pl.BlockSpec((None, rows, width), lambda l, i: (l, i, 0))
    return pl.pallas_call(
        _split_w_in_kernel,
        out_shape=[jax.ShapeDtypeStruct((depth, d, width), BF16) for width in (W_HEAD, Z_PAD, W_TAIL)],
        grid=(depth, d // rows),
        in_specs=[spec(n)],
        out_specs=[spec(W_HEAD), spec(Z_PAD), spec(W_TAIL)],
        compiler_params=pltpu.CompilerParams(
            dimension_semantics=("parallel", "parallel"), vmem_limit_bytes=VMEM_LIMIT),
        name="split_w_in",
    )(w)


def kernel(x, rel_bias, ln_mix_pre, w_in, pool_w, pool_scale, gla_w_decay, gla_b_decay, gla_norm,
           diff_lambda, diff_norm, w_branch, w_o, ln_mix_post, ln_ffn_pre, ffn_w_gate, ffn_w_up,
           ffn_w_down, ln_ffn_post):
    batch, seq, d = x.shape
    depth = w_in.shape[0]
    assert d == D_MODEL and seq % TC_GLA == 0 and seq % T_ATT == 0 and seq % TM_MERGE == 0
    x2 = x.reshape(batch * seq, d)
    bias = _bias_tiles(rel_bias)
    row = lambda p: p[:, None, :]
    w_head, w_z, w_tail = _split_w_in(w_in)
    w_decay = jnp.pad(gla_w_decay, ((0, 0), (0, Z_PAD - GLA_GATE_RANK), (0, 0))).astype(BF16)
    pool_w, w_branch, w_o = pool_w.astype(BF16), w_branch.astype(BF16), w_o.astype(BF16)
    ffn_w_gate, ffn_w_up, ffn_w_down = ffn_w_gate.astype(BF16), ffn_w_up.astype(BF16), ffn_w_down.astype(BF16)
    for l in range(depth):
        lam_init = 0.8 - 0.6 * math.exp(-0.3 * l)
        proj = _in_proj(l, x2, row(ln_mix_pre), w_head, w_z, w_tail)
        y_gla = _gla(l, proj, w_decay, row(gla_b_decay), row(gla_norm), batch, seq)
        y_diff = _diff(l, proj, diff_lambda, bias, diff_norm[:, :, None], lam_init, batch, seq)
        x2 = _merge(l, x2, proj, y_gla, y_diff, pool_w, row(pool_scale), w_branch, w_o, row(ln_mix_post), seq)
        x2 = _ffn(l, x2, row(ln_ffn_pre), ffn_w_gate, ffn_w_up, ffn_w_down, row(ln_ffn_post))
    return x2.reshape(batch, seq, d)
```

```python
import functools
import math

import jax
import jax.numpy as jnp
from jax import lax
from jax.experimental import pallas as pl
from jax.experimental.pallas import tpu as pltpu

F32 = jnp.float32
BF16 = jnp.bfloat16

D_MODEL = 1024
POOL_WINDOWS = (2, 4, 8, 16)
POOL_GROUP_DIM = 128
POOL_DIM = 512
GLA_HEADS = 4
GLA_DK = 64
GLA_DV = 128
GLA_QK_DIM = 256
GLA_V_DIM = 512
GLA_GATE_RANK = 16
GLA_TAU = 16.0
GLA_CHUNK = 64
DIFF_HEADS = 4
DIFF_DK = 64
DIFF_DV = 128
REL_BUCKETS = 32
REL_MAX_DIST = 128
BRANCH_DIM = 512
FFN_DIM = 2816
RMS_EPS = 1e-6
IN_SPLITS = (512, 256, 256, 512, 512, 16, 512, 512, 512, 3072)

LANES = 128
VMEM_LIMIT = 56 * 1024 * 1024

C_GATE = 0
C_U = 3072
C_GQ = 3584
C_GK = 3840
C_GV = 4096
C_GO = 4608
C_DQ = 5120
C_DK = 5632
C_DV = 6144
C_Z = 6656
Z_PAD = LANES
PROJ_DIM = C_Z + Z_PAD

TM_PROJ = 512
TM_MERGE = 512
TM_FFN = 512
TC_GLA = 1024
GLA_BLOCK = 256
T_ATT = 256
POOL_HALO = 16


def _rms(x, gain):
    ms = jnp.mean(x * x, axis=-1, keepdims=True)
    return x * lax.rsqrt(ms + RMS_EPS) * gain


def _nt_dot(a, b):
    return lax.dot_general(a, b, (((1,), (1,)), ((), ())), preferred_element_type=F32)


def _tn_dot(a, b):
    return lax.dot_general(a, b, (((0,), (0,)), ((), ())), preferred_element_type=F32)


def _const_spec(shape):
    nd = len(shape)
    return pl.BlockSpec(shape, lambda *_: (0,) * nd, pipeline_mode=pl.Buffered(1))


def _layer_spec(layer, shape):
    nd = len(shape)
    return pl.BlockSpec((None,) + tuple(shape), lambda *_: (layer,) + (0,) * nd, pipeline_mode=pl.Buffered(1))


W_HEAD = C_DQ - C_U
W_TAIL = C_Z - C_DQ + C_U
N_DIFF = C_Z - C_DQ


def _in_proj_kernel(x_ref, g_ref, wh_ref, wz_ref, wt_ref, o_ref):
    h = _rms(x_ref[...], g_ref[...]).astype(BF16)
    step = 1024
    segments = ((wt_ref, N_DIFF, W_TAIL - N_DIFF, C_GATE), (wh_ref, 0, W_HEAD, C_U),
                (wt_ref, 0, N_DIFF, C_DQ), (wz_ref, 0, Z_PAD, C_Z))
    for w_ref, src, width, dst in segments:
        for c in range(0, width, step):
            w = min(step, width - c)
            o_ref[:, dst + c:dst + c + w] = jnp.dot(
                h, w_ref[:, src + c:src + c + w], preferred_element_type=F32).astype(BF16)


def _in_proj(layer, x2, gain, w_head, w_z, w_tail):
    n = x2.shape[0]
    return pl.pallas_call(
        _in_proj_kernel,
        out_shape=jax.ShapeDtypeStruct((n, PROJ_DIM), BF16),
        grid=(n // TM_PROJ,),
        in_specs=[
            pl.BlockSpec((TM_PROJ, D_MODEL), lambda i: (i, 0)),
            _layer_spec(layer, (1, D_MODEL)),
            _layer_spec(layer, (D_MODEL, W_HEAD)),
            _layer_spec(layer, (D_MODEL, Z_PAD)),
            _layer_spec(layer, (D_MODEL, W_TAIL)),
        ],
        out_specs=pl.BlockSpec((TM_PROJ, PROJ_DIM), lambda i: (i, 0)),
        compiler_params=pltpu.CompilerParams(
            dimension_semantics=("parallel",), vmem_limit_bytes=VMEM_LIMIT),
        name="in_proj",
    )(x2, gain, w_head, w_z, w_tail)


def _log_sigmoid(x):
    return jnp.minimum(x, 0.0) - jnp.log1p(jnp.exp(-jnp.abs(x)))


def _gla_kernel(q_ref, k_ref, v_ref, go_ref, z_ref, wd_ref, bd_ref, gn_ref, y_ref, st_ref):
    n_chunks = TC_GLA // GLA_CHUNK
    chunks_per_block = GLA_BLOCK // GLA_CHUNK

    @pl.when(pl.program_id(1) == 0)
    def _():
        st_ref[...] = jnp.zeros_like(st_ref)

    def pair(h):
        return slice((h // 2) * LANES, (h // 2 + 1) * LANES)

    def vcols(h):
        return slice(h * GLA_DV, (h + 1) * GLA_DV)

    def crow(c):
        return slice(c * GLA_CHUNK, (c + 1) * GLA_CHUNK)

    zl = jnp.dot(z_ref[...], wd_ref[...], preferred_element_type=F32) + bd_ref[...]
    la = _log_sigmoid(zl) / GLA_TAU

    hi = la.astype(BF16)
    r1 = la - hi.astype(F32)
    mid = r1.astype(BF16)
    lo = (r1 - mid.astype(F32)).astype(BF16)
    pieces = jnp.concatenate([hi, mid, lo], axis=1)
    row = lax.broadcasted_iota(jnp.int32, (GLA_CHUNK, GLA_CHUNK), 0)
    col = lax.broadcasted_iota(jnp.int32, (GLA_CHUNK, GLA_CHUNK), 1)
    tril_bf = jnp.where(row >= col, 1.0, 0.0).astype(BF16)
    cums, totals, decay = [], [], []
    for c in range(n_chunks):
        cs = jnp.dot(tril_bf, pieces[crow(c), :], preferred_element_type=F32)
        cum_c = cs[:, :GLA_QK_DIM] + cs[:, GLA_QK_DIM:2 * GLA_QK_DIM] + cs[:, 2 * GLA_QK_DIM:]
        last = cum_c[GLA_CHUNK - 1:GLA_CHUNK, :]
        cums.append(cum_c)
        totals.append(jnp.broadcast_to(last, (GLA_CHUNK, GLA_QK_DIM)))
        decay.append(jnp.exp(last))
    cum = jnp.concatenate(cums, axis=0)
    total = jnp.concatenate(totals, axis=0)

    q = q_ref[...].astype(F32)
    k = k_ref[...].astype(F32)
    q_dec = (q * (GLA_DK ** -0.5) * jnp.exp(cum)).astype(BF16)
    k_dec = k * jnp.exp(-cum)
    k_end = k * jnp.exp(total - cum)
    lane = lax.broadcasted_iota(jnp.int32, (1, LANES), 1)
    head_mask = (jnp.where(lane < GLA_DK, 1.0, 0.0), jnp.where(lane >= GLA_DK, 1.0, 0.0))
    k_dec_h = [(k_dec[:, pair(h)] * head_mask[h % 2]).astype(BF16) for h in range(GLA_HEADS)]
    k_end_h = [(k_end[:, pair(h)] * head_mask[h % 2]).astype(BF16) for h in range(GLA_HEADS)]

    state_in = [[None] * GLA_HEADS for _ in range(n_chunks)]
    for h in range(GLA_HEADS):
        st = st_ref[h]
        for c in range(n_chunks):
            state_in[c][h] = st.astype(BF16)
            st = st * decay[c][:, pair(h)] + _tn_dot(v_ref[crow(c), vcols(h)], k_end_h[h][crow(c), :])
        st_ref[h] = st

    row = lax.broadcasted_iota(jnp.int32, (GLA_BLOCK, GLA_BLOCK), 0)
    col = lax.broadcasted_iota(jnp.int32, (GLA_BLOCK, GLA_BLOCK), 1)
    visible = (row >= col) & (row // GLA_CHUNK == col // GLA_CHUNK)
    gn = gn_ref[...]
    for b in range(TC_GLA // GLA_BLOCK):
        rb = slice(b * GLA_BLOCK, (b + 1) * GLA_BLOCK)
        for h in range(GLA_HEADS):
            scores = jnp.where(visible, _nt_dot(q_dec[rb, pair(h)], k_dec_h[h][rb, :]), 0.0).astype(BF16)
            o = jnp.dot(scores, v_ref[rb, vcols(h)], preferred_element_type=F32)
            o = o + jnp.concatenate(
                [_nt_dot(q_dec[crow(c), pair(h)], state_in[c][h])
                 for c in range(b * chunks_per_block, (b + 1) * chunks_per_block)], axis=0)
            g = go_ref[rb, vcols(h)].astype(F32)
            y_ref[rb, vcols(h)] = (_rms(o, gn) * (g * jax.nn.sigmoid(g))).astype(BF16)


def _gla(layer, proj, w_decay, b_decay, norm_gain, batch, seq):
    n = proj.shape[0]
    tps = seq // TC_GLA
    row = lambda b, t: b * tps + t
    return pl.pallas_call(
        _gla_kernel,
        out_shape=jax.ShapeDtypeStruct((n, GLA_V_DIM), BF16),
        grid=(batch, tps),
        in_specs=[
            pl.BlockSpec((TC_GLA, GLA_QK_DIM), lambda b, t: (row(b, t), C_GQ // GLA_QK_DIM)),
            pl.BlockSpec((TC_GLA, GLA_QK_DIM), lambda b, t: (row(b, t), C_GK // GLA_QK_DIM)),
            pl.BlockSpec((TC_GLA, GLA_V_DIM), lambda b, t: (row(b, t), C_GV // GLA_V_DIM)),
            pl.BlockSpec((TC_GLA, GLA_V_DIM), lambda b, t: (row(b, t), C_GO // GLA_V_DIM)),
            pl.BlockSpec((TC_GLA, Z_PAD), lambda b, t: (row(b, t), C_Z // Z_PAD)),
            _layer_spec(layer, (Z_PAD, GLA_QK_DIM)),
            _layer_spec(layer, (1, GLA_QK_DIM)),
            _layer_spec(layer, (1, GLA_DV)),
        ],
        out_specs=pl.BlockSpec((TC_GLA, GLA_V_DIM), lambda b, t: (row(b, t), 0)),
        scratch_shapes=[pltpu.VMEM((GLA_HEADS, GLA_DV, LANES), F32)],
        compiler_params=pltpu.CompilerParams(
            dimension_semantics=("parallel", "arbitrary"), vmem_limit_bytes=VMEM_LIMIT),
        name="gla",
    )(proj, proj, proj, proj, proj, w_decay, b_decay, norm_gain)


ONES_ROWS = 16
LOG2E = 1.4426950408889634


def _diff_kernel(lam_ref, q_ref, k_ref, v_ref, bias_ref, gn_ref, y_ref,
                 qs_ref, vt_ref, s0_ref, s1_ref, mx0_ref, mx1_ref, m_ref, acc_ref, *, lam_init):
    i = pl.program_id(1)
    t = T_ATT
    lane = lax.broadcasted_iota(jnp.int32, (1, LANES), 1)
    for h in range(DIFF_HEADS):
        q = q_ref[:, h * LANES:(h + 1) * LANES].astype(F32) * (DIFF_DK ** -0.5 * LOG2E)
        qs_ref[h, 0:t, :] = jnp.where(lane < DIFF_DK, q, 0.0).astype(BF16)
        qs_ref[h, t:2 * t, :] = jnp.where(lane >= DIFF_DK, q, 0.0).astype(BF16)

    def rows(j):
        return pl.ds(pl.multiple_of(j * t, t), t)

    def head_cols(h):
        return slice(h * LANES, (h + 1) * LANES)

    m_ref[...] = jnp.full_like(m_ref, -1e30)
    acc_ref[...] = jnp.zeros_like(acc_ref)

    @pl.when(i == 0)
    def _():
        vt_ref[:, :, DIFF_DV:, :] = jnp.ones((DIFF_HEADS, vt_ref.shape[1], ONES_ROWS, t), BF16)

        def transpose_block(jb, carry):
            for h in range(DIFF_HEADS):
                vt_ref[h, jb, 0:DIFF_DV, :] = v_ref[rows(jb), head_cols(h)].T
            return carry

        lax.fori_loop(0, vt_ref.shape[1], transpose_block, 0)

    def logits(j, buf):
        s_ref, mx_ref = buf
        for h in range(DIFF_HEADS):
            s = _nt_dot(k_ref[rows(j), head_cols(h)], qs_ref[h])
            s_ref[h] = s
            mx_ref[h] = jnp.max(s, axis=0, keepdims=True)

    def softmax_pv(j, buf, bias_cols):
        s_ref, mx_ref = buf
        for h in range(DIFF_HEADS):
            s = s_ref[h]
            if bias_cols is None:
                s_max = mx_ref[h]
            else:
                bias = bias_ref[h, :, bias_cols]
                s = s + jnp.concatenate([bias, bias], axis=1)
                s_max = jnp.max(s, axis=0, keepdims=True)
            m_old = m_ref[h]
            m_new = jnp.maximum(m_old, s_max)
            alpha = jnp.exp2(m_old - m_new)
            p = jnp.exp2(s - m_new).astype(BF16)
            m_ref[h] = m_new
            acc_ref[h] = alpha * acc_ref[h] + jnp.dot(vt_ref[h, j], p, preferred_element_type=F32)

    s0 = (s0_ref, mx0_ref)
    s1 = (s1_ref, mx1_ref)
    n_far = jnp.maximum(i - 1, 0)
    peeled = n_far % 2

    @pl.when(i % 2 == 0)
    def _():
        logits(0, s0)

    @pl.when(i % 2 == 1)
    def _():
        logits(0, s1)

    @pl.when(peeled == 1)
    def _():
        logits(1, s1)
        softmax_pv(0, s0, None)

    def far_pair(j):
        logits(j + 1, s0)
        softmax_pv(j, s1, None)
        logits(j + 2, s1)
        softmax_pv(j + 1, s0, None)

    n_pairs = n_far // 2
    peeled_pair = n_pairs % 2

    @pl.when(peeled_pair == 1)
    def _():
        far_pair(peeled)

    def far_quad(jj, carry):
        j = peeled + 2 * peeled_pair + 4 * jj
        far_pair(j)
        far_pair(j + 2)
        return carry

    lax.fori_loop(0, n_pairs // 2, far_quad, 0)

    @pl.when(i >= 1)
    def _():
        logits(i, s0)
        softmax_pv(i - 1, s1, slice(t, 2 * t))

    softmax_pv(i, s0, slice(0, t))

    lam_p = lam_ref[...]
    lam = (jnp.exp(jnp.sum(lam_p[0:1] * lam_p[1:2], axis=1, keepdims=True))
           - jnp.exp(jnp.sum(lam_p[2:3] * lam_p[3:4], axis=1, keepdims=True)) + lam_init)
    for h in range(DIFF_HEADS):
        acc = acc_ref[h, 0:DIFF_DV, :]
        inv_l = 1.0 / acc_ref[h, DIFF_DV:DIFF_DV + 1, :]
        o = acc[:, :t] * inv_l[:, :t] - lam * (acc[:, t:] * inv_l[:, t:])
        ms = jnp.mean(o * o, axis=0, keepdims=True)
        o = o * lax.rsqrt(ms + RMS_EPS) * gn_ref[...] * (1.0 - lam_init)
        y_ref[:, h * DIFF_DV:(h + 1) * DIFF_DV] = o.T.astype(BF16)


def _diff(layer, proj, lam, bias, norm_gain_col, lam_init, batch, seq):
    n = proj.shape[0]
    nq = seq // T_ATT
    hw = DIFF_HEADS * LANES
    return pl.pallas_call(
        functools.partial(_diff_kernel, lam_init=lam_init),
        out_shape=jax.ShapeDtypeStruct((n, DIFF_HEADS * DIFF_DV), BF16),
        grid=(batch, nq),
        in_specs=[
            _layer_spec(layer, (4, DIFF_DK)),
            pl.BlockSpec((T_ATT, hw), lambda b, i: (b * nq + i, C_DQ // hw)),
            pl.BlockSpec((seq, hw), lambda b, i: (b, C_DK // hw), pipeline_mode=pl.Buffered(1)),
            pl.BlockSpec((seq, hw), lambda b, i: (b, C_DV // hw), pipeline_mode=pl.Buffered(1)),
            _const_spec((DIFF_HEADS, T_ATT, 2 * T_ATT)),
            _layer_spec(layer, (DIFF_DV, 1)),
        ],
        out_specs=pl.BlockSpec((T_ATT, DIFF_HEADS * DIFF_DV), lambda b, i: (b * nq + i, 0)),
        scratch_shapes=[
            pltpu.VMEM((DIFF_HEADS, 2 * T_ATT, LANES), BF16),
            pltpu.VMEM((DIFF_HEADS, nq, DIFF_DV + ONES_ROWS, T_ATT), BF16),
            pltpu.VMEM((DIFF_HEADS, T_ATT, 2 * T_ATT), F32),
            pltpu.VMEM((DIFF_HEADS, T_ATT, 2 * T_ATT), F32),
            pltpu.VMEM((DIFF_HEADS, 1, 2 * T_ATT), F32),
            pltpu.VMEM((DIFF_HEADS, 1, 2 * T_ATT), F32),
            pltpu.VMEM((DIFF_HEADS, 1, 2 * T_ATT), F32),
            pltpu.VMEM((DIFF_HEADS, DIFF_DV + ONES_ROWS, 2 * T_ATT), F32),
        ],
        compiler_params=pltpu.CompilerParams(
            dimension_semantics=("parallel", "arbitrary"), vmem_limit_bytes=VMEM_LIMIT),
        name="diff_attn",
    )(lam, proj, proj, proj, bias, norm_gain_col)


def _t5_bucket(rel):
    n = jnp.maximum(rel, 0)
    max_exact = REL_BUCKETS // 2
    nf = jnp.maximum(n, 1).astype(F32)
    large = max_exact + (jnp.log(nf / max_exact) / math.log(REL_MAX_DIST / max_exact)
                         * (REL_BUCKETS - max_exact)).astype(jnp.int32)
    large = jnp.minimum(large, REL_BUCKETS - 1)
    return jnp.where(n < max_exact, n, large)


def _bias_tiles(rel_bias):
    assert T_ATT + 1 >= REL_MAX_DIST
    t = T_ATT
    table = rel_bias.astype(F32)
    near = (table[_t5_bucket(jnp.arange(2 * t, dtype=jnp.int32))] - table[REL_BUCKETS - 1]) * LOG2E
    by_rel = jnp.concatenate([near, jnp.full((t, DIFF_HEADS), -jnp.inf, F32)], axis=0).T
    period = 3 * t
    flat = jnp.tile(by_rel, (1, t))[:, :t * (period - 1)]
    return flat.reshape(DIFF_HEADS, t, period - 1)[:, :, :2 * t]


def _merge_kernel(x_ref, g0_ref, g1_ref, g2_ref, u_ref, halo_ref, ygla_ref, ydiff_ref,
                  pw_ref, ps_ref, wb_ref, wo_ref, gpost_ref, o_ref, *, tiles_per_seq):
    i = pl.program_id(0)
    tm = TM_MERGE
    t_in_seq = i % tiles_per_seq
    halo = jnp.where(t_in_seq == 0, 0.0, halo_ref[...].astype(F32))
    ucat = jnp.concatenate([halo, u_ref[...].astype(F32)], axis=0)
    pos = t_in_seq * tm + lax.broadcasted_iota(jnp.int32, (tm, 1), 0)
    mixed = []
    for g, w in enumerate(POOL_WINDOWS):
        cols = slice(g * POOL_GROUP_DIM, (g + 1) * POOL_GROUP_DIM)
        win = ucat[:, cols]
        span = 1
        while span < w:
            win = win + pltpu.roll(win, span, axis=0)
            span *= 2
        cur = ucat[POOL_HALO:, cols]
        win = win[POOL_HALO:, :]
        cnt = jnp.minimum(pos + 1, w).astype(F32)
        pooled = win / cnt - cur
        mixed.append(jnp.dot(pooled.astype(BF16), pw_ref[g], preferred_element_type=F32))
    y_pool = (jnp.concatenate(mixed, axis=1) * ps_ref[...]).astype(BF16)

    merged = jax.nn.sigmoid(g0_ref[...].astype(F32)) * jnp.dot(y_pool, wb_ref[0], preferred_element_type=F32)
    merged = merged + jax.nn.sigmoid(g1_ref[...].astype(F32)) * jnp.dot(
        ygla_ref[...], wb_ref[1], preferred_element_type=F32)
    merged = merged + jax.nn.sigmoid(g2_ref[...].astype(F32)) * jnp.dot(
        ydiff_ref[...], wb_ref[2], preferred_element_type=F32)
    out = jnp.dot(merged.astype(BF16), wo_ref[...], preferred_element_type=F32)
    o_ref[...] = x_ref[...] + _rms(out, gpost_ref[...])


def _merge(layer, x2, proj, y_gla, y_diff, pool_w, pool_scale, w_branch, w_o, gpost, seq):
    n = x2.shape[0]
    tm = TM_MERGE
    halo_blocks = tm // POOL_HALO
    return pl.pallas_call(
        functools.partial(_merge_kernel, tiles_per_seq=seq // tm),
        out_shape=jax.ShapeDtypeStruct((n, D_MODEL), F32),
        grid=(n // tm,),
        in_specs=[
            pl.BlockSpec((tm, D_MODEL), lambda i: (i, 0)),
            pl.BlockSpec((tm, D_MODEL), lambda i: (i, C_GATE // D_MODEL + 0)),
            pl.BlockSpec((tm, D_MODEL), lambda i: (i, C_GATE // D_MODEL + 1)),
            pl.BlockSpec((tm, D_MODEL), lambda i: (i, C_GATE // D_MODEL + 2)),
            pl.BlockSpec((tm, POOL_DIM), lambda i: (i, C_U // POOL_DIM)),
            pl.BlockSpec((POOL_HALO, POOL_DIM),
                         lambda i: (jnp.maximum(i * halo_blocks - 1, 0), C_U // POOL_DIM)),
            pl.BlockSpec((tm, BRANCH_DIM), lambda i: (i, 0)),
            pl.BlockSpec((tm, BRANCH_DIM), lambda i: (i, 0)),
            _layer_spec(layer, (len(POOL_WINDOWS), POOL_GROUP_DIM, POOL_GROUP_DIM)),
            _layer_spec(layer, (1, POOL_DIM)),
            _layer_spec(layer, (3, BRANCH_DIM, D_MODEL)),
            _layer_spec(layer, (D_MODEL, D_MODEL)),
            _layer_spec(layer, (1, D_MODEL)),
        ],
        out_specs=pl.BlockSpec((tm, D_MODEL), lambda i: (i, 0)),
        compiler_params=pltpu.CompilerParams(
            dimension_semantics=("parallel",), vmem_limit_bytes=VMEM_LIMIT),
        name="merge",
    )(x2, proj, proj, proj, proj, proj, y_gla, y_diff, pool_w, pool_scale, w_branch, w_o, gpost)


def _ffn_kernel(x_ref, gpre_ref, wg_ref, wu_ref, wd_ref, gpost_ref, o_ref):
    x = x_ref[...]
    h = _rms(x, gpre_ref[...]).astype(BF16)
    g = jnp.dot(h, wg_ref[...], preferred_element_type=F32)
    u = jnp.dot(h, wu_ref[...], preferred_element_type=F32)
    a = (g * jax.nn.sigmoid(g) * u).astype(BF16)
    f = jnp.dot(a, wd_ref[...], preferred_element_type=F32)
    o_ref[...] = x + _rms(f, gpost_ref[...])


def _ffn(layer, x2, gpre, wg, wu, wd, gpost):
    n = x2.shape[0]
    tm = TM_FFN
    return pl.pallas_call(
        _ffn_kernel,
        out_shape=jax.ShapeDtypeStruct((n, D_MODEL), F32),
        grid=(n // tm,),
        in_specs=[
            pl.BlockSpec((tm, D_MODEL), lambda i: (i, 0)),
            _layer_spec(layer, (1, D_MODEL)),
            _layer_spec(layer, (D_MODEL, FFN_DIM)),
            _layer_spec(layer, (D_MODEL, FFN_DIM)),
            _layer_spec(layer, (FFN_DIM, D_MODEL)),
            _layer_spec(layer, (1, D_MODEL)),
        ],
        out_specs=pl.BlockSpec((tm, D_MODEL), lambda i: (i, 0)),
        compiler_params=pltpu.CompilerParams(
            dimension_semantics=("parallel",), vmem_limit_bytes=VMEM_LIMIT),
        name="ffn",
    )(x2, gpre, wg, wu, wd, gpost)


def _split_w_in(w):
    z0 = int(sum(IN_SPLITS[:5]))
    assert z0 == W_HEAD and w.shape[-1] - z0 - GLA_GATE_RANK == W_TAIL
    w = w.astype(BF16)
    w_z = jnp.pad(w[..., z0:z0 + GLA_GATE_RANK], ((0, 0), (0, 0), (0, Z_PAD - GLA_GATE_RANK)))
    return w[..., :z0], w_z, w[..., z0 + GLA_GATE_RANK:]


def kernel(x, rel_bias, ln_mix_pre, w_in, pool_w, pool_scale, gla_w_decay, gla_b_decay, gla_norm,
           diff_lambda, diff_norm, w_branch, w_o, ln_mix_post, ln_ffn_pre, ffn_w_gate, ffn_w_up,
           ffn_w_down, ln_ffn_post):
    batch, seq, d = x.shape
    depth = w_in.shape[0]
    assert d == D_MODEL and seq % TC_GLA == 0 and seq % T_ATT == 0 and seq % TM_MERGE == 0
    x2 = x.reshape(batch * seq, d)
    bias = _bias_tiles(rel_bias)
    row = lambda p: p[:, None, :]
    w_head, w_z, w_tail = _split_w_in(w_in)
    w_decay = jnp.pad(gla_w_decay, ((0, 0), (0, Z_PAD - GLA_GATE_RANK), (0, 0))).astype(BF16)
    pool_w, w_branch, w_o = pool_w.astype(BF16), w_branch.astype(BF16), w_o.astype(BF16)
    ffn_w_gate, ffn_w_up, ffn_w_down = ffn_w_gate.astype(BF16), ffn_w_up.astype(BF16), ffn_w_down.astype(BF16)
    for l in range(depth):
        lam_init = 0.8 - 0.6 * math.exp(-0.3 * l)
        proj = _in_proj(l, x2, row(ln_mix_pre), w_head, w_z, w_tail)
        y_gla = _gla(l, proj, w_decay, row(gla_b_decay), row(gla_norm), batch, seq)
        y_diff = _diff(l, proj, diff_lambda, bias, diff_norm[:, :, None], lam_init, batch, seq)
        x2 = _merge(l, x2, proj, y_gla, y_diff, pool_w, row(pool_scale), w_branch, w_o, row(ln_mix_post), seq)
        x2 = _ffn(l, x2, row(ln_ffn_pre), ffn_w_gate, ffn_w_up, ffn_w_down, row(ln_ffn_post))
    return x2.reshape(batch, seq, d)
```

```python
import functools
import math

import jax
import jax.numpy as jnp
from jax import lax
from jax.experimental import pallas as pl
from jax.experimental.pallas import tpu as pltpu

F32 = jnp.float32
BF16 = jnp.bfloat16

D_MODEL = 1024
POOL_WINDOWS = (2, 4, 8, 16)
POOL_GROUP_DIM = 128
POOL_DIM = 512
GLA_HEADS = 4
GLA_DK = 64
GLA_DV = 128
GLA_QK_DIM = 256
GLA_V_DIM = 512
GLA_GATE_RANK = 16
GLA_TAU = 16.0
GLA_CHUNK = 64
DIFF_HEADS = 4
DIFF_DK = 64
DIFF_DV = 128
REL_BUCKETS = 32
REL_MAX_DIST = 128
BRANCH_DIM = 512
FFN_DIM = 2816
RMS_EPS = 1e-6
IN_SPLITS = (512, 256, 256, 512, 512, 16, 512, 512, 512, 3072)

LANES = 128
BF16_SUBLANES = 16
MXU_TILE = 256
VMEM_LIMIT = 56 * 1024 * 1024

C_GATE = 0
C_U = 3072
C_GQ = 3584
C_GK = 3840
C_GV = 4096
C_GO = 4608
C_DQ = 5120
C_DK = 5632
C_DV = 6144
C_Z = 6656
Z_PAD = LANES
PROJ_DIM = C_Z + Z_PAD

TM_PROJ = 512
TM_MERGE = 512
TM_FFN = 512
TC_GLA = 2048
GLA_BLOCK = MXU_TILE
T_ATT = 256
POOL_HALO = BF16_SUBLANES


def _rms(x, gain):
    ms = jnp.mean(x * x, axis=-1, keepdims=True)
    return x * lax.rsqrt(ms + RMS_EPS) * gain


def _nt_dot(a, b):
    return lax.dot_general(a, b, (((1,), (1,)), ((), ())), preferred_element_type=F32)


def _tn_dot(a, b):
    return lax.dot_general(a, b, (((0,), (0,)), ((), ())), preferred_element_type=F32)


def _const_spec(shape):
    nd = len(shape)
    return pl.BlockSpec(shape, lambda *_: (0,) * nd, pipeline_mode=pl.Buffered(1))


def _layer_spec(layer, shape):
    nd = len(shape)
    return pl.BlockSpec((None,) + tuple(shape), lambda *_: (layer,) + (0,) * nd, pipeline_mode=pl.Buffered(1))


W_HEAD = C_DQ - C_U
W_TAIL = C_Z - C_DQ + C_U
N_DIFF = C_Z - C_DQ
PROJ_COL_STEP = 4 * MXU_TILE


def _in_proj_kernel(x_ref, g_ref, wh_ref, wz_ref, wt_ref, o_ref):
    h = _rms(x_ref[...], g_ref[...]).astype(BF16)
    step = PROJ_COL_STEP
    segments = ((wt_ref, N_DIFF, W_TAIL - N_DIFF, C_GATE), (wh_ref, 0, W_HEAD, C_U),
                (wt_ref, 0, N_DIFF, C_DQ), (wz_ref, 0, Z_PAD, C_Z))
    for w_ref, src, width, dst in segments:
        for c in range(0, width, step):
            w = min(step, width - c)
            o_ref[:, dst + c:dst + c + w] = jnp.dot(
                h, w_ref[:, src + c:src + c + w], preferred_element_type=F32).astype(BF16)


def _in_proj(layer, x2, gain, w_head, w_z, w_tail):
    n = x2.shape[0]
    return pl.pallas_call(
        _in_proj_kernel,
        out_shape=jax.ShapeDtypeStruct((n, PROJ_DIM), BF16),
        grid=(n // TM_PROJ,),
        in_specs=[
            pl.BlockSpec((TM_PROJ, D_MODEL), lambda i: (i, 0)),
            _layer_spec(layer, (1, D_MODEL)),
            _layer_spec(layer, (D_MODEL, W_HEAD)),
            _layer_spec(layer, (D_MODEL, Z_PAD)),
            _layer_spec(layer, (D_MODEL, W_TAIL)),
        ],
        out_specs=pl.BlockSpec((TM_PROJ, PROJ_DIM), lambda i: (i, 0)),
        compiler_params=pltpu.CompilerParams(
            dimension_semantics=("parallel",), vmem_limit_bytes=VMEM_LIMIT),
        name="in_proj",
    )(x2, gain, w_head, w_z, w_tail)


def _log_sigmoid(x):
    return jnp.minimum(x, 0.0) - jnp.log1p(jnp.exp(-jnp.abs(x)))


def _gla_kernel(q_ref, k_ref, v_ref, go_ref, z_ref, wd_ref, bd_ref, gn_ref, y_ref, st_ref):
    n_chunks = TC_GLA // GLA_CHUNK
    chunks_per_block = GLA_BLOCK // GLA_CHUNK

    @pl.when(pl.program_id(1) == 0)
    def _():
        st_ref[...] = jnp.zeros_like(st_ref)

    def pair(h):
        return slice((h // 2) * LANES, (h // 2 + 1) * LANES)

    def vcols(h):
        return slice(h * GLA_DV, (h + 1) * GLA_DV)

    def crow(c):
        return slice(c * GLA_CHUNK, (c + 1) * GLA_CHUNK)

    zl = jnp.dot(z_ref[...], wd_ref[...], preferred_element_type=F32) + bd_ref[...]
    la = _log_sigmoid(zl) / GLA_TAU

    hi = la.astype(BF16)
    r1 = la - hi.astype(F32)
    mid = r1.astype(BF16)
    lo = (r1 - mid.astype(F32)).astype(BF16)
    pieces = jnp.concatenate([hi, mid, lo], axis=1)
    row = lax.broadcasted_iota(jnp.int32, (GLA_CHUNK, GLA_CHUNK), 0)
    col = lax.broadcasted_iota(jnp.int32, (GLA_CHUNK, GLA_CHUNK), 1)
    tril_bf = jnp.where(row >= col, 1.0, 0.0).astype(BF16)
    cums, totals, decay = [], [], []
    for c in range(n_chunks):
        cs = jnp.dot(tril_bf, pieces[crow(c), :], preferred_element_type=F32)
        cum_c = cs[:, :GLA_QK_DIM] + cs[:, GLA_QK_DIM:2 * GLA_QK_DIM] + cs[:, 2 * GLA_QK_DIM:]
        last = cum_c[GLA_CHUNK - 1:GLA_CHUNK, :]
        cums.append(cum_c)
        totals.append(jnp.broadcast_to(last, (GLA_CHUNK, GLA_QK_DIM)))
        decay.append(jnp.exp(last))
    cum = jnp.concatenate(cums, axis=0)
    total = jnp.concatenate(totals, axis=0)

    q = q_ref[...].astype(F32)
    k = k_ref[...].astype(F32)
    q_dec = (q * (GLA_DK ** -0.5) * jnp.exp(cum)).astype(BF16)
    k_dec = k * jnp.exp(-cum)
    k_end = k * jnp.exp(total - cum)
    lane = lax.broadcasted_iota(jnp.int32, (1, LANES), 1)
    head_mask = (jnp.where(lane < GLA_DK, 1.0, 0.0), jnp.where(lane >= GLA_DK, 1.0, 0.0))
    k_dec_h = [(k_dec[:, pair(h)] * head_mask[h % 2]).astype(BF16) for h in range(GLA_HEADS)]
    k_end_h = [(k_end[:, pair(h)] * head_mask[h % 2]).astype(BF16) for h in range(GLA_HEADS)]

    state_in = [[None] * GLA_HEADS for _ in range(n_chunks)]
    for h in range(GLA_HEADS):
        st = st_ref[h]
        for c in range(n_chunks):
            state_in[c][h] = st.astype(BF16)
            st = st * decay[c][:, pair(h)] + _tn_dot(v_ref[crow(c), vcols(h)], k_end_h[h][crow(c), :])
        st_ref[h] = st

    row = lax.broadcasted_iota(jnp.int32, (GLA_BLOCK, GLA_BLOCK), 0)
    col = lax.broadcasted_iota(jnp.int32, (GLA_BLOCK, GLA_BLOCK), 1)
    visible = (row >= col) & (row // GLA_CHUNK == col // GLA_CHUNK)
    gn = gn_ref[...]
    for b in range(TC_GLA // GLA_BLOCK):
        rb = slice(b * GLA_BLOCK, (b + 1) * GLA_BLOCK)
        for h in range(GLA_HEADS):
            scores = jnp.where(visible, _nt_dot(q_dec[rb, pair(h)], k_dec_h[h][rb, :]), 0.0).astype(BF16)
            o = jnp.dot(scores, v_ref[rb, vcols(h)], preferred_element_type=F32)
            o = o + jnp.concatenate(
                [_nt_dot(q_dec[crow(c), pair(h)], state_in[c][h])
                 for c in range(b * chunks_per_block, (b + 1) * chunks_per_block)], axis=0)
            g = go_ref[rb, vcols(h)].astype(F32)
            y_ref[rb, vcols(h)] = (_rms(o, gn) * (g * jax.nn.sigmoid(g))).astype(BF16)


def _gla(layer, proj, w_decay, b_decay, norm_gain, batch, seq):
    n = proj.shape[0]
    tps = seq // TC_GLA
    row = lambda b, t: b * tps + t
    return pl.pallas_call(
        _gla_kernel,
        out_shape=jax.ShapeDtypeStruct((n, GLA_V_DIM), BF16),
        grid=(batch, tps),
        in_specs=[
            pl.BlockSpec((TC_GLA, GLA_QK_DIM), lambda b, t: (row(b, t), C_GQ // GLA_QK_DIM)),
            pl.BlockSpec((TC_GLA, GLA_QK_DIM), lambda b, t: (row(b, t), C_GK // GLA_QK_DIM)),
            pl.BlockSpec((TC_GLA, GLA_V_DIM), lambda b, t: (row(b, t), C_GV // GLA_V_DIM)),
            pl.BlockSpec((TC_GLA, GLA_V_DIM), lambda b, t: (row(b, t), C_GO // GLA_V_DIM)),
            pl.BlockSpec((TC_GLA, Z_PAD), lambda b, t: (row(b, t), C_Z // Z_PAD)),
            _layer_spec(layer, (Z_PAD, GLA_QK_DIM)),
            _layer_spec(layer, (1, GLA_QK_DIM)),
            _layer_spec(layer, (1, GLA_DV)),
        ],
        out_specs=pl.BlockSpec((TC_GLA, GLA_V_DIM), lambda b, t: (row(b, t), 0)),
        scratch_shapes=[pltpu.VMEM((GLA_HEADS, GLA_DV, LANES), F32)],
        compiler_params=pltpu.CompilerParams(
            dimension_semantics=("parallel", "arbitrary"), vmem_limit_bytes=VMEM_LIMIT),
        name="gla",
    )(proj, proj, proj, proj, proj, w_decay, b_decay, norm_gain)


ONES_ROWS = BF16_SUBLANES
LOG2E = 1.4426950408889634


def _diff_kernel(lam_ref, q_ref, k_ref, v_ref, bias_ref, gn_ref, y_ref,
                 qs_ref, vt_ref, s0_ref, s1_ref, mx0_ref, mx1_ref, m_ref, acc_ref, *, lam_init):
    i = pl.program_id(1)
    t = T_ATT
    lane = lax.broadcasted_iota(jnp.int32, (1, LANES), 1)
    for h in range(DIFF_HEADS):
        q = q_ref[:, h * LANES:(h + 1) * LANES].astype(F32) * (DIFF_DK ** -0.5 * LOG2E)
        qs_ref[h, 0:t, :] = jnp.where(lane < DIFF_DK, q, 0.0).astype(BF16)
        qs_ref[h, t:2 * t, :] = jnp.where(lane >= DIFF_DK, q, 0.0).astype(BF16)

    def rows(j):
        return pl.ds(pl.multiple_of(j * t, t), t)

    def head_cols(h):
        return slice(h * LANES, (h + 1) * LANES)

    m_ref[...] = jnp.full_like(m_ref, -1e30)
    acc_ref[...] = jnp.zeros_like(acc_ref)

    @pl.when(i == 0)
    def _():
        vt_ref[:, :, DIFF_DV:, :] = jnp.ones((DIFF_HEADS, vt_ref.shape[1], ONES_ROWS, t), BF16)

        def transpose_block(jb, carry):
            for h in range(DIFF_HEADS):
                vt_ref[h, jb, 0:DIFF_DV, :] = v_ref[rows(jb), head_cols(h)].T
            return carry

        lax.fori_loop(0, vt_ref.shape[1], transpose_block, 0)

    def logits(j, buf):
        s_ref, mx_ref = buf
        for h in range(DIFF_HEADS):
            s = _nt_dot(k_ref[rows(j), head_cols(h)], qs_ref[h])
            s_ref[h] = s
            mx_ref[h] = jnp.max(s, axis=0, keepdims=True)

    def softmax_pv(j, buf, bias_cols):
        s_ref, mx_ref = buf
        for h in range(DIFF_HEADS):
            s = s_ref[h]
            if bias_cols is None:
                s_max = mx_ref[h]
            else:
                bias = bias_ref[h, :, bias_cols]
                s = s + jnp.concatenate([bias, bias], axis=1)
                s_max = jnp.max(s, axis=0, keepdims=True)
            m_old = m_ref[h]
            m_new = jnp.maximum(m_old, s_max)
            alpha = jnp.exp2(m_old - m_new)
            p = jnp.exp2(s - m_new).astype(BF16)
            m_ref[h] = m_new
            acc_ref[h] = alpha * acc_ref[h] + jnp.dot(vt_ref[h, j], p, preferred_element_type=F32)

    s0 = (s0_ref, mx0_ref)
    s1 = (s1_ref, mx1_ref)
    n_far = jnp.maximum(i - 1, 0)
    peeled = n_far % 2

    @pl.when(i % 2 == 0)
    def _():
        logits(0, s0)

    @pl.when(i % 2 == 1)
    def _():
        logits(0, s1)

    @pl.when(peeled == 1)
    def _():
        logits(1, s1)
        softmax_pv(0, s0, None)

    def far_pair(j):
        logits(j + 1, s0)
        softmax_pv(j, s1, None)
        logits(j + 2, s1)
        softmax_pv(j + 1, s0, None)

    n_pairs = n_far // 2
    peeled_pair = n_pairs % 2

    @pl.when(peeled_pair == 1)
    def _():
        far_pair(peeled)

    def far_quad(jj, carry):
        j = peeled + 2 * peeled_pair + 4 * jj
        far_pair(j)
        far_pair(j + 2)
        return carry

    lax.fori_loop(0, n_pairs // 2, far_quad, 0)

    @pl.when(i >= 1)
    def _():
        logits(i, s0)
        softmax_pv(i - 1, s1, slice(t, 2 * t))

    softmax_pv(i, s0, slice(0, t))

    lam_p = lam_ref[...]
    lam = (jnp.exp(jnp.sum(lam_p[0:1] * lam_p[1:2], axis=1, keepdims=True))
           - jnp.exp(jnp.sum(lam_p[2:3] * lam_p[3:4], axis=1, keepdims=True)) + lam_init)
    for h in range(DIFF_HEADS):
        acc = acc_ref[h, 0:DIFF_DV, :]
        inv_l = 1.0 / acc_ref[h, DIFF_DV:DIFF_DV + 1, :]
        o = acc[:, :t] * inv_l[:, :t] - lam * (acc[:, t:] * inv_l[:, t:])
        ms = jnp.mean(o * o, axis=0, keepdims=True)
        o = o * lax.rsqrt(ms + RMS_EPS) * gn_ref[...] * (1.0 - lam_init)
        y_ref[:, h * DIFF_DV:(h + 1) * DIFF_DV] = o.T.astype(BF16)


def _diff(layer, proj, lam, bias, norm_gain_col, lam_init, batch, seq):
    n = proj.shape[0]
    nq = seq // T_ATT
    hw = DIFF_HEADS * LANES
    return pl.pallas_call(
        functools.partial(_diff_kernel, lam_init=lam_init),
        out_shape=jax.ShapeDtypeStruct((n, DIFF_HEADS * DIFF_DV), BF16),
        grid=(batch, nq),
        in_specs=[
            _layer_spec(layer, (4, DIFF_DK)),
            pl.BlockSpec((T_ATT, hw), lambda b, i: (b * nq + i, C_DQ // hw)),
            pl.BlockSpec((seq, hw), lambda b, i: (b, C_DK // hw), pipeline_mode=pl.Buffered(1)),
            pl.BlockSpec((seq, hw), lambda b, i: (b, C_DV // hw), pipeline_mode=pl.Buffered(1)),
            _const_spec((DIFF_HEADS, T_ATT, 2 * T_ATT)),
            _layer_spec(layer, (DIFF_DV, 1)),
        ],
        out_specs=pl.BlockSpec((T_ATT, DIFF_HEADS * DIFF_DV), lambda b, i: (b * nq + i, 0)),
        scratch_shapes=[
            pltpu.VMEM((DIFF_HEADS, 2 * T_ATT, LANES), BF16),
            pltpu.VMEM((DIFF_HEADS, nq, DIFF_DV + ONES_ROWS, T_ATT), BF16),
            pltpu.VMEM((DIFF_HEADS, T_ATT, 2 * T_ATT), F32),
            pltpu.VMEM((DIFF_HEADS, T_ATT, 2 * T_ATT), F32),
            pltpu.VMEM((DIFF_HEADS, 1, 2 * T_ATT), F32),
            pltpu.VMEM((DIFF_HEADS, 1, 2 * T_ATT), F32),
            pltpu.VMEM((DIFF_HEADS, 1, 2 * T_ATT), F32),
            pltpu.VMEM((DIFF_HEADS, DIFF_DV + ONES_ROWS, 2 * T_ATT), F32),
        ],
        compiler_params=pltpu.CompilerParams(
            dimension_semantics=("parallel", "arbitrary"), vmem_limit_bytes=VMEM_LIMIT),
        name="diff_attn",
    )(lam, proj, proj, proj, bias, norm_gain_col)


def _t5_bucket(rel):
    n = jnp.maximum(rel, 0)
    max_exact = REL_BUCKETS // 2
    nf = jnp.maximum(n, 1).astype(F32)
    large = max_exact + (jnp.log(nf / max_exact) / math.log(REL_MAX_DIST / max_exact)
                         * (REL_BUCKETS - max_exact)).astype(jnp.int32)
    large = jnp.minimum(large, REL_BUCKETS - 1)
    return jnp.where(n < max_exact, n, large)


def _bias_tiles(rel_bias):
    assert T_ATT + 1 >= REL_MAX_DIST
    t = T_ATT
    table = rel_bias.astype(F32)
    near = (table[_t5_bucket(jnp.arange(2 * t, dtype=jnp.int32))] - table[REL_BUCKETS - 1]) * LOG2E
    by_rel = jnp.concatenate([near, jnp.full((t, DIFF_HEADS), -jnp.inf, F32)], axis=0).T
    period = 3 * t
    flat = jnp.tile(by_rel, (1, t))[:, :t * (period - 1)]
    return flat.reshape(DIFF_HEADS, t, period - 1)[:, :, :2 * t]


def _merge_kernel(x_ref, g0_ref, g1_ref, g2_ref, u_ref, halo_ref, ygla_ref, ydiff_ref,
                  pw_ref, ps_ref, wb_ref, wo_ref, gpost_ref, o_ref, *, tiles_per_seq):
    i = pl.program_id(0)
    tm = TM_MERGE
    t_in_seq = i % tiles_per_seq
    halo = jnp.where(t_in_seq == 0, 0.0, halo_ref[...].astype(F32))
    ucat = jnp.concatenate([halo, u_ref[...].astype(F32)], axis=0)
    pos = t_in_seq * tm + lax.broadcasted_iota(jnp.int32, (tm, 1), 0)
    mixed = []
    for g, w in enumerate(POOL_WINDOWS):
        cols = slice(g * POOL_GROUP_DIM, (g + 1) * POOL_GROUP_DIM)
        win = ucat[:, cols]
        span = 1
        while span < w:
            win = win + pltpu.roll(win, span, axis=0)
            span *= 2
        cur = ucat[POOL_HALO:, cols]
        win = win[POOL_HALO:, :]
        cnt = jnp.minimum(pos + 1, w).astype(F32)
        pooled = win / cnt - cur
        mixed.append(jnp.dot(pooled.astype(BF16), pw_ref[g], preferred_element_type=F32))
    y_pool = (jnp.concatenate(mixed, axis=1) * ps_ref[...]).astype(BF16)

    merged = jax.nn.sigmoid(g0_ref[...].astype(F32)) * jnp.dot(y_pool, wb_ref[0], preferred_element_type=F32)
    merged = merged + jax.nn.sigmoid(g1_ref[...].astype(F32)) * jnp.dot(
        ygla_ref[...], wb_ref[1], preferred_element_type=F32)
    merged = merged + jax.nn.sigmoid(g2_ref[...].astype(F32)) * jnp.dot(
        ydiff_ref[...], wb_ref[2], preferred_element_type=F32)
    out = jnp.dot(merged.astype(BF16), wo_ref[...], preferred_element_type=F32)
    o_ref[...] = x_ref[...] + _rms(out, gpost_ref[...])


def _merge(layer, x2, proj, y_gla, y_diff, pool_w, pool_scale, w_branch, w_o, gpost, seq):
    n = x2.shape[0]
    tm = TM_MERGE
    halo_blocks = tm // POOL_HALO
    return pl.pallas_call(
        functools.partial(_merge_kernel, tiles_per_seq=seq // tm),
        out_shape=jax.ShapeDtypeStruct((n, D_MODEL), F32),
        grid=(n // tm,),
        in_specs=[
            pl.BlockSpec((tm, D_MODEL), lambda i: (i, 0)),
            pl.BlockSpec((tm, D_MODEL), lambda i: (i, C_GATE // D_MODEL + 0)),
            pl.BlockSpec((tm, D_MODEL), lambda i: (i, C_GATE // D_MODEL + 1)),
            pl.BlockSpec((tm, D_MODEL), lambda i: (i, C_GATE // D_MODEL + 2)),
            pl.BlockSpec((tm, POOL_DIM), lambda i: (i, C_U // POOL_DIM)),
            pl.BlockSpec((POOL_HALO, POOL_DIM),
                         lambda i: (jnp.maximum(i * halo_blocks - 1, 0), C_U // POOL_DIM)),
            pl.BlockSpec((tm, BRANCH_DIM), lambda i: (i, 0)),
            pl.BlockSpec((tm, BRANCH_DIM), lambda i: (i, 0)),
            _layer_spec(layer, (len(POOL_WINDOWS), POOL_GROUP_DIM, POOL_GROUP_DIM)),
            _layer_spec(layer, (1, POOL_DIM)),
            _layer_spec(layer, (3, BRANCH_DIM, D_MODEL)),
            _layer_spec(layer, (D_MODEL, D_MODEL)),
            _layer_spec(layer, (1, D_MODEL)),
        ],
        out_specs=pl.BlockSpec((tm, D_MODEL), lambda i: (i, 0)),
        compiler_params=pltpu.CompilerParams(
            dimension_semantics=("parallel",), vmem_limit_bytes=VMEM_LIMIT),
        name="merge",
    )(x2, proj, proj, proj, proj, proj, y_gla, y_diff, pool_w, pool_scale, w_branch, w_o, gpost)


def _ffn_kernel(x_ref, gpre_ref, wg_ref, wu_ref, wd_ref, gpost_ref, o_ref):
    x = x_ref[...]
    h = _rms(x, gpre_ref[...]).astype(BF16)
    g = jnp.dot(h, wg_ref[...], preferred_element_type=F32)
    u = jnp.dot(h, wu_ref[...], preferred_element_type=F32)
    a = (g * jax.nn.sigmoid(g) * u).astype(BF16)
    f = jnp.dot(a, wd_ref[...], preferred_element_type=F32)
    o_ref[...] = x + _rms(f, gpost_ref[...])


def _ffn(layer, x2, gpre, wg, wu, wd, gpost):
    n = x2.shape[0]
    tm = TM_FFN
    return pl.pallas_call(
        _ffn_kernel,
        out_shape=jax.ShapeDtypeStruct((n, D_MODEL), F32),
        grid=(n // tm,),
        in_specs=[
            pl.BlockSpec((tm, D_MODEL), lambda i: (i, 0)),
            _layer_spec(layer, (1, D_MODEL)),
            _layer_spec(layer, (D_MODEL, FFN_DIM)),
            _layer_spec(layer, (D_MODEL, FFN_DIM)),
            _layer_spec(layer, (FFN_DIM, D_MODEL)),
            _layer_spec(layer, (1, D_MODEL)),
        ],
        out_specs=pl.BlockSpec((tm, D_MODEL), lambda i: (i, 0)),
        compiler_params=pltpu.CompilerParams(
            dimension_semantics=("parallel",), vmem_limit_bytes=VMEM_LIMIT),
        name="ffn",
    )(x2, gpre, wg, wu, wd, gpost)


def _split_w_in(w):
    z0 = int(sum(IN_SPLITS[:5]))
    assert z0 == W_HEAD and w.shape[-1] - z0 - GLA_GATE_RANK == W_TAIL
    w = w.astype(BF16)
    w_z = jnp.pad(w[..., z0:z0 + GLA_GATE_RANK], ((0, 0), (0, 0), (0, Z_PAD - GLA_GATE_RANK)))
    return w[..., :z0], w_z, w[..., z0 + GLA_GATE_RANK:]


def kernel(x, rel_bias, ln_mix_pre, w_in, pool_w, pool_scale, gla_w_decay, gla_b_decay, gla_norm,
           diff_lambda, diff_norm, w_branch, w_o, ln_mix_post, ln_ffn_pre, ffn_w_gate, ffn_w_up,
           ffn_w_down, ln_ffn_post):
    batch, seq, d = x.shape
    depth = w_in.shape[0]
    assert POOL_HALO >= max(POOL_WINDOWS)
    assert d == D_MODEL and seq % TC_GLA == 0 and seq % T_ATT == 0 and seq % TM_MERGE == 0
    x2 = x.reshape(batch * seq, d)
    bias = _bias_tiles(rel_bias)
    row = lambda p: p[:, None, :]
    w_head, w_z, w_tail = _split_w_in(w_in)
    w_decay = jnp.pad(gla_w_decay, ((0, 0), (0, Z_PAD - GLA_GATE_RANK), (0, 0))).astype(BF16)
    pool_w, w_branch, w_o = pool_w.astype(BF16), w_branch.astype(BF16), w_o.astype(BF16)
    ffn_w_gate, ffn_w_up, ffn_w_down = ffn_w_gate.astype(BF16), ffn_w_up.astype(BF16), ffn_w_down.astype(BF16)
    for l in range(depth):
        lam_init = 0.8 - 0.6 * math.exp(-0.3 * l)
        proj = _in_proj(l, x2, row(ln_mix_pre), w_head, w_z, w_tail)
        y_gla = _gla(l, proj, w_decay, row(gla_b_decay), row(gla_norm), batch, seq)
        y_diff = _diff(l, proj, diff_lambda, bias, diff_norm[:, :, None], lam_init, batch, seq)
        x2 = _merge(l, x2, proj, y_gla, y_diff, pool_w, row(pool_scale), w_branch, w_o, row(ln_mix_post), seq)
        x2 = _ffn(l, x2, row(ln_ffn_pre), ffn_w_gate, ffn_w_up, ffn_w_down, row(ln_ffn_post))
    return x2.reshape(batch, seq, d)
```

```python
import functools
import math

import jax
import jax.numpy as jnp
from jax import lax
from jax.experimental import pallas as pl
from jax.experimental.pallas import tpu as pltpu

F32 = jnp.float32
BF16 = jnp.bfloat16

D_MODEL = 1024
POOL_WINDOWS = (2, 4, 8, 16)
POOL_GROUP_DIM = 128
POOL_DIM = 512
GLA_HEADS = 4
GLA_DK = 64
GLA_DV = 128
GLA_QK_DIM = 256
GLA_V_DIM = 512
GLA_GATE_RANK = 16
GLA_TAU = 16.0
GLA_CHUNK = 64
DIFF_HEADS = 4
DIFF_DK = 64
DIFF_DV = 128
REL_BUCKETS = 32
REL_MAX_DIST = 128
BRANCH_DIM = 512
FFN_DIM = 2816
RMS_EPS = 1e-6
IN_SPLITS = (512, 256, 256, 512, 512, 16, 512, 512, 512, 3072)

LANES = 128
BF16_SUBLANES = 16
MXU_TILE = 256
VMEM_LIMIT = 56 * 1024 * 1024

C_GATE = 0
C_U = 3072
C_GQ = 3584
C_GK = 3840
C_GV = 4096
C_GO = 4608
C_DQ = 5120
C_DK = 5632
C_DV = 6144
C_Z = 6656
Z_PAD = LANES
PROJ_DIM = C_Z + Z_PAD

TM_PROJ = 512
TM_MERGE = 512
TM_FFN = 512
TC_GLA = 2048
GLA_BLOCK = MXU_TILE
T_ATT = 256
POOL_HALO = BF16_SUBLANES


def _rms(x, gain):
    ms = jnp.mean(x * x, axis=-1, keepdims=True)
    return x * lax.rsqrt(ms + RMS_EPS) * gain


def _nt_dot(a, b):
    return lax.dot_general(a, b, (((1,), (1,)), ((), ())), preferred_element_type=F32)


def _tn_dot(a, b):
    return lax.dot_general(a, b, (((0,), (0,)), ((), ())), preferred_element_type=F32)


def _const_spec(shape):
    nd = len(shape)
    return pl.BlockSpec(shape, lambda *_: (0,) * nd, pipeline_mode=pl.Buffered(1))


def _layer_spec(layer, shape):
    nd = len(shape)
    return pl.BlockSpec((None,) + tuple(shape), lambda *_: (layer,) + (0,) * nd, pipeline_mode=pl.Buffered(1))


W_HEAD = C_DQ - C_U
W_TAIL = C_Z - C_DQ + C_U
N_DIFF = C_Z - C_DQ
PROJ_COL_STEP = 4 * MXU_TILE


def _in_proj_kernel(x_ref, g_ref, wh_ref, wz_ref, wt_ref, o_ref):
    h = _rms(x_ref[...], g_ref[...]).astype(BF16)
    step = PROJ_COL_STEP
    segments = ((wt_ref, N_DIFF, W_TAIL - N_DIFF, C_GATE), (wh_ref, 0, W_HEAD, C_U),
                (wt_ref, 0, N_DIFF, C_DQ), (wz_ref, 0, Z_PAD, C_Z))
    for w_ref, src, width, dst in segments:
        for c in range(0, width, step):
            w = min(step, width - c)
            o_ref[:, dst + c:dst + c + w] = jnp.dot(
                h, w_ref[:, src + c:src + c + w], preferred_element_type=F32).astype(BF16)


def _in_proj(layer, x2, gain, w_head, w_z, w_tail):
    n = x2.shape[0]
    return pl.pallas_call(
        _in_proj_kernel,
        out_shape=jax.ShapeDtypeStruct((n, PROJ_DIM), BF16),
        grid=(n // TM_PROJ,),
        in_specs=[
            pl.BlockSpec((TM_PROJ, D_MODEL), lambda i: (i, 0)),
            _layer_spec(layer, (1, D_MODEL)),
            _layer_spec(layer, (D_MODEL, W_HEAD)),
            _layer_spec(layer, (D_MODEL, Z_PAD)),
            _layer_spec(layer, (D_MODEL, W_TAIL)),
        ],
        out_specs=pl.BlockSpec((TM_PROJ, PROJ_DIM), lambda i: (i, 0)),
        compiler_params=pltpu.CompilerParams(
            dimension_semantics=("parallel",), vmem_limit_bytes=VMEM_LIMIT),
        name="in_proj",
    )(x2, gain, w_head, w_z, w_tail)


def _log_sigmoid(x):
    return jnp.minimum(x, 0.0) - jnp.log1p(jnp.exp(-jnp.abs(x)))


def _gla_kernel(q_ref, k_ref, v_ref, go_ref, z_ref, wd_ref, bd_ref, gn_ref, y_ref, st_ref):
    n_chunks = TC_GLA // GLA_CHUNK
    chunks_per_block = GLA_BLOCK // GLA_CHUNK

    @pl.when(pl.program_id(1) == 0)
    def _():
        st_ref[...] = jnp.zeros_like(st_ref)

    def pair(h):
        return slice((h // 2) * LANES, (h // 2 + 1) * LANES)

    def vcols(h):
        return slice(h * GLA_DV, (h + 1) * GLA_DV)

    def crow(c):
        return slice(c * GLA_CHUNK, (c + 1) * GLA_CHUNK)

    zl = jnp.dot(z_ref[...], wd_ref[...], preferred_element_type=F32) + bd_ref[...]
    la = _log_sigmoid(zl) / GLA_TAU

    hi = la.astype(BF16)
    r1 = la - hi.astype(F32)
    mid = r1.astype(BF16)
    lo = (r1 - mid.astype(F32)).astype(BF16)
    pieces = jnp.concatenate([hi, mid, lo], axis=1)
    row = lax.broadcasted_iota(jnp.int32, (GLA_CHUNK, GLA_CHUNK), 0)
    col = lax.broadcasted_iota(jnp.int32, (GLA_CHUNK, GLA_CHUNK), 1)
    tril_bf = jnp.where(row >= col, 1.0, 0.0).astype(BF16)
    cums, totals, decay = [], [], []
    for c in range(n_chunks):
        cs = jnp.dot(tril_bf, pieces[crow(c), :], preferred_element_type=F32)
        cum_c = cs[:, :GLA_QK_DIM] + cs[:, GLA_QK_DIM:2 * GLA_QK_DIM] + cs[:, 2 * GLA_QK_DIM:]
        last = cum_c[GLA_CHUNK - 1:GLA_CHUNK, :]
        cums.append(cum_c)
        totals.append(jnp.broadcast_to(last, (GLA_CHUNK, GLA_QK_DIM)))
        decay.append(jnp.exp(last))
    cum = jnp.concatenate(cums, axis=0)
    total = jnp.concatenate(totals, axis=0)

    q = q_ref[...].astype(F32)
    k = k_ref[...].astype(F32)
    q_dec = (q * (GLA_DK ** -0.5) * jnp.exp(cum)).astype(BF16)
    k_dec = k * jnp.exp(-cum)
    k_end = k * jnp.exp(total - cum)
    lane = lax.broadcasted_iota(jnp.int32, (1, LANES), 1)
    head_mask = (jnp.where(lane < GLA_DK, 1.0, 0.0), jnp.where(lane >= GLA_DK, 1.0, 0.0))
    k_dec_h = [(k_dec[:, pair(h)] * head_mask[h % 2]).astype(BF16) for h in range(GLA_HEADS)]
    k_end_h = [(k_end[:, pair(h)] * head_mask[h % 2]).astype(BF16) for h in range(GLA_HEADS)]

    state_in = [[None] * GLA_HEADS for _ in range(n_chunks)]
    for h in range(GLA_HEADS):
        st = st_ref[h]
        for c in range(n_chunks):
            state_in[c][h] = st.astype(BF16)
            st = st * decay[c][:, pair(h)] + _tn_dot(v_ref[crow(c), vcols(h)], k_end_h[h][crow(c), :])
        st_ref[h] = st

    row = lax.broadcasted_iota(jnp.int32, (GLA_BLOCK, GLA_BLOCK), 0)
    col = lax.broadcasted_iota(jnp.int32, (GLA_BLOCK, GLA_BLOCK), 1)
    visible = (row >= col) & (row // GLA_CHUNK == col // GLA_CHUNK)
    gn = gn_ref[...]
    for b in range(TC_GLA // GLA_BLOCK):
        rb = slice(b * GLA_BLOCK, (b + 1) * GLA_BLOCK)
        for h in range(GLA_HEADS):
            scores = jnp.where(visible, _nt_dot(q_dec[rb, pair(h)], k_dec_h[h][rb, :]), 0.0).astype(BF16)
            o = jnp.dot(scores, v_ref[rb, vcols(h)], preferred_element_type=F32)
            o = o + jnp.concatenate(
                [_nt_dot(q_dec[crow(c), pair(h)], state_in[c][h])
                 for c in range(b * chunks_per_block, (b + 1) * chunks_per_block)], axis=0)
            g = go_ref[rb, vcols(h)].astype(F32)
            y_ref[rb, vcols(h)] = (_rms(o, gn) * (g * jax.nn.sigmoid(g))).astype(BF16)


def _gla(layer, proj, w_decay, b_decay, norm_gain, batch, seq):
    n = proj.shape[0]
    tps = seq // TC_GLA
    row = lambda b, t: b * tps + t
    return pl.pallas_call(
        _gla_kernel,
        out_shape=jax.ShapeDtypeStruct((n, GLA_V_DIM), BF16),
        grid=(batch, tps),
        in_specs=[
            pl.BlockSpec((TC_GLA, GLA_QK_DIM), lambda b, t: (row(b, t), C_GQ // GLA_QK_DIM)),
            pl.BlockSpec((TC_GLA, GLA_QK_DIM), lambda b, t: (row(b, t), C_GK // GLA_QK_DIM)),
            pl.BlockSpec((TC_GLA, GLA_V_DIM), lambda b, t: (row(b, t), C_GV // GLA_V_DIM)),
            pl.BlockSpec((TC_GLA, GLA_V_DIM), lambda b, t: (row(b, t), C_GO // GLA_V_DIM)),
            pl.BlockSpec((TC_GLA, Z_PAD), lambda b, t: (row(b, t), C_Z // Z_PAD)),
            _layer_spec(layer, (Z_PAD, GLA_QK_DIM)),
            _layer_spec(layer, (1, GLA_QK_DIM)),
            _layer_spec(layer, (1, GLA_DV)),
        ],
        out_specs=pl.BlockSpec((TC_GLA, GLA_V_DIM), lambda b, t: (row(b, t), 0)),
        scratch_shapes=[pltpu.VMEM((GLA_HEADS, GLA_DV, LANES), F32)],
        compiler_params=pltpu.CompilerParams(
            dimension_semantics=("parallel", "arbitrary"), vmem_limit_bytes=VMEM_LIMIT),
        name="gla",
    )(proj, proj, proj, proj, proj, w_decay, b_decay, norm_gain)


ONES_ROWS = BF16_SUBLANES
LOG2E = 1.4426950408889634


def _diff_kernel(lam_ref, q_ref, k_ref, v_ref, bias_ref, gn_ref, y_ref,
                 qs_ref, vt_ref, s0_ref, s1_ref, mx0_ref, mx1_ref, m_ref, acc_ref, *, lam_init):
    i = pl.program_id(1)
    t = T_ATT
    lane = lax.broadcasted_iota(jnp.int32, (1, LANES), 1)
    for h in range(DIFF_HEADS):
        q = q_ref[:, h * LANES:(h + 1) * LANES].astype(F32) * (DIFF_DK ** -0.5 * LOG2E)
        qs_ref[h, 0:t, :] = jnp.where(lane < DIFF_DK, q, 0.0).astype(BF16)
        qs_ref[h, t:2 * t, :] = jnp.where(lane >= DIFF_DK, q, 0.0).astype(BF16)

    def rows(j):
        return pl.ds(pl.multiple_of(j * t, t), t)

    def head_cols(h):
        return slice(h * LANES, (h + 1) * LANES)

    m_ref[...] = jnp.full_like(m_ref, -1e30)
    acc_ref[...] = jnp.zeros_like(acc_ref)

    @pl.when(i == 0)
    def _():
        vt_ref[:, :, DIFF_DV:, :] = jnp.ones((DIFF_HEADS, vt_ref.shape[1], ONES_ROWS, t), BF16)

        def transpose_block(jb, carry):
            for h in range(DIFF_HEADS):
                vt_ref[h, jb, 0:DIFF_DV, :] = v_ref[rows(jb), head_cols(h)].T
            return carry

        lax.fori_loop(0, vt_ref.shape[1], transpose_block, 0)

    def logits(j, buf):
        s_ref, mx_ref = buf
        for h in range(DIFF_HEADS):
            s = _nt_dot(k_ref[rows(j), head_cols(h)], qs_ref[h])
            s_ref[h] = s
            mx_ref[h] = jnp.max(s, axis=0, keepdims=True)

    def softmax_pv(j, buf, bias_cols):
        s_ref, mx_ref = buf
        for h in range(DIFF_HEADS):
            s = s_ref[h]
            if bias_cols is None:
                s_max = mx_ref[h]
            else:
                bias = bias_ref[h, :, bias_cols]
                s = s + jnp.concatenate([bias, bias], axis=1)
                s_max = jnp.max(s, axis=0, keepdims=True)
            m_old = m_ref[h]
            m_new = jnp.maximum(m_old, s_max)
            alpha = jnp.exp2(m_old - m_new)
            p = jnp.exp2(s - m_new).astype(BF16)
            m_ref[h] = m_new
            acc_ref[h] = alpha * acc_ref[h] + jnp.dot(vt_ref[h, j], p, preferred_element_type=F32)

    s0 = (s0_ref, mx0_ref)
    s1 = (s1_ref, mx1_ref)
    n_far = jnp.maximum(i - 1, 0)
    peeled = n_far % 2

    @pl.when(i % 2 == 0)
    def _():
        logits(0, s0)

    @pl.when(i % 2 == 1)
    def _():
        logits(0, s1)

    @pl.when(peeled == 1)
    def _():
        logits(1, s1)
        softmax_pv(0, s0, None)

    def far_pair(j):
        logits(j + 1, s0)
        softmax_pv(j, s1, None)
        logits(j + 2, s1)
        softmax_pv(j + 1, s0, None)

    n_pairs = n_far // 2
    peeled_pair = n_pairs % 2

    @pl.when(peeled_pair == 1)
    def _():
        far_pair(peeled)

    def far_quad(jj, carry):
        j = peeled + 2 * peeled_pair + 4 * jj
        far_pair(j)
        far_pair(j + 2)
        return carry

    lax.fori_loop(0, n_pairs // 2, far_quad, 0)

    def finish():
        lam_p = lam_ref[...]
        lam = (jnp.exp(jnp.sum(lam_p[0:1] * lam_p[1:2], axis=1, keepdims=True))
               - jnp.exp(jnp.sum(lam_p[2:3] * lam_p[3:4], axis=1, keepdims=True)) + lam_init)
        for h in range(DIFF_HEADS):
            acc = acc_ref[h, 0:DIFF_DV, :]
            inv_l = 1.0 / acc_ref[h, DIFF_DV:DIFF_DV + 1, :]
            o = acc[:, :t] * inv_l[:, :t] - lam * (acc[:, t:] * inv_l[:, t:])
            ms = jnp.mean(o * o, axis=0, keepdims=True)
            o = o * lax.rsqrt(ms + RMS_EPS) * gn_ref[...] * (1.0 - lam_init)
            y_ref[:, h * DIFF_DV:(h + 1) * DIFF_DV] = o.T.astype(BF16)

    @pl.when(i >= 1)
    def _():
        logits(i, s0)
        softmax_pv(i - 1, s1, slice(t, 2 * t))
        softmax_pv(i, s0, slice(0, t))
        finish()

    @pl.when(i == 0)
    def _():
        softmax_pv(i, s0, slice(0, t))
        finish()


def _diff(layer, proj, lam, bias, norm_gain_col, lam_init, batch, seq):
    n = proj.shape[0]
    nq = seq // T_ATT
    hw = DIFF_HEADS * LANES
    return pl.pallas_call(
        functools.partial(_diff_kernel, lam_init=lam_init),
        out_shape=jax.ShapeDtypeStruct((n, DIFF_HEADS * DIFF_DV), BF16),
        grid=(batch, nq),
        in_specs=[
            _layer_spec(layer, (4, DIFF_DK)),
            pl.BlockSpec((T_ATT, hw), lambda b, i: (b * nq + i, C_DQ // hw)),
            pl.BlockSpec((seq, hw), lambda b, i: (b, C_DK // hw)),
            pl.BlockSpec((seq, hw), lambda b, i: (b, C_DV // hw)),
            _const_spec((DIFF_HEADS, T_ATT, 2 * T_ATT)),
            _layer_spec(layer, (DIFF_DV, 1)),
        ],
        out_specs=pl.BlockSpec((T_ATT, DIFF_HEADS * DIFF_DV), lambda b, i: (b * nq + i, 0)),
        scratch_shapes=[
            pltpu.VMEM((DIFF_HEADS, 2 * T_ATT, LANES), BF16),
            pltpu.VMEM((DIFF_HEADS, nq, DIFF_DV + ONES_ROWS, T_ATT), BF16),
            pltpu.VMEM((DIFF_HEADS, T_ATT, 2 * T_ATT), F32),
            pltpu.VMEM((DIFF_HEADS, T_ATT, 2 * T_ATT), F32),
            pltpu.VMEM((DIFF_HEADS, 1, 2 * T_ATT), F32),
            pltpu.VMEM((DIFF_HEADS, 1, 2 * T_ATT), F32),
            pltpu.VMEM((DIFF_HEADS, 1, 2 * T_ATT), F32),
            pltpu.VMEM((DIFF_HEADS, DIFF_DV + ONES_ROWS, 2 * T_ATT), F32),
        ],
        compiler_params=pltpu.CompilerParams(
            dimension_semantics=("parallel", "arbitrary"), vmem_limit_bytes=VMEM_LIMIT),
        name="diff_attn",
    )(lam, proj, proj, proj, bias, norm_gain_col)


def _t5_bucket(rel):
    n = jnp.maximum(rel, 0)
    max_exact = REL_BUCKETS // 2
    nf = jnp.maximum(n, 1).astype(F32)
    large = max_exact + (jnp.log(nf / max_exact) / math.log(REL_MAX_DIST / max_exact)
                         * (REL_BUCKETS - max_exact)).astype(jnp.int32)
    large = jnp.minimum(large, REL_BUCKETS - 1)
    return jnp.where(n < max_exact, n, large)


def _bias_tiles(rel_bias):
    assert T_ATT + 1 >= REL_MAX_DIST
    t = T_ATT
    table = rel_bias.astype(F32)
    near = (table[_t5_bucket(jnp.arange(2 * t, dtype=jnp.int32))] - table[REL_BUCKETS - 1]) * LOG2E
    by_rel = jnp.concatenate([near, jnp.full((t, DIFF_HEADS), -jnp.inf, F32)], axis=0).T
    period = 3 * t
    flat = jnp.tile(by_rel, (1, t))[:, :t * (period - 1)]
    return flat.reshape(DIFF_HEADS, t, period - 1)[:, :, :2 * t]


def _merge_kernel(x_ref, g0_ref, g1_ref, g2_ref, u_ref, halo_ref, ygla_ref, ydiff_ref,
                  pw_ref, ps_ref, wb_ref, wo_ref, gpost_ref, o_ref, *, tiles_per_seq):
    i = pl.program_id(0)
    tm = TM_MERGE
    t_in_seq = i % tiles_per_seq
    halo = jnp.where(t_in_seq == 0, 0.0, halo_ref[...].astype(F32))
    ucat = jnp.concatenate([halo, u_ref[...].astype(F32)], axis=0)
    pos = t_in_seq * tm + lax.broadcasted_iota(jnp.int32, (tm, 1), 0)
    mixed = []
    for g, w in enumerate(POOL_WINDOWS):
        cols = slice(g * POOL_GROUP_DIM, (g + 1) * POOL_GROUP_DIM)
        win = ucat[:, cols]
        span = 1
        while span < w:
            win = win + pltpu.roll(win, span, axis=0)
            span *= 2
        cur = ucat[POOL_HALO:, cols]
        win = win[POOL_HALO:, :]
        cnt = jnp.minimum(pos + 1, w).astype(F32)
        pooled = win / cnt - cur
        mixed.append(jnp.dot(pooled.astype(BF16), pw_ref[g], preferred_element_type=F32))
    y_pool = (jnp.concatenate(mixed, axis=1) * ps_ref[...]).astype(BF16)

    merged = jax.nn.sigmoid(g0_ref[...].astype(F32)) * jnp.dot(y_pool, wb_ref[0], preferred_element_type=F32)
    merged = merged + jax.nn.sigmoid(g1_ref[...].astype(F32)) * jnp.dot(
        ygla_ref[...], wb_ref[1], preferred_element_type=F32)
    merged = merged + jax.nn.sigmoid(g2_ref[...].astype(F32)) * jnp.dot(
        ydiff_ref[...], wb_ref[2], preferred_element_type=F32)
    out = jnp.dot(merged.astype(BF16), wo_ref[...], preferred_element_type=F32)
    o_ref[...] = x_ref[...] + _rms(out, gpost_ref[...])


def _merge(layer, x2, proj, y_gla, y_diff, pool_w, pool_scale, w_branch, w_o, gpost, seq):
    n = x2.shape[0]
    tm = TM_MERGE
    halo_blocks = tm // POOL_HALO
    return pl.pallas_call(
        functools.partial(_merge_kernel, tiles_per_seq=seq // tm),
        out_shape=jax.ShapeDtypeStruct((n, D_MODEL), F32),
        grid=(n // tm,),
        in_specs=[
            pl.BlockSpec((tm, D_MODEL), lambda i: (i, 0)),
            pl.BlockSpec((tm, D_MODEL), lambda i: (i, C_GATE // D_MODEL + 0)),
            pl.BlockSpec((tm, D_MODEL), lambda i: (i, C_GATE // D_MODEL + 1)),
            pl.BlockSpec((tm, D_MODEL), lambda i: (i, C_GATE // D_MODEL + 2)),
            pl.BlockSpec((tm, POOL_DIM), lambda i: (i, C_U // POOL_DIM)),
            pl.BlockSpec((POOL_HALO, POOL_DIM),
                         lambda i: (jnp.maximum(i * halo_blocks - 1, 0), C_U // POOL_DIM)),
            pl.BlockSpec((tm, BRANCH_DIM), lambda i: (i, 0)),
            pl.BlockSpec((tm, BRANCH_DIM), lambda i: (i, 0)),
            _layer_spec(layer, (len(POOL_WINDOWS), POOL_GROUP_DIM, POOL_GROUP_DIM)),
            _layer_spec(layer, (1, POOL_DIM)),
            _layer_spec(layer, (3, BRANCH_DIM, D_MODEL)),
            _layer_spec(layer, (D_MODEL, D_MODEL)),
            _layer_spec(layer, (1, D_MODEL)),
        ],
        out_specs=pl.BlockSpec((tm, D_MODEL), lambda i: (i, 0)),
        compiler_params=pltpu.CompilerParams(
            dimension_semantics=("parallel",), vmem_limit_bytes=VMEM_LIMIT),
        name="merge",
    )(x2, proj, proj, proj, proj, proj, y_gla, y_diff, pool_w, pool_scale, w_branch, w_o, gpost)


def _ffn_kernel(x_ref, gpre_ref, wg_ref, wu_ref, wd_ref, gpost_ref, o_ref):
    x = x_ref[...]
    h = _rms(x, gpre_ref[...]).astype(BF16)
    g = jnp.dot(h, wg_ref[...], preferred_element_type=F32)
    u = jnp.dot(h, wu_ref[...], preferred_element_type=F32)
    a = (g * jax.nn.sigmoid(g) * u).astype(BF16)
    f = jnp.dot(a, wd_ref[...], preferred_element_type=F32)
    o_ref[...] = x + _rms(f, gpost_ref[...])


def _ffn(layer, x2, gpre, wg, wu, wd, gpost):
    n = x2.shape[0]
    tm = TM_FFN
    return pl.pallas_call(
        _ffn_kernel,
        out_shape=jax.ShapeDtypeStruct((n, D_MODEL), F32),
        grid=(n // tm,),
        in_specs=[
            pl.BlockSpec((tm, D_MODEL), lambda i: (i, 0)),
            _layer_spec(layer, (1, D_MODEL)),
            _layer_spec(layer, (D_MODEL, FFN_DIM)),
            _layer_spec(layer, (D_MODEL, FFN_DIM)),
            _layer_spec(layer, (FFN_DIM, D_MODEL)),
            _layer_spec(layer, (1, D_MODEL)),
        ],
        out_specs=pl.BlockSpec((tm, D_MODEL), lambda i: (i, 0)),
        compiler_params=pltpu.CompilerParams(
            dimension_semantics=("parallel",), vmem_limit_bytes=VMEM_LIMIT),
        name="ffn",
    )(x2, gpre, wg, wu, wd, gpost)


def _split_w_in(w):
    z0 = int(sum(IN_SPLITS[:5]))
    assert z0 == W_HEAD and w.shape[-1] - z0 - GLA_GATE_RANK == W_TAIL
    w = w.astype(BF16)
    w_z = jnp.pad(w[..., z0:z0 + GLA_GATE_RANK], ((0, 0), (0, 0), (0, Z_PAD - GLA_GATE_RANK)))
    return w[..., :z0], w_z, w[..., z0 + GLA_GATE_RANK:]


def kernel(x, rel_bias, ln_mix_pre, w_in, pool_w, pool_scale, gla_w_decay, gla_b_decay, gla_norm,
           diff_lambda, diff_norm, w_branch, w_o, ln_mix_post, ln_ffn_pre, ffn_w_gate, ffn_w_up,
           ffn_w_down, ln_ffn_post):
    batch, seq, d = x.shape
    depth = w_in.shape[0]
    assert POOL_HALO >= max(POOL_WINDOWS)
    assert d == D_MODEL and seq % TC_GLA == 0 and seq % T_ATT == 0 and seq % TM_MERGE == 0
    x2 = x.reshape(batch * seq, d)
    bias = _bias_tiles(rel_bias)
    row = lambda p: p[:, None, :]
    w_head, w_z, w_tail = _split_w_in(w_in)
    w_decay = jnp.pad(gla_w_decay, ((0, 0), (0, Z_PAD - GLA_GATE_RANK), (0, 0))).astype(BF16)
    pool_w, w_branch, w_o = pool_w.astype(BF16), w_branch.astype(BF16), w_o.astype(BF16)
    ffn_w_gate, ffn_w_up, ffn_w_down = ffn_w_gate.astype(BF16), ffn_w_up.astype(BF16), ffn_w_down.astype(BF16)
    for l in range(depth):
        lam_init = 0.8 - 0.6 * math.exp(-0.3 * l)
        proj = _in_proj(l, x2, row(ln_mix_pre), w_head, w_z, w_tail)
        y_gla = _gla(l, proj, w_decay, row(gla_b_decay), row(gla_norm), batch, seq)
        y_diff = _diff(l, proj, diff_lambda, bias, diff_norm[:, :, None], lam_init, batch, seq)
        x2 = _merge(l, x2, proj, y_gla, y_diff, pool_w, row(pool_scale), w_branch, w_o, row(ln_mix_post), seq)
        x2 = _ffn(l, x2, row(ln_ffn_pre), ffn_w_gate, ffn_w_up, ffn_w_down, row(ln_ffn_post))
    return x2.reshape(batch, seq, d)
```

```python
import functools
import math

import jax
import jax.numpy as jnp
from jax import lax
from jax.experimental import pallas as pl
from jax.experimental.pallas import tpu as pltpu

F32 = jnp.float32
BF16 = jnp.bfloat16

D_MODEL = 1024
POOL_WINDOWS = (2, 4, 8, 16)
POOL_GROUP_DIM = 128
POOL_DIM = 512
GLA_HEADS = 4
GLA_DK = 64
GLA_DV = 128
GLA_QK_DIM = 256
GLA_V_DIM = 512
GLA_GATE_RANK = 16
GLA_TAU = 16.0
GLA_CHUNK = 64
DIFF_HEADS = 4
DIFF_DK = 64
DIFF_DV = 128
REL_BUCKETS = 32
REL_MAX_DIST = 128
BRANCH_DIM = 512
FFN_DIM = 2816
RMS_EPS = 1e-6
IN_SPLITS = (512, 256, 256, 512, 512, 16, 512, 512, 512, 3072)

LANES = 128
BF16_SUBLANES = 16
MXU_TILE = 256
VMEM_LIMIT = 56 * 1024 * 1024

C_GATE = 0
C_U = 3072
C_GQ = 3584
C_GK = 3840
C_GV = 4096
C_GO = 4608
C_DQ = 5120
C_DK = 5632
C_DV = 6144
C_Z = 6656
Z_PAD = LANES
PROJ_DIM = C_Z + Z_PAD

TM_PROJ = 512
TM_MERGE = 512
TM_FFN = 512
TC_GLA = 2048
GLA_BLOCK = MXU_TILE
T_ATT = 256
POOL_HALO = BF16_SUBLANES


def _rms(x, gain):
    ms = jnp.mean(x * x, axis=-1, keepdims=True)
    return x * lax.rsqrt(ms + RMS_EPS) * gain


def _nt_dot(a, b):
    return lax.dot_general(a, b, (((1,), (1,)), ((), ())), preferred_element_type=F32)


def _tn_dot(a, b):
    return lax.dot_general(a, b, (((0,), (0,)), ((), ())), preferred_element_type=F32)


def _const_spec(shape):
    nd = len(shape)
    return pl.BlockSpec(shape, lambda *_: (0,) * nd, pipeline_mode=pl.Buffered(1))


def _layer_spec(layer, shape):
    nd = len(shape)
    return pl.BlockSpec((None,) + tuple(shape), lambda *_: (layer,) + (0,) * nd, pipeline_mode=pl.Buffered(1))


W_HEAD = C_DQ - C_U
W_TAIL = C_Z - C_DQ + C_U
N_DIFF = C_Z - C_DQ
PROJ_COL_STEP = 4 * MXU_TILE


def _in_proj_kernel(x_ref, g_ref, wh_ref, wz_ref, wt_ref, o_ref):
    h = _rms(x_ref[...], g_ref[...]).astype(BF16)
    step = PROJ_COL_STEP
    segments = ((wt_ref, N_DIFF, W_TAIL - N_DIFF, C_GATE), (wh_ref, 0, W_HEAD, C_U),
                (wt_ref, 0, N_DIFF, C_DQ), (wz_ref, 0, Z_PAD, C_Z))
    for w_ref, src, width, dst in segments:
        for c in range(0, width, step):
            w = min(step, width - c)
            o_ref[:, dst + c:dst + c + w] = jnp.dot(
                h, w_ref[:, src + c:src + c + w], preferred_element_type=F32).astype(BF16)


def _in_proj(layer, x2, gain, w_head, w_z, w_tail):
    n = x2.shape[0]
    return pl.pallas_call(
        _in_proj_kernel,
        out_shape=jax.ShapeDtypeStruct((n, PROJ_DIM), BF16),
        grid=(n // TM_PROJ,),
        in_specs=[
            pl.BlockSpec((TM_PROJ, D_MODEL), lambda i: (i, 0)),
            _layer_spec(layer, (1, D_MODEL)),
            _layer_spec(layer, (D_MODEL, W_HEAD)),
            _layer_spec(layer, (D_MODEL, Z_PAD)),
            _layer_spec(layer, (D_MODEL, W_TAIL)),
        ],
        out_specs=pl.BlockSpec((TM_PROJ, PROJ_DIM), lambda i: (i, 0)),
        compiler_params=pltpu.CompilerParams(
            dimension_semantics=("parallel",), vmem_limit_bytes=VMEM_LIMIT),
        name="in_proj",
    )(x2, gain, w_head, w_z, w_tail)


def _log_sigmoid(x):
    return jnp.minimum(x, 0.0) - jnp.log1p(jnp.exp(-jnp.abs(x)))


def _gla_kernel(q_ref, k_ref, v_ref, go_ref, z_ref, wd_ref, bd_ref, gn_ref, y_ref, st_ref):
    n_chunks = TC_GLA // GLA_CHUNK
    chunks_per_block = GLA_BLOCK // GLA_CHUNK

    @pl.when(pl.program_id(1) == 0)
    def _():
        st_ref[...] = jnp.zeros_like(st_ref)

    def pair(h):
        return slice((h // 2) * LANES, (h // 2 + 1) * LANES)

    def vcols(h):
        return slice(h * GLA_DV, (h + 1) * GLA_DV)

    def crow(c):
        return slice(c * GLA_CHUNK, (c + 1) * GLA_CHUNK)

    zl = jnp.dot(z_ref[...], wd_ref[...], preferred_element_type=F32) + bd_ref[...]
    la = _log_sigmoid(zl) / GLA_TAU

    hi = la.astype(BF16)
    r1 = la - hi.astype(F32)
    mid = r1.astype(BF16)
    lo = (r1 - mid.astype(F32)).astype(BF16)
    pieces = jnp.concatenate([hi, mid, lo], axis=1)
    row = lax.broadcasted_iota(jnp.int32, (GLA_CHUNK, GLA_CHUNK), 0)
    col = lax.broadcasted_iota(jnp.int32, (GLA_CHUNK, GLA_CHUNK), 1)
    tril_bf = jnp.where(row >= col, 1.0, 0.0).astype(BF16)
    cums, totals, decay = [], [], []
    for c in range(n_chunks):
        cs = jnp.dot(tril_bf, pieces[crow(c), :], preferred_element_type=F32)
        cum_c = cs[:, :GLA_QK_DIM] + cs[:, GLA_QK_DIM:2 * GLA_QK_DIM] + cs[:, 2 * GLA_QK_DIM:]
        last = cum_c[GLA_CHUNK - 1:GLA_CHUNK, :]
        cums.append(cum_c)
        totals.append(jnp.broadcast_to(last, (GLA_CHUNK, GLA_QK_DIM)))
        decay.append(jnp.exp(last))
    cum = jnp.concatenate(cums, axis=0)
    total = jnp.concatenate(totals, axis=0)

    q = q_ref[...].astype(F32)
    k = k_ref[...].astype(F32)
    q_dec = (q * (GLA_DK ** -0.5) * jnp.exp(cum)).astype(BF16)
    k_dec = k * jnp.exp(-cum)
    k_end = k * jnp.exp(total - cum)
    lane = lax.broadcasted_iota(jnp.int32, (1, LANES), 1)
    head_mask = (jnp.where(lane < GLA_DK, 1.0, 0.0), jnp.where(lane >= GLA_DK, 1.0, 0.0))
    k_dec_h = [(k_dec[:, pair(h)] * head_mask[h % 2]).astype(BF16) for h in range(GLA_HEADS)]
    k_end_h = [(k_end[:, pair(h)] * head_mask[h % 2]).astype(BF16) for h in range(GLA_HEADS)]

    state_in = [[None] * GLA_HEADS for _ in range(n_chunks)]
    for h in range(GLA_HEADS):
        st = st_ref[h]
        for c in range(n_chunks):
            state_in[c][h] = st.astype(BF16)
            st = st * decay[c][:, pair(h)] + _tn_dot(v_ref[crow(c), vcols(h)], k_end_h[h][crow(c), :])
        st_ref[h] = st

    row = lax.broadcasted_iota(jnp.int32, (GLA_BLOCK, GLA_BLOCK), 0)
    col = lax.broadcasted_iota(jnp.int32, (GLA_BLOCK, GLA_BLOCK), 1)
    visible = (row >= col) & (row // GLA_CHUNK == col // GLA_CHUNK)
    gn = gn_ref[...]
    for b in range(TC_GLA // GLA_BLOCK):
        rb = slice(b * GLA_BLOCK, (b + 1) * GLA_BLOCK)
        for h in range(GLA_HEADS):
            scores = jnp.where(visible, _nt_dot(q_dec[rb, pair(h)], k_dec_h[h][rb, :]), 0.0).astype(BF16)
            o = jnp.dot(scores, v_ref[rb, vcols(h)], preferred_element_type=F32)
            o = o + jnp.concatenate(
                [_nt_dot(q_dec[crow(c), pair(h)], state_in[c][h])
                 for c in range(b * chunks_per_block, (b + 1) * chunks_per_block)], axis=0)
            g = go_ref[rb, vcols(h)].astype(F32)
            y_ref[rb, vcols(h)] = (_rms(o, gn) * (g * jax.nn.sigmoid(g))).astype(BF16)


def _gla(layer, proj, w_decay, b_decay, norm_gain, batch, seq):
    n = proj.shape[0]
    tps = seq // TC_GLA
    row = lambda b, t: b * tps + t
    return pl.pallas_call(
        _gla_kernel,
        out_shape=jax.ShapeDtypeStruct((n, GLA_V_DIM), BF16),
        grid=(batch, tps),
        in_specs=[
            pl.BlockSpec((TC_GLA, GLA_QK_DIM), lambda b, t: (row(b, t), C_GQ // GLA_QK_DIM)),
            pl.BlockSpec((TC_GLA, GLA_QK_DIM), lambda b, t: (row(b, t), C_GK // GLA_QK_DIM)),
            pl.BlockSpec((TC_GLA, GLA_V_DIM), lambda b, t: (row(b, t), C_GV // GLA_V_DIM)),
            pl.BlockSpec((TC_GLA, GLA_V_DIM), lambda b, t: (row(b, t), C_GO // GLA_V_DIM)),
            pl.BlockSpec((TC_GLA, Z_PAD), lambda b, t: (row(b, t), C_Z // Z_PAD)),
            _layer_spec(layer, (Z_PAD, GLA_QK_DIM)),
            _layer_spec(layer, (1, GLA_QK_DIM)),
            _layer_spec(layer, (1, GLA_DV)),
        ],
        out_specs=pl.BlockSpec((TC_GLA, GLA_V_DIM), lambda b, t: (row(b, t), 0)),
        scratch_shapes=[pltpu.VMEM((GLA_HEADS, GLA_DV, LANES), F32)],
        compiler_params=pltpu.CompilerParams(
            dimension_semantics=("parallel", "arbitrary"), vmem_limit_bytes=VMEM_LIMIT),
        name="gla",
    )(proj, proj, proj, proj, proj, w_decay, b_decay, norm_gain)


ONES_ROWS = BF16_SUBLANES
LOG2E = 1.4426950408889634


def _diff_kernel(lam_ref, q_ref, k_ref, v_ref, bias_ref, gn_ref, y_ref,
                 qs_ref, vt_ref, s0_ref, s1_ref, mx0_ref, mx1_ref, m_ref, acc_ref, *, lam_init):
    i = pl.program_id(1)
    t = T_ATT

    def rows(j):
        return pl.ds(pl.multiple_of(j * t, t), t)

    def head_cols(h):
        return slice(h * LANES, (h + 1) * LANES)

    def start():
        lane = lax.broadcasted_iota(jnp.int32, (1, LANES), 1)
        for h in range(DIFF_HEADS):
            q = q_ref[:, head_cols(h)].astype(F32) * (DIFF_DK ** -0.5 * LOG2E)
            qs_ref[h, 0:t, :] = jnp.where(lane < DIFF_DK, q, 0.0).astype(BF16)
            qs_ref[h, t:2 * t, :] = jnp.where(lane >= DIFF_DK, q, 0.0).astype(BF16)
        m_ref[...] = jnp.full_like(m_ref, -1e30)
        acc_ref[...] = jnp.zeros_like(acc_ref)

    @pl.when(i == 0)
    def _():
        vt_ref[:, :, DIFF_DV:, :] = jnp.ones((DIFF_HEADS, vt_ref.shape[1], ONES_ROWS, t), BF16)

        def transpose_block(jb, carry):
            for h in range(DIFF_HEADS):
                vt_ref[h, jb, 0:DIFF_DV, :] = v_ref[rows(jb), head_cols(h)].T
            return carry

        lax.fori_loop(0, vt_ref.shape[1], transpose_block, 0)

    def logits(j, buf):
        s_ref, mx_ref = buf
        for h in range(DIFF_HEADS):
            s = _nt_dot(k_ref[rows(j), head_cols(h)], qs_ref[h])
            s_ref[h] = s
            mx_ref[h] = jnp.max(s, axis=0, keepdims=True)

    def softmax_pv(j, buf, bias_cols):
        s_ref, mx_ref = buf
        for h in range(DIFF_HEADS):
            s = s_ref[h]
            if bias_cols is None:
                s_max = mx_ref[h]
            else:
                bias = bias_ref[h, :, bias_cols]
                s = s + jnp.concatenate([bias, bias], axis=1)
                s_max = jnp.max(s, axis=0, keepdims=True)
            m_old = m_ref[h]
            m_new = jnp.maximum(m_old, s_max)
            alpha = jnp.exp2(m_old - m_new)
            p = jnp.exp2(s - m_new).astype(BF16)
            m_ref[h] = m_new
            acc_ref[h] = alpha * acc_ref[h] + jnp.dot(vt_ref[h, j], p, preferred_element_type=F32)

    s0 = (s0_ref, mx0_ref)
    s1 = (s1_ref, mx1_ref)
    n_far = jnp.maximum(i - 1, 0)
    peeled = n_far % 2

    @pl.when(i == 0)
    def _():
        start()
        logits(0, s0)

    @pl.when(i % 2 == 1)
    def _():
        start()
        logits(0, s1)

    @pl.when(peeled == 1)
    def _():
        start()
        logits(0, s0)
        logits(1, s1)
        softmax_pv(0, s0, None)

    def far_pair(j):
        logits(j + 1, s0)
        softmax_pv(j, s1, None)
        logits(j + 2, s1)
        softmax_pv(j + 1, s0, None)

    n_pairs = n_far // 2
    peeled_pair = n_pairs % 2

    @pl.when(peeled_pair == 1)
    def _():
        far_pair(peeled)

    def far_quad(jj, carry):
        j = peeled + 2 * peeled_pair + 4 * jj
        far_pair(j)
        far_pair(j + 2)
        return carry

    lax.fori_loop(0, n_pairs // 2, far_quad, 0)

    def finish():
        lam_p = lam_ref[...]
        lam = (jnp.exp(jnp.sum(lam_p[0:1] * lam_p[1:2], axis=1, keepdims=True))
               - jnp.exp(jnp.sum(lam_p[2:3] * lam_p[3:4], axis=1, keepdims=True)) + lam_init)
        for h in range(DIFF_HEADS):
            acc = acc_ref[h, 0:DIFF_DV, :]
            inv_l = 1.0 / acc_ref[h, DIFF_DV:DIFF_DV + 1, :]
            o = acc[:, :t] * inv_l[:, :t] - lam * (acc[:, t:] * inv_l[:, t:])
            ms = jnp.mean(o * o, axis=0, keepdims=True)
            o = o * lax.rsqrt(ms + RMS_EPS) * gn_ref[...] * (1.0 - lam_init)
            y_ref[:, h * DIFF_DV:(h + 1) * DIFF_DV] = o.T.astype(BF16)

    @pl.when(i >= 1)
    def _():
        logits(i, s0)
        softmax_pv(i - 1, s1, slice(t, 2 * t))
        softmax_pv(i, s0, slice(0, t))
        finish()

    @pl.when(i == 0)
    def _():
        softmax_pv(i, s0, slice(0, t))
        finish()


def _diff(layer, proj, lam, bias, norm_gain_col, lam_init, batch, seq):
    n = proj.shape[0]
    nq = seq // T_ATT
    hw = DIFF_HEADS * LANES
    return pl.pallas_call(
        functools.partial(_diff_kernel, lam_init=lam_init),
        out_shape=jax.ShapeDtypeStruct((n, DIFF_HEADS * DIFF_DV), BF16),
        grid=(batch, nq),
        in_specs=[
            _layer_spec(layer, (4, DIFF_DK)),
            pl.BlockSpec((T_ATT, hw), lambda b, i: (b * nq + i, C_DQ // hw)),
            pl.BlockSpec((seq, hw), lambda b, i: (b, C_DK // hw)),
            pl.BlockSpec((seq, hw), lambda b, i: (b, C_DV // hw)),
            _const_spec((DIFF_HEADS, T_ATT, 2 * T_ATT)),
            _layer_spec(layer, (DIFF_DV, 1)),
        ],
        out_specs=pl.BlockSpec((T_ATT, DIFF_HEADS * DIFF_DV), lambda b, i: (b * nq + i, 0)),
        scratch_shapes=[
            pltpu.VMEM((DIFF_HEADS, 2 * T_ATT, LANES), BF16),
            pltpu.VMEM((DIFF_HEADS, nq, DIFF_DV + ONES_ROWS, T_ATT), BF16),
            pltpu.VMEM((DIFF_HEADS, T_ATT, 2 * T_ATT), F32),
            pltpu.VMEM((DIFF_HEADS, T_ATT, 2 * T_ATT), F32),
            pltpu.VMEM((DIFF_HEADS, 1, 2 * T_ATT), F32),
            pltpu.VMEM((DIFF_HEADS, 1, 2 * T_ATT), F32),
            pltpu.VMEM((DIFF_HEADS, 1, 2 * T_ATT), F32),
            pltpu.VMEM((DIFF_HEADS, DIFF_DV + ONES_ROWS, 2 * T_ATT), F32),
        ],
        compiler_params=pltpu.CompilerParams(
            dimension_semantics=("parallel", "arbitrary"), vmem_limit_bytes=VMEM_LIMIT),
        name="diff_attn",
    )(lam, proj, proj, proj, bias, norm_gain_col)


def _t5_bucket(rel):
    n = jnp.maximum(rel, 0)
    max_exact = REL_BUCKETS // 2
    nf = jnp.maximum(n, 1).astype(F32)
    large = max_exact + (jnp.log(nf / max_exact) / math.log(REL_MAX_DIST / max_exact)
                         * (REL_BUCKETS - max_exact)).astype(jnp.int32)
    large = jnp.minimum(large, REL_BUCKETS - 1)
    return jnp.where(n < max_exact, n, large)


def _bias_tiles(rel_bias):
    assert T_ATT + 1 >= REL_MAX_DIST
    t = T_ATT
    table = rel_bias.astype(F32)
    near = (table[_t5_bucket(jnp.arange(2 * t, dtype=jnp.int32))] - table[REL_BUCKETS - 1]) * LOG2E
    by_rel = jnp.concatenate([near, jnp.full((t, DIFF_HEADS), -jnp.inf, F32)], axis=0).T
    period = 3 * t
    flat = jnp.tile(by_rel, (1, t))[:, :t * (period - 1)]
    return flat.reshape(DIFF_HEADS, t, period - 1)[:, :, :2 * t]


def _merge_kernel(x_ref, g0_ref, g1_ref, g2_ref, u_ref, halo_ref, ygla_ref, ydiff_ref,
                  pw_ref, ps_ref, wb_ref, wo_ref, gpost_ref, o_ref, *, tiles_per_seq):
    i = pl.program_id(0)
    tm = TM_MERGE
    t_in_seq = i % tiles_per_seq
    halo = jnp.where(t_in_seq == 0, 0.0, halo_ref[...].astype(F32))
    ucat = jnp.concatenate([halo, u_ref[...].astype(F32)], axis=0)
    pos = t_in_seq * tm + lax.broadcasted_iota(jnp.int32, (tm, 1), 0)
    mixed = []
    for g, w in enumerate(POOL_WINDOWS):
        cols = slice(g * POOL_GROUP_DIM, (g + 1) * POOL_GROUP_DIM)
        win = ucat[:, cols]
        span = 1
        while span < w:
            win = win + pltpu.roll(win, span, axis=0)
            span *= 2
        cur = ucat[POOL_HALO:, cols]
        win = win[POOL_HALO:, :]
        cnt = jnp.minimum(pos + 1, w).astype(F32)
        pooled = win / cnt - cur
        mixed.append(jnp.dot(pooled.astype(BF16), pw_ref[g], preferred_element_type=F32))
    y_pool = (jnp.concatenate(mixed, axis=1) * ps_ref[...]).astype(BF16)

    merged = jax.nn.sigmoid(g0_ref[...].astype(F32)) * jnp.dot(y_pool, wb_ref[0], preferred_element_type=F32)
    merged = merged + jax.nn.sigmoid(g1_ref[...].astype(F32)) * jnp.dot(
        ygla_ref[...], wb_ref[1], preferred_element_type=F32)
    merged = merged + jax.nn.sigmoid(g2_ref[...].astype(F32)) * jnp.dot(
        ydiff_ref[...], wb_ref[2], preferred_element_type=F32)
    out = jnp.dot(merged.astype(BF16), wo_ref[...], preferred_element_type=F32)
    o_ref[...] = x_ref[...] + _rms(out, gpost_ref[...])


def _merge(layer, x2, proj, y_gla, y_diff, pool_w, pool_scale, w_branch, w_o, gpost, seq):
    n = x2.shape[0]
    tm = TM_MERGE
    halo_blocks = tm // POOL_HALO
    return pl.pallas_call(
        functools.partial(_merge_kernel, tiles_per_seq=seq // tm),
        out_shape=jax.ShapeDtypeStruct((n, D_MODEL), F32),
        grid=(n // tm,),
        in_specs=[
            pl.BlockSpec((tm, D_MODEL), lambda i: (i, 0)),
            pl.BlockSpec((tm, D_MODEL), lambda i: (i, C_GATE // D_MODEL + 0)),
            pl.BlockSpec((tm, D_MODEL), lambda i: (i, C_GATE // D_MODEL + 1)),
            pl.BlockSpec((tm, D_MODEL), lambda i: (i, C_GATE // D_MODEL + 2)),
            pl.BlockSpec((tm, POOL_DIM), lambda i: (i, C_U // POOL_DIM)),
            pl.BlockSpec((POOL_HALO, POOL_DIM),
                         lambda i: (jnp.maximum(i * halo_blocks - 1, 0), C_U // POOL_DIM)),
            pl.BlockSpec((tm, BRANCH_DIM), lambda i: (i, 0)),
            pl.BlockSpec((tm, BRANCH_DIM), lambda i: (i, 0)),
            _layer_spec(layer, (len(POOL_WINDOWS), POOL_GROUP_DIM, POOL_GROUP_DIM)),
            _layer_spec(layer, (1, POOL_DIM)),
            _layer_spec(layer, (3, BRANCH_DIM, D_MODEL)),
            _layer_spec(layer, (D_MODEL, D_MODEL)),
            _layer_spec(layer, (1, D_MODEL)),
        ],
        out_specs=pl.BlockSpec((tm, D_MODEL), lambda i: (i, 0)),
        compiler_params=pltpu.CompilerParams(
            dimension_semantics=("parallel",), vmem_limit_bytes=VMEM_LIMIT),
        name="merge",
    )(x2, proj, proj, proj, proj, proj, y_gla, y_diff, pool_w, pool_scale, w_branch, w_o, gpost)


def _ffn_kernel(x_ref, gpre_ref, wg_ref, wu_ref, wd_ref, gpost_ref, o_ref):
    x = x_ref[...]
    h = _rms(x, gpre_ref[...]).astype(BF16)
    g = jnp.dot(h, wg_ref[...], preferred_element_type=F32)
    u = jnp.dot(h, wu_ref[...], preferred_element_type=F32)
    a = (g * jax.nn.sigmoid(g) * u).astype(BF16)
    f = jnp.dot(a, wd_ref[...], preferred_element_type=F32)
    o_ref[...] = x + _rms(f, gpost_ref[...])


def _ffn(layer, x2, gpre, wg, wu, wd, gpost):
    n = x2.shape[0]
    tm = TM_FFN
    return pl.pallas_call(
        _ffn_kernel,
        out_shape=jax.ShapeDtypeStruct((n, D_MODEL), F32),
        grid=(n // tm,),
        in_specs=[
            pl.BlockSpec((tm, D_MODEL), lambda i: (i, 0)),
            _layer_spec(layer, (1, D_MODEL)),
            _layer_spec(layer, (D_MODEL, FFN_DIM)),
            _layer_spec(layer, (D_MODEL, FFN_DIM)),
            _layer_spec(layer, (FFN_DIM, D_MODEL)),
            _layer_spec(layer, (1, D_MODEL)),
        ],
        out_specs=pl.BlockSpec((tm, D_MODEL), lambda i: (i, 0)),
        compiler_params=pltpu.CompilerParams(
            dimension_semantics=("parallel",), vmem_limit_bytes=VMEM_LIMIT),
        name="ffn",
    )(x2, gpre, wg, wu, wd, gpost)


def _split_w_in(w):
    z0 = int(sum(IN_SPLITS[:5]))
    assert z0 == W_HEAD and w.shape[-1] - z0 - GLA_GATE_RANK == W_TAIL
    w = w.astype(BF16)
    w_z = jnp.pad(w[..., z0:z0 + GLA_GATE_RANK], ((0, 0), (0, 0), (0, Z_PAD - GLA_GATE_RANK)))
    return w[..., :z0], w_z, w[..., z0 + GLA_GATE_RANK:]


def kernel(x, rel_bias, ln_mix_pre, w_in, pool_w, pool_scale, gla_w_decay, gla_b_decay, gla_norm,
           diff_lambda, diff_norm, w_branch, w_o, ln_mix_post, ln_ffn_pre, ffn_w_gate, ffn_w_up,
           ffn_w_down, ln_ffn_post):
    batch, seq, d = x.shape
    depth = w_in.shape[0]
    assert POOL_HALO >= max(POOL_WINDOWS)
    assert d == D_MODEL and seq % TC_GLA == 0 and seq % T_ATT == 0 and seq % TM_MERGE == 0
    x2 = x.reshape(batch * seq, d)
    bias = _bias_tiles(rel_bias)
    row = lambda p: p[:, None, :]
    w_head, w_z, w_tail = _split_w_in(w_in)
    w_decay = jnp.pad(gla_w_decay, ((0, 0), (0, Z_PAD - GLA_GATE_RANK), (0, 0))).astype(BF16)
    pool_w, w_branch, w_o = pool_w.astype(BF16), w_branch.astype(BF16), w_o.astype(BF16)
    ffn_w_gate, ffn_w_up, ffn_w_down = ffn_w_gate.astype(BF16), ffn_w_up.astype(BF16), ffn_w_down.astype(BF16)
    for l in range(depth):
        lam_init = 0.8 - 0.6 * math.exp(-0.3 * l)
        proj = _in_proj(l, x2, row(ln_mix_pre), w_head, w_z, w_tail)
        y_gla = _gla(l, proj, w_decay, row(gla_b_decay), row(gla_norm), batch, seq)
        y_diff = _diff(l, proj, diff_lambda, bias, diff_norm[:, :, None], lam_init, batch, seq)
        x2 = _merge(l, x2, proj, y_gla, y_diff, pool_w, row(pool_scale), w_branch, w_o, row(ln_mix_post), seq)
        x2 = _ffn(l, x2, row(ln_ffn_pre), ffn_w_gate, ffn_w_up, ffn_w_down, row(ln_ffn_post))
    return x2.reshape(batch, seq, d)
```

```python
import functools
import math

import jax
import jax.numpy as jnp
from jax import lax
from jax.experimental import pallas as pl
from jax.experimental.pallas import tpu as pltpu

F32 = jnp.float32
BF16 = jnp.bfloat16

D_MODEL = 1024
POOL_WINDOWS = (2, 4, 8, 16)
POOL_GROUP_DIM = 128
POOL_DIM = 512
GLA_HEADS = 4
GLA_DK = 64
GLA_DV = 128
GLA_QK_DIM = 256
GLA_V_DIM = 512
GLA_GATE_RANK = 16
GLA_TAU = 16.0
GLA_CHUNK = 64
DIFF_HEADS = 4
DIFF_DK = 64
DIFF_DV = 128
REL_BUCKETS = 32
REL_MAX_DIST = 128
BRANCH_DIM = 512
FFN_DIM = 2816
RMS_EPS = 1e-6
IN_SPLITS = (512, 256, 256, 512, 512, 16, 512, 512, 512, 3072)

LANES = 128
BF16_SUBLANES = 16
MXU_TILE = 256
VMEM_LIMIT = 56 * 1024 * 1024

C_GATE = 0
C_U = 3072
C_GQ = 3584
C_GK = 3840
C_GV = 4096
C_GO = 4608
C_DQ = 5120
C_DK = 5632
C_DV = 6144
C_Z = 6656
Z_PAD = LANES
PROJ_DIM = C_Z + Z_PAD

TM_PROJ = 512
TM_MERGE = 512
TM_FFN = 512
TC_GLA = 2048
GLA_BLOCK = MXU_TILE
T_ATT = 256
POOL_HALO = BF16_SUBLANES


def _rms(x, gain):
    ms = jnp.mean(x * x, axis=-1, keepdims=True)
    return x * lax.rsqrt(ms + RMS_EPS) * gain


def _nt_dot(a, b):
    return lax.dot_general(a, b, (((1,), (1,)), ((), ())), preferred_element_type=F32)


def _tn_dot(a, b):
    return lax.dot_general(a, b, (((0,), (0,)), ((), ())), preferred_element_type=F32)


def _const_spec(shape):
    nd = len(shape)
    return pl.BlockSpec(shape, lambda *_: (0,) * nd, pipeline_mode=pl.Buffered(1))


def _layer_spec(layer, shape):
    nd = len(shape)
    return pl.BlockSpec((None,) + tuple(shape), lambda *_: (layer,) + (0,) * nd, pipeline_mode=pl.Buffered(1))


W_HEAD = C_DQ - C_U
W_TAIL = C_Z - C_DQ + C_U
N_DIFF = C_Z - C_DQ
PROJ_COL_STEP = 4 * MXU_TILE


def _in_proj_kernel(x_ref, g_ref, wh_ref, wz_ref, wt_ref, o_ref):
    h = _rms(x_ref[...], g_ref[...]).astype(BF16)
    step = PROJ_COL_STEP
    segments = ((wt_ref, N_DIFF, W_TAIL - N_DIFF, C_GATE), (wh_ref, 0, W_HEAD, C_U),
                (wt_ref, 0, N_DIFF, C_DQ), (wz_ref, 0, Z_PAD, C_Z))
    for w_ref, src, width, dst in segments:
        for c in range(0, width, step):
            w = min(step, width - c)
            o_ref[:, dst + c:dst + c + w] = jnp.dot(
                h, w_ref[:, src + c:src + c + w], preferred_element_type=F32).astype(BF16)


def _in_proj(layer, x2, gain, w_head, w_z, w_tail):
    n = x2.shape[0]
    return pl.pallas_call(
        _in_proj_kernel,
        out_shape=jax.ShapeDtypeStruct((n, PROJ_DIM), BF16),
        grid=(n // TM_PROJ,),
        in_specs=[
            pl.BlockSpec((TM_PROJ, D_MODEL), lambda i: (i, 0)),
            _layer_spec(layer, (1, D_MODEL)),
            _layer_spec(layer, (D_MODEL, W_HEAD)),
            _layer_spec(layer, (D_MODEL, Z_PAD)),
            _layer_spec(layer, (D_MODEL, W_TAIL)),
        ],
        out_specs=pl.BlockSpec((TM_PROJ, PROJ_DIM), lambda i: (i, 0)),
        compiler_params=pltpu.CompilerParams(
            dimension_semantics=("parallel",), vmem_limit_bytes=VMEM_LIMIT),
        name="in_proj",
    )(x2, gain, w_head, w_z, w_tail)


def _log_sigmoid(x):
    return jnp.minimum(x, 0.0) - jnp.log1p(jnp.exp(-jnp.abs(x)))


def _gla_kernel(q_ref, k_ref, v_ref, go_ref, z_ref, wd_ref, bd_ref, gn_ref, y_ref, st_ref):
    n_chunks = TC_GLA // GLA_CHUNK
    chunks_per_block = GLA_BLOCK // GLA_CHUNK

    @pl.when(pl.program_id(1) == 0)
    def _():
        st_ref[...] = jnp.zeros_like(st_ref)

    def pair(h):
        return slice((h // 2) * LANES, (h // 2 + 1) * LANES)

    def vcols(h):
        return slice(h * GLA_DV, (h + 1) * GLA_DV)

    def crow(c):
        return slice(c * GLA_CHUNK, (c + 1) * GLA_CHUNK)

    zl = jnp.dot(z_ref[...], wd_ref[...], preferred_element_type=F32) + bd_ref[...]
    la = _log_sigmoid(zl) / GLA_TAU

    hi = la.astype(BF16)
    r1 = la - hi.astype(F32)
    mid = r1.astype(BF16)
    lo = (r1 - mid.astype(F32)).astype(BF16)
    pieces = jnp.concatenate([hi, mid, lo], axis=1)
    row = lax.broadcasted_iota(jnp.int32, (GLA_CHUNK, GLA_CHUNK), 0)
    col = lax.broadcasted_iota(jnp.int32, (GLA_CHUNK, GLA_CHUNK), 1)
    tril_bf = jnp.where(row >= col, 1.0, 0.0).astype(BF16)
    cums, totals, decay = [], [], []
    for c in range(n_chunks):
        cs = jnp.dot(tril_bf, pieces[crow(c), :], preferred_element_type=F32)
        cum_c = cs[:, :GLA_QK_DIM] + cs[:, GLA_QK_DIM:2 * GLA_QK_DIM] + cs[:, 2 * GLA_QK_DIM:]
        last = cum_c[GLA_CHUNK - 1:GLA_CHUNK, :]
        cums.append(cum_c)
        totals.append(jnp.broadcast_to(last, (GLA_CHUNK, GLA_QK_DIM)))
        decay.append(jnp.exp(last))
    cum = jnp.concatenate(cums, axis=0)
    total = jnp.concatenate(totals, axis=0)

    q = q_ref[...].astype(F32)
    k = k_ref[...].astype(F32)
    q_dec = (q * (GLA_DK ** -0.5) * jnp.exp(cum)).astype(BF16)
    k_dec = k * jnp.exp(-cum)
    k_end = k * jnp.exp(total - cum)
    lane = lax.broadcasted_iota(jnp.int32, (1, LANES), 1)
    head_mask = (jnp.where(lane < GLA_DK, 1.0, 0.0), jnp.where(lane >= GLA_DK, 1.0, 0.0))
    k_dec_h = [(k_dec[:, pair(h)] * head_mask[h % 2]).astype(BF16) for h in range(GLA_HEADS)]
    k_end_h = [(k_end[:, pair(h)] * head_mask[h % 2]).astype(BF16) for h in range(GLA_HEADS)]

    state_in = [[None] * GLA_HEADS for _ in range(n_chunks)]
    for h in range(GLA_HEADS):
        st = st_ref[h]
        for c in range(n_chunks):
            state_in[c][h] = st.astype(BF16)
            st = st * decay[c][:, pair(h)] + _tn_dot(v_ref[crow(c), vcols(h)], k_end_h[h][crow(c), :])
        st_ref[h] = st

    row = lax.broadcasted_iota(jnp.int32, (GLA_BLOCK, GLA_BLOCK), 0)
    col = lax.broadcasted_iota(jnp.int32, (GLA_BLOCK, GLA_BLOCK), 1)
    visible = (row >= col) & (row // GLA_CHUNK == col // GLA_CHUNK)
    gn = gn_ref[...]
    for b in range(TC_GLA // GLA_BLOCK):
        rb = slice(b * GLA_BLOCK, (b + 1) * GLA_BLOCK)
        for h in range(GLA_HEADS):
            scores = jnp.where(visible, _nt_dot(q_dec[rb, pair(h)], k_dec_h[h][rb, :]), 0.0).astype(BF16)
            o = jnp.dot(scores, v_ref[rb, vcols(h)], preferred_element_type=F32)
            o = o + jnp.concatenate(
                [_nt_dot(q_dec[crow(c), pair(h)], state_in[c][h])
                 for c in range(b * chunks_per_block, (b + 1) * chunks_per_block)], axis=0)
            g = go_ref[rb, vcols(h)].astype(F32)
            y_ref[rb, vcols(h)] = (_rms(o, gn) * (g * jax.nn.sigmoid(g))).astype(BF16)


def _gla(layer, proj, w_decay, b_decay, norm_gain, batch, seq):
    n = proj.shape[0]
    tps = seq // TC_GLA
    row = lambda b, t: b * tps + t
    return pl.pallas_call(
        _gla_kernel,
        out_shape=jax.ShapeDtypeStruct((n, GLA_V_DIM), BF16),
        grid=(batch, tps),
        in_specs=[
            pl.BlockSpec((TC_GLA, GLA_QK_DIM), lambda b, t: (row(b, t), C_GQ // GLA_QK_DIM)),
            pl.BlockSpec((TC_GLA, GLA_QK_DIM), lambda b, t: (row(b, t), C_GK // GLA_QK_DIM)),
            pl.BlockSpec((TC_GLA, GLA_V_DIM), lambda b, t: (row(b, t), C_GV // GLA_V_DIM)),
            pl.BlockSpec((TC_GLA, GLA_V_DIM), lambda b, t: (row(b, t), C_GO // GLA_V_DIM)),
            pl.BlockSpec((TC_GLA, Z_PAD), lambda b, t: (row(b, t), C_Z // Z_PAD)),
            _layer_spec(layer, (Z_PAD, GLA_QK_DIM)),
            _layer_spec(layer, (1, GLA_QK_DIM)),
            _layer_spec(layer, (1, GLA_DV)),
        ],
        out_specs=pl.BlockSpec((TC_GLA, GLA_V_DIM), lambda b, t: (row(b, t), 0)),
        scratch_shapes=[pltpu.VMEM((GLA_HEADS, GLA_DV, LANES), F32)],
        compiler_params=pltpu.CompilerParams(
            dimension_semantics=("parallel", "arbitrary"), vmem_limit_bytes=VMEM_LIMIT),
        name="gla",
    )(proj, proj, proj, proj, proj, w_decay, b_decay, norm_gain)


ONES_ROWS = BF16_SUBLANES
LOG2E = 1.4426950408889634


def _diff_kernel(lam_ref, q_ref, k_ref, v_ref, bias_ref, gn_ref, y_ref,
                 qs_ref, vt_ref, s0_ref, s1_ref, mx0_ref, mx1_ref, m_ref, acc_ref, *, lam_init):
    i = pl.program_id(1)
    t = T_ATT

    def rows(j):
        return pl.ds(pl.multiple_of(j * t, t), t)

    def head_cols(h):
        return slice(h * LANES, (h + 1) * LANES)

    def start():
        lane = lax.broadcasted_iota(jnp.int32, (1, LANES), 1)
        for h in range(DIFF_HEADS):
            q = q_ref[:, head_cols(h)].astype(F32) * (DIFF_DK ** -0.5 * LOG2E)
            qs_ref[h, 0:t, :] = jnp.where(lane < DIFF_DK, q, 0.0).astype(BF16)
            qs_ref[h, t:2 * t, :] = jnp.where(lane >= DIFF_DK, q, 0.0).astype(BF16)
        m_ref[...] = jnp.full_like(m_ref, -1e30)
        acc_ref[...] = jnp.zeros_like(acc_ref)

    @pl.when(i == 0)
    def _():
        vt_ref[:, :, DIFF_DV:, :] = jnp.ones((DIFF_HEADS, vt_ref.shape[1], ONES_ROWS, t), BF16)

        def transpose_block(jb, carry):
            for h in range(DIFF_HEADS):
                vt_ref[h, jb, 0:DIFF_DV, :] = v_ref[rows(jb), head_cols(h)].T
            return carry

        lax.fori_loop(0, vt_ref.shape[1], transpose_block, 0)

    def logits(j, buf):
        s_ref, mx_ref = buf
        for h in range(DIFF_HEADS):
            s = _nt_dot(k_ref[rows(j), head_cols(h)], qs_ref[h])
            s_ref[h] = s
            mx_ref[h] = jnp.max(s, axis=0, keepdims=True)

    def softmax_pv(j, buf, bias_cols):
        s_ref, mx_ref = buf
        for h in range(DIFF_HEADS):
            s = s_ref[h]
            if bias_cols is None:
                s_max = mx_ref[h]
            else:
                bias = bias_ref[h, :, bias_cols]
                s = s + jnp.concatenate([bias, bias], axis=1)
                s_max = jnp.max(s, axis=0, keepdims=True)
            m_old = m_ref[h]
            m_new = jnp.maximum(m_old, s_max)
            alpha = jnp.exp2(m_old - m_new)
            p = jnp.exp2(s - m_new).astype(BF16)
            m_ref[h] = m_new
            acc_ref[h] = alpha * acc_ref[h] + jnp.dot(vt_ref[h, j], p, preferred_element_type=F32)

    s0 = (s0_ref, mx0_ref)
    s1 = (s1_ref, mx1_ref)
    n_far = jnp.maximum(i - 1, 0)
    peeled = n_far % 2

    @pl.when(i == 0)
    def _():
        start()
        logits(0, s0)

    @pl.when(i % 2 == 1)
    def _():
        start()
        logits(0, s1)

    @pl.when(peeled == 1)
    def _():
        start()
        logits(0, s0)
        logits(1, s1)
        softmax_pv(0, s0, None)

    def far_pair(j):
        logits(j + 1, s0)
        softmax_pv(j, s1, None)
        logits(j + 2, s1)
        softmax_pv(j + 1, s0, None)

    n_pairs = n_far // 2
    peeled_pair = n_pairs % 2

    @pl.when(peeled_pair == 1)
    def _():
        far_pair(peeled)

    n_quads = n_pairs // 2
    peeled_quad = n_quads % 2

    @pl.when(peeled_quad == 1)
    def _():
        j = peeled + 2 * peeled_pair
        far_pair(j)
        far_pair(j + 2)

    def far_oct(jj, carry):
        j = peeled + 2 * peeled_pair + 4 * peeled_quad + 8 * jj
        for d in range(0, 8, 2):
            far_pair(j + d)
        return carry

    lax.fori_loop(0, n_quads // 2, far_oct, 0)

    def finish():
        lam_p = lam_ref[...]
        lam = (jnp.exp(jnp.sum(lam_p[0:1] * lam_p[1:2], axis=1, keepdims=True))
               - jnp.exp(jnp.sum(lam_p[2:3] * lam_p[3:4], axis=1, keepdims=True)) + lam_init)
        for h in range(DIFF_HEADS):
            acc = acc_ref[h, 0:DIFF_DV, :]
            inv_l = 1.0 / acc_ref[h, DIFF_DV:DIFF_DV + 1, :]
            o = acc[:, :t] * inv_l[:, :t] - lam * (acc[:, t:] * inv_l[:, t:])
            ms = jnp.mean(o * o, axis=0, keepdims=True)
            o = o * lax.rsqrt(ms + RMS_EPS) * gn_ref[...] * (1.0 - lam_init)
            y_ref[:, h * DIFF_DV:(h + 1) * DIFF_DV] = o.T.astype(BF16)

    @pl.when(i >= 1)
    def _():
        logits(i, s0)
        softmax_pv(i - 1, s1, slice(t, 2 * t))
        softmax_pv(i, s0, slice(0, t))
        finish()

    @pl.when(i == 0)
    def _():
        softmax_pv(i, s0, slice(0, t))
        finish()


def _diff(layer, proj, lam, bias, norm_gain_col, lam_init, batch, seq):
    n = proj.shape[0]
    nq = seq // T_ATT
    hw = DIFF_HEADS * LANES
    return pl.pallas_call(
        functools.partial(_diff_kernel, lam_init=lam_init),
        out_shape=jax.ShapeDtypeStruct((n, DIFF_HEADS * DIFF_DV), BF16),
        grid=(batch, nq),
        in_specs=[
            _layer_spec(layer, (4, DIFF_DK)),
            pl.BlockSpec((T_ATT, hw), lambda b, i: (b * nq + i, C_DQ // hw)),
            pl.BlockSpec((seq, hw), lambda b, i: (b, C_DK // hw)),
            pl.BlockSpec((seq, hw), lambda b, i: (b, C_DV // hw)),
            _const_spec((DIFF_HEADS, T_ATT, 2 * T_ATT)),
            _layer_spec(layer, (DIFF_DV, 1)),
        ],
        out_specs=pl.BlockSpec((T_ATT, DIFF_HEADS * DIFF_DV), lambda b, i: (b * nq + i, 0)),
        scratch_shapes=[
            pltpu.VMEM((DIFF_HEADS, 2 * T_ATT, LANES), BF16),
            pltpu.VMEM((DIFF_HEADS, nq, DIFF_DV + ONES_ROWS, T_ATT), BF16),
            pltpu.VMEM((DIFF_HEADS, T_ATT, 2 * T_ATT), F32),
            pltpu.VMEM((DIFF_HEADS, T_ATT, 2 * T_ATT), F32),
            pltpu.VMEM((DIFF_HEADS, 1, 2 * T_ATT), F32),
            pltpu.VMEM((DIFF_HEADS, 1, 2 * T_ATT), F32),
            pltpu.VMEM((DIFF_HEADS, 1, 2 * T_ATT), F32),
            pltpu.VMEM((DIFF_HEADS, DIFF_DV + ONES_ROWS, 2 * T_ATT), F32),
        ],
        compiler_params=pltpu.CompilerParams(
            dimension_semantics=("parallel", "arbitrary"), vmem_limit_bytes=VMEM_LIMIT),
        name="diff_attn",
    )(lam, proj, proj, proj, bias, norm_gain_col)


def _t5_bucket(rel):
    n = jnp.maximum(rel, 0)
    max_exact = REL_BUCKETS // 2
    nf = jnp.maximum(n, 1).astype(F32)
    large = max_exact + (jnp.log(nf / max_exact) / math.log(REL_MAX_DIST / max_exact)
                         * (REL_BUCKETS - max_exact)).astype(jnp.int32)
    large = jnp.minimum(large, REL_BUCKETS - 1)
    return jnp.where(n < max_exact, n, large)


def _bias_tiles(rel_bias):
    assert T_ATT + 1 >= REL_MAX_DIST
    t = T_ATT
    table = rel_bias.astype(F32)
    near = (table[_t5_bucket(jnp.arange(2 * t, dtype=jnp.int32))] - table[REL_BUCKETS - 1]) * LOG2E
    by_rel = jnp.concatenate([near, jnp.full((t, DIFF_HEADS), -jnp.inf, F32)], axis=0).T
    period = 3 * t
    flat = jnp.tile(by_rel, (1, t))[:, :t * (period - 1)]
    return flat.reshape(DIFF_HEADS, t, period - 1)[:, :, :2 * t]


def _merge_kernel(x_ref, g0_ref, g1_ref, g2_ref, u_ref, halo_ref, ygla_ref, ydiff_ref,
                  pw_ref, ps_ref, wb_ref, wo_ref, gpost_ref, o_ref, *, tiles_per_seq):
    i = pl.program_id(0)
    tm = TM_MERGE
    t_in_seq = i % tiles_per_seq
    halo = jnp.where(t_in_seq == 0, 0.0, halo_ref[...].astype(F32))
    ucat = jnp.concatenate([halo, u_ref[...].astype(F32)], axis=0)
    pos = t_in_seq * tm + lax.broadcasted_iota(jnp.int32, (tm, 1), 0)
    mixed = []
    for g, w in enumerate(POOL_WINDOWS):
        cols = slice(g * POOL_GROUP_DIM, (g + 1) * POOL_GROUP_DIM)
        win = ucat[:, cols]
        span = 1
        while span < w:
            win = win + pltpu.roll(win, span, axis=0)
            span *= 2
        cur = ucat[POOL_HALO:, cols]
        win = win[POOL_HALO:, :]
        cnt = jnp.minimum(pos + 1, w).astype(F32)
        pooled = win / cnt - cur
        mixed.append(jnp.dot(pooled.astype(BF16), pw_ref[g], preferred_element_type=F32))
    y_pool = (jnp.concatenate(mixed, axis=1) * ps_ref[...]).astype(BF16)

    merged = jax.nn.sigmoid(g0_ref[...].astype(F32)) * jnp.dot(y_pool, wb_ref[0], preferred_element_type=F32)
    merged = merged + jax.nn.sigmoid(g1_ref[...].astype(F32)) * jnp.dot(
        ygla_ref[...], wb_ref[1], preferred_element_type=F32)
    merged = merged + jax.nn.sigmoid(g2_ref[...].astype(F32)) * jnp.dot(
        ydiff_ref[...], wb_ref[2], preferred_element_type=F32)
    out = jnp.dot(merged.astype(BF16), wo_ref[...], preferred_element_type=F32)
    o_ref[...] = x_ref[...] + _rms(out, gpost_ref[...])


def _merge(layer, x2, proj, y_gla, y_diff, pool_w, pool_scale, w_branch, w_o, gpost, seq):
    n = x2.shape[0]
    tm = TM_MERGE
    halo_blocks = tm // POOL_HALO
    return pl.pallas_call(
        functools.partial(_merge_kernel, tiles_per_seq=seq // tm),
        out_shape=jax.ShapeDtypeStruct((n, D_MODEL), F32),
        grid=(n // tm,),
        in_specs=[
            pl.BlockSpec((tm, D_MODEL), lambda i: (i, 0)),
            pl.BlockSpec((tm, D_MODEL), lambda i: (i, C_GATE // D_MODEL + 0)),
            pl.BlockSpec((tm, D_MODEL), lambda i: (i, C_GATE // D_MODEL + 1)),
            pl.BlockSpec((tm, D_MODEL), lambda i: (i, C_GATE // D_MODEL + 2)),
            pl.BlockSpec((tm, POOL_DIM), lambda i: (i, C_U // POOL_DIM)),
            pl.BlockSpec((POOL_HALO, POOL_DIM),
                         lambda i: (jnp.maximum(i * halo_blocks - 1, 0), C_U // POOL_DIM)),
            pl.BlockSpec((tm, BRANCH_DIM), lambda i: (i, 0)),
            pl.BlockSpec((tm, BRANCH_DIM), lambda i: (i, 0)),
            _layer_spec(layer, (len(POOL_WINDOWS), POOL_GROUP_DIM, POOL_GROUP_DIM)),
            _layer_spec(layer, (1, POOL_DIM)),
            _layer_spec(layer, (3, BRANCH_DIM, D_MODEL)),
            _layer_spec(layer, (D_MODEL, D_MODEL)),
            _layer_spec(layer, (1, D_MODEL)),
        ],
        out_specs=pl.BlockSpec((tm, D_MODEL), lambda i: (i, 0)),
        compiler_params=pltpu.CompilerParams(
            dimension_semantics=("parallel",), vmem_limit_bytes=VMEM_LIMIT),
        name="merge",
    )(x2, proj, proj, proj, proj, proj, y_gla, y_diff, pool_w, pool_scale, w_branch, w_o, gpost)


def _ffn_kernel(x_ref, gpre_ref, wg_ref, wu_ref, wd_ref, gpost_ref, o_ref):
    x = x_ref[...]
    h = _rms(x, gpre_ref[...]).astype(BF16)
    g = jnp.dot(h, wg_ref[...], preferred_element_type=F32)
    u = jnp.dot(h, wu_ref[...], preferred_element_type=F32)
    a = (g * jax.nn.sigmoid(g) * u).astype(BF16)
    f = jnp.dot(a, wd_ref[...], preferred_element_type=F32)
    o_ref[...] = x + _rms(f, gpost_ref[...])


def _ffn(layer, x2, gpre, wg, wu, wd, gpost):
    n = x2.shape[0]
    tm = TM_FFN
    return pl.pallas_call(
        _ffn_kernel,
        out_shape=jax.ShapeDtypeStruct((n, D_MODEL), F32),
        grid=(n // tm,),
        in_specs=[
            pl.BlockSpec((tm, D_MODEL), lambda i: (i, 0)),
            _layer_spec(layer, (1, D_MODEL)),
            _layer_spec(layer, (D_MODEL, FFN_DIM)),
            _layer_spec(layer, (D_MODEL, FFN_DIM)),
            _layer_spec(layer, (FFN_DIM, D_MODEL)),
            _layer_spec(layer, (1, D_MODEL)),
        ],
        out_specs=pl.BlockSpec((tm, D_MODEL), lambda i: (i, 0)),
        compiler_params=pltpu.CompilerParams(
            dimension_semantics=("parallel",), vmem_limit_bytes=VMEM_LIMIT),
        name="ffn",
    )(x2, gpre, wg, wu, wd, gpost)


def _split_w_in(w):
    z0 = int(sum(IN_SPLITS[:5]))
    assert z0 == W_HEAD and w.shape[-1] - z0 - GLA_GATE_RANK == W_TAIL
    w = w.astype(BF16)
    w_z = jnp.pad(w[..., z0:z0 + GLA_GATE_RANK], ((0, 0), (0, 0), (0, Z_PAD - GLA_GATE_RANK)))
    return w[..., :z0], w_z, w[..., z0 + GLA_GATE_RANK:]


def kernel(x, rel_bias, ln_mix_pre, w_in, pool_w, pool_scale, gla_w_decay, gla_b_decay, gla_norm,
           diff_lambda, diff_norm, w_branch, w_o, ln_mix_post, ln_ffn_pre, ffn_w_gate, ffn_w_up,
           ffn_w_down, ln_ffn_post):
    batch, seq, d = x.shape
    depth = w_in.shape[0]
    assert POOL_HALO >= max(POOL_WINDOWS)
    assert d == D_MODEL and seq % TC_GLA == 0 and seq % T_ATT == 0 and seq % TM_MERGE == 0
    x2 = x.reshape(batch * seq, d)
    bias = _bias_tiles(rel_bias)
    row = lambda p: p[:, None, :]
    w_head, w_z, w_tail = _split_w_in(w_in)
    w_decay = jnp.pad(gla_w_decay, ((0, 0), (0, Z_PAD - GLA_GATE_RANK), (0, 0))).astype(BF16)
    pool_w, w_branch, w_o = pool_w.astype(BF16), w_branch.astype(BF16), w_o.astype(BF16)
    ffn_w_gate, ffn_w_up, ffn_w_down = ffn_w_gate.astype(BF16), ffn_w_up.astype(BF16), ffn_w_down.astype(BF16)
    for l in range(depth):
        lam_init = 0.8 - 0.6 * math.exp(-0.3 * l)
        proj = _in_proj(l, x2, row(ln_mix_pre), w_head, w_z, w_tail)
        y_gla = _gla(l, proj, w_decay, row(gla_b_decay), row(gla_norm), batch, seq)
        y_diff = _diff(l, proj, diff_lambda, bias, diff_norm[:, :, None], lam_init, batch, seq)
        x2 = _merge(l, x2, proj, y_gla, y_diff, pool_w, row(pool_scale), w_branch, w_o, row(ln_mix_post), seq)
        x2 = _ffn(l, x2, row(ln_ffn_pre), ffn_w_gate, ffn_w_up, ffn_w_down, row(ln_ffn_post))
    return x2.reshape(batch, seq, d)
```

```python
import functools
import math

import jax
import jax.numpy as jnp
from jax import lax
from jax.experimental import pallas as pl
from jax.experimental.pallas import tpu as pltpu

F32 = jnp.float32
BF16 = jnp.bfloat16

D_MODEL = 1024
POOL_WINDOWS = (2, 4, 8, 16)
POOL_GROUP_DIM = 128
POOL_DIM = 512
GLA_HEADS = 4
GLA_DK = 64
GLA_DV = 128
GLA_QK_DIM = 256
GLA_V_DIM = 512
GLA_GATE_RANK = 16
GLA_TAU = 16.0
GLA_CHUNK = 64
DIFF_HEADS = 4
DIFF_DK = 64
DIFF_DV = 128
REL_BUCKETS = 32
REL_MAX_DIST = 128
BRANCH_DIM = 512
FFN_DIM = 2816
RMS_EPS = 1e-6
IN_SPLITS = (512, 256, 256, 512, 512, 16, 512, 512, 512, 3072)

LANES = 128
BF16_SUBLANES = 16
MXU_TILE = 256
VMEM_LIMIT = 56 * 1024 * 1024

C_GATE = 0
C_U = 3072
C_GQ = 3584
C_GK = 3840
C_GV = 4096
C_GO = 4608
C_DQ = 5120
C_DK = 5632
C_DV = 6144
C_Z = 6656
Z_PAD = LANES
PROJ_DIM = C_Z + Z_PAD

TM_PROJ = 512
TM_MERGE = 512
TM_FFN = 512
TC_GLA = 2048
GLA_BLOCK = MXU_TILE
T_ATT = 256
POOL_HALO = BF16_SUBLANES


def _rms(x, gain):
    ms = jnp.mean(x * x, axis=-1, keepdims=True)
    return x * lax.rsqrt(ms + RMS_EPS) * gain


def _nt_dot(a, b):
    return lax.dot_general(a, b, (((1,), (1,)), ((), ())), preferred_element_type=F32)


def _tn_dot(a, b):
    return lax.dot_general(a, b, (((0,), (0,)), ((), ())), preferred_element_type=F32)


def _const_spec(shape):
    nd = len(shape)
    return pl.BlockSpec(shape, lambda *_: (0,) * nd, pipeline_mode=pl.Buffered(1))


def _layer_spec(layer, shape):
    nd = len(shape)
    return pl.BlockSpec((None,) + tuple(shape), lambda *_: (layer,) + (0,) * nd, pipeline_mode=pl.Buffered(1))


W_HEAD = C_DQ - C_U
W_TAIL = C_Z - C_DQ + C_U
N_DIFF = C_Z - C_DQ
PROJ_COL_STEP = 4 * MXU_TILE


def _in_proj_kernel(x_ref, g_ref, wh_ref, wz_ref, wt_ref, o_ref):
    h = _rms(x_ref[...], g_ref[...]).astype(BF16)
    step = PROJ_COL_STEP
    segments = ((wt_ref, N_DIFF, W_TAIL - N_DIFF, C_GATE), (wh_ref, 0, W_HEAD, C_U),
                (wt_ref, 0, N_DIFF, C_DQ), (wz_ref, 0, Z_PAD, C_Z))
    for w_ref, src, width, dst in segments:
        for c in range(0, width, step):
            w = min(step, width - c)
            o_ref[:, dst + c:dst + c + w] = jnp.dot(
                h, w_ref[:, src + c:src + c + w], preferred_element_type=F32).astype(BF16)


def _in_proj(layer, x2, gain, w_head, w_z, w_tail):
    n = x2.shape[0]
    return pl.pallas_call(
        _in_proj_kernel,
        out_shape=jax.ShapeDtypeStruct((n, PROJ_DIM), BF16),
        grid=(n // TM_PROJ,),
        in_specs=[
            pl.BlockSpec((TM_PROJ, D_MODEL), lambda i: (i, 0)),
            _layer_spec(layer, (1, D_MODEL)),
            _layer_spec(layer, (D_MODEL, W_HEAD)),
            _layer_spec(layer, (D_MODEL, Z_PAD)),
            _layer_spec(layer, (D_MODEL, W_TAIL)),
        ],
        out_specs=pl.BlockSpec((TM_PROJ, PROJ_DIM), lambda i: (i, 0)),
        compiler_params=pltpu.CompilerParams(
            dimension_semantics=("parallel",), vmem_limit_bytes=VMEM_LIMIT),
        name="in_proj",
    )(x2, gain, w_head, w_z, w_tail)


def _log_sigmoid(x):
    return jnp.minimum(x, 0.0) - jnp.log1p(jnp.exp(-jnp.abs(x)))


def _gla_kernel(q_ref, k_ref, v_ref, go_ref, z_ref, wd_ref, bd_ref, gn_ref, y_ref, st_ref):
    n_chunks = TC_GLA // GLA_CHUNK
    chunks_per_block = GLA_BLOCK // GLA_CHUNK

    @pl.when(pl.program_id(1) == 0)
    def _():
        st_ref[...] = jnp.zeros_like(st_ref)

    def pair(h):
        return slice((h // 2) * LANES, (h // 2 + 1) * LANES)

    def vcols(h):
        return slice(h * GLA_DV, (h + 1) * GLA_DV)

    def crow(c):
        return slice(c * GLA_CHUNK, (c + 1) * GLA_CHUNK)

    zl = jnp.dot(z_ref[...], wd_ref[...], preferred_element_type=F32) + bd_ref[...]
    la = _log_sigmoid(zl) / GLA_TAU

    hi = la.astype(BF16)
    r1 = la - hi.astype(F32)
    mid = r1.astype(BF16)
    lo = (r1 - mid.astype(F32)).astype(BF16)
    pieces = jnp.concatenate([hi, mid, lo], axis=1)
    row = lax.broadcasted_iota(jnp.int32, (GLA_CHUNK, GLA_CHUNK), 0)
    col = lax.broadcasted_iota(jnp.int32, (GLA_CHUNK, GLA_CHUNK), 1)
    tril_bf = jnp.where(row >= col, 1.0, 0.0).astype(BF16)
    cums, totals, decay = [], [], []
    for c in range(n_chunks):
        cs = jnp.dot(tril_bf, pieces[crow(c), :], preferred_element_type=F32)
        cum_c = cs[:, :GLA_QK_DIM] + cs[:, GLA_QK_DIM:2 * GLA_QK_DIM] + cs[:, 2 * GLA_QK_DIM:]
        last = cum_c[GLA_CHUNK - 1:GLA_CHUNK, :]
        cums.append(cum_c)
        totals.append(jnp.broadcast_to(last, (GLA_CHUNK, GLA_QK_DIM)))
        decay.append(jnp.exp(last))
    cum = jnp.concatenate(cums, axis=0)
    total = jnp.concatenate(totals, axis=0)

    q = q_ref[...].astype(F32)
    k = k_ref[...].astype(F32)
    q_dec = (q * (GLA_DK ** -0.5) * jnp.exp(cum)).astype(BF16)
    k_dec = k * jnp.exp(-cum)
    k_end = k * jnp.exp(total - cum)
    lane = lax.broadcasted_iota(jnp.int32, (1, LANES), 1)
    head_mask = (jnp.where(lane < GLA_DK, 1.0, 0.0), jnp.where(lane >= GLA_DK, 1.0, 0.0))
    k_dec_h = [(k_dec[:, pair(h)] * head_mask[h % 2]).astype(BF16) for h in range(GLA_HEADS)]
    k_end_h = [(k_end[:, pair(h)] * head_mask[h % 2]).astype(BF16) for h in range(GLA_HEADS)]

    state_in = [[None] * GLA_HEADS for _ in range(n_chunks)]
    for h in range(GLA_HEADS):
        st = st_ref[h]
        for c in range(n_chunks):
            state_in[c][h] = st.astype(BF16)
            st = st * decay[c][:, pair(h)] + _tn_dot(v_ref[crow(c), vcols(h)], k_end_h[h][crow(c), :])
        st_ref[h] = st

    row = lax.broadcasted_iota(jnp.int32, (GLA_BLOCK, GLA_BLOCK), 0)
    col = lax.broadcasted_iota(jnp.int32, (GLA_BLOCK, GLA_BLOCK), 1)
    visible = (row >= col) & (row // GLA_CHUNK == col // GLA_CHUNK)
    gn = gn_ref[...]
    for b in range(TC_GLA // GLA_BLOCK):
        rb = slice(b * GLA_BLOCK, (b + 1) * GLA_BLOCK)
        for h in range(GLA_HEADS):
            scores = jnp.where(visible, _nt_dot(q_dec[rb, pair(h)], k_dec_h[h][rb, :]), 0.0).astype(BF16)
            o = jnp.dot(scores, v_ref[rb, vcols(h)], preferred_element_type=F32)
            o = o + jnp.concatenate(
                [_nt_dot(q_dec[crow(c), pair(h)], state_in[c][h])
                 for c in range(b * chunks_per_block, (b + 1) * chunks_per_block)], axis=0)
            g = go_ref[rb, vcols(h)].astype(F32)
            y_ref[rb, vcols(h)] = (_rms(o, gn) * (g * jax.nn.sigmoid(g))).astype(BF16)


def _gla(layer, proj, w_decay, b_decay, norm_gain, batch, seq):
    n = proj.shape[0]
    tps = seq // TC_GLA
    row = lambda b, t: b * tps + t
    return pl.pallas_call(
        _gla_kernel,
        out_shape=jax.ShapeDtypeStruct((n, GLA_V_DIM), BF16),
        grid=(batch, tps),
        in_specs=[
            pl.BlockSpec((TC_GLA, GLA_QK_DIM), lambda b, t: (row(b, t), C_GQ // GLA_QK_DIM)),
            pl.BlockSpec((TC_GLA, GLA_QK_DIM), lambda b, t: (row(b, t), C_GK // GLA_QK_DIM)),
            pl.BlockSpec((TC_GLA, GLA_V_DIM), lambda b, t: (row(b, t), C_GV // GLA_V_DIM)),
            pl.BlockSpec((TC_GLA, GLA_V_DIM), lambda b, t: (row(b, t), C_GO // GLA_V_DIM)),
            pl.BlockSpec((TC_GLA, Z_PAD), lambda b, t: (row(b, t), C_Z // Z_PAD)),
            _layer_spec(layer, (Z_PAD, GLA_QK_DIM)),
            _layer_spec(layer, (1, GLA_QK_DIM)),
            _layer_spec(layer, (1, GLA_DV)),
        ],
        out_specs=pl.BlockSpec((TC_GLA, GLA_V_DIM), lambda b, t: (row(b, t), 0)),
        scratch_shapes=[pltpu.VMEM((GLA_HEADS, GLA_DV, LANES), F32)],
        compiler_params=pltpu.CompilerParams(
            dimension_semantics=("parallel", "arbitrary"), vmem_limit_bytes=VMEM_LIMIT),
        name="gla",
    )(proj, proj, proj, proj, proj, w_decay, b_decay, norm_gain)


ONES_ROWS = BF16_SUBLANES
LOG2E = 1.4426950408889634


def _diff_kernel(lam_ref, q_ref, k_ref, v_ref, bias_ref, gn_ref, y_ref,
                 qs_ref, vt_ref, s0_ref, s1_ref, mx0_ref, mx1_ref, m_ref, acc_ref, *, lam_init):
    i = pl.program_id(1)
    t = T_ATT

    def rows(j):
        return pl.ds(pl.multiple_of(j * t, t), t)

    def head_cols(h):
        return slice(h * LANES, (h + 1) * LANES)

    def start():
        lane = lax.broadcasted_iota(jnp.int32, (1, LANES), 1)
        for h in range(DIFF_HEADS):
            q = q_ref[:, head_cols(h)].astype(F32) * (DIFF_DK ** -0.5 * LOG2E)
            qs_ref[h, 0:t, :] = jnp.where(lane < DIFF_DK, q, 0.0).astype(BF16)
            qs_ref[h, t:2 * t, :] = jnp.where(lane >= DIFF_DK, q, 0.0).astype(BF16)
        m_ref[...] = jnp.full_like(m_ref, -1e30)
        acc_ref[...] = jnp.zeros_like(acc_ref)

    @pl.when(i == 0)
    def _():
        vt_ref[:, :, DIFF_DV:, :] = jnp.ones((DIFF_HEADS, vt_ref.shape[1], ONES_ROWS, t), BF16)

        def transpose_block(jb, carry):
            for h in range(DIFF_HEADS):
                vt_ref[h, jb, 0:DIFF_DV, :] = v_ref[rows(jb), head_cols(h)].T
            return carry

        lax.fori_loop(0, vt_ref.shape[1], transpose_block, 0)

    def logits(j, buf):
        s_ref, mx_ref = buf
        for h in range(DIFF_HEADS):
            s = _nt_dot(k_ref[rows(j), head_cols(h)], qs_ref[h])
            s_ref[h] = s
            mx_ref[h] = jnp.max(s, axis=0, keepdims=True)

    def softmax_pv(j, buf, bias_cols):
        s_ref, mx_ref = buf
        for h in range(DIFF_HEADS):
            s = s_ref[h]
            if bias_cols is None:
                s_max = mx_ref[h]
            else:
                bias = bias_ref[h, :, bias_cols]
                s = s + jnp.concatenate([bias, bias], axis=1)
                s_max = jnp.max(s, axis=0, keepdims=True)
            m_old = m_ref[h]
            m_new = jnp.maximum(m_old, s_max)
            alpha = jnp.exp2(m_old - m_new)
            p = jnp.exp2(s - m_new).astype(BF16)
            m_ref[h] = m_new
            acc_ref[h] = alpha * acc_ref[h] + jnp.dot(vt_ref[h, j], p, preferred_element_type=F32)

    s0 = (s0_ref, mx0_ref)
    s1 = (s1_ref, mx1_ref)
    n_far = jnp.maximum(i - 1, 0)
    peeled = n_far % 2

    def far_pair(j):
        logits(j + 1, s0)
        softmax_pv(j, s1, None)
        logits(j + 2, s1)
        softmax_pv(j + 1, s0, None)

    n_pairs = n_far // 2
    peeled_pair = n_pairs % 2

    @pl.when(i == 0)
    def _():
        start()
        logits(0, s0)

    def fill_odd(with_pair):
        start()
        logits(0, s1)
        if with_pair:
            far_pair(0)

    def fill_even(with_pair):
        start()
        logits(0, s0)
        logits(1, s1)
        softmax_pv(0, s0, None)
        if with_pair:
            far_pair(1)

    for with_pair in (False, True):
        pl.when((i % 2 == 1) & (peeled_pair == int(with_pair)))(functools.partial(fill_odd, with_pair))
        pl.when((peeled == 1) & (peeled_pair == int(with_pair)))(functools.partial(fill_even, with_pair))

    n_quads = n_pairs // 2
    peeled_quad = n_quads % 2

    @pl.when(peeled_quad == 1)
    def _():
        j = peeled + 2 * peeled_pair
        far_pair(j)
        far_pair(j + 2)

    def far_oct(jj, carry):
        j = peeled + 2 * peeled_pair + 4 * peeled_quad + 8 * jj
        for d in range(0, 8, 2):
            far_pair(j + d)
        return carry

    lax.fori_loop(0, n_quads // 2, far_oct, 0)

    def finish():
        lam_p = lam_ref[...]
        lam = (jnp.exp(jnp.sum(lam_p[0:1] * lam_p[1:2], axis=1, keepdims=True))
               - jnp.exp(jnp.sum(lam_p[2:3] * lam_p[3:4], axis=1, keepdims=True)) + lam_init)
        for h in range(DIFF_HEADS):
            acc = acc_ref[h, 0:DIFF_DV, :]
            inv_l = 1.0 / acc_ref[h, DIFF_DV:DIFF_DV + 1, :]
            o = acc[:, :t] * inv_l[:, :t] - lam * (acc[:, t:] * inv_l[:, t:])
            ms = jnp.mean(o * o, axis=0, keepdims=True)
            o = o * lax.rsqrt(ms + RMS_EPS) * gn_ref[...] * (1.0 - lam_init)
            y_ref[:, h * DIFF_DV:(h + 1) * DIFF_DV] = o.T.astype(BF16)

    @pl.when(i >= 1)
    def _():
        logits(i, s0)
        softmax_pv(i - 1, s1, slice(t, 2 * t))
        softmax_pv(i, s0, slice(0, t))
        finish()

    @pl.when(i == 0)
    def _():
        softmax_pv(i, s0, slice(0, t))
        finish()


def _diff(layer, proj, lam, bias, norm_gain_col, lam_init, batch, seq):
    n = proj.shape[0]
    nq = seq // T_ATT
    hw = DIFF_HEADS * LANES
    return pl.pallas_call(
        functools.partial(_diff_kernel, lam_init=lam_init),
        out_shape=jax.ShapeDtypeStruct((n, DIFF_HEADS * DIFF_DV), BF16),
        grid=(batch, nq),
        in_specs=[
            _layer_spec(layer, (4, DIFF_DK)),
            pl.BlockSpec((T_ATT, hw), lambda b, i: (b * nq + i, C_DQ // hw)),
            pl.BlockSpec((seq, hw), lambda b, i: (b, C_DK // hw)),
            pl.BlockSpec((seq, hw), lambda b, i: (b, C_DV // hw)),
            _const_spec((DIFF_HEADS, T_ATT, 2 * T_ATT)),
            _layer_spec(layer, (DIFF_DV, 1)),
        ],
        out_specs=pl.BlockSpec((T_ATT, DIFF_HEADS * DIFF_DV), lambda b, i: (b * nq + i, 0)),
        scratch_shapes=[
            pltpu.VMEM((DIFF_HEADS, 2 * T_ATT, LANES), BF16),
            pltpu.VMEM((DIFF_HEADS, nq, DIFF_DV + ONES_ROWS, T_ATT), BF16),
            pltpu.VMEM((DIFF_HEADS, T_ATT, 2 * T_ATT), F32),
            pltpu.VMEM((DIFF_HEADS, T_ATT, 2 * T_ATT), F32),
            pltpu.VMEM((DIFF_HEADS, 1, 2 * T_ATT), F32),
            pltpu.VMEM((DIFF_HEADS, 1, 2 * T_ATT), F32),
            pltpu.VMEM((DIFF_HEADS, 1, 2 * T_ATT), F32),
            pltpu.VMEM((DIFF_HEADS, DIFF_DV + ONES_ROWS, 2 * T_ATT), F32),
        ],
        compiler_params=pltpu.CompilerParams(
            dimension_semantics=("parallel", "arbitrary"), vmem_limit_bytes=VMEM_LIMIT),
        name="diff_attn",
    )(lam, proj, proj, proj, bias, norm_gain_col)


def _t5_bucket(rel):
    n = jnp.maximum(rel, 0)
    max_exact = REL_BUCKETS // 2
    nf = jnp.maximum(n, 1).astype(F32)
    large = max_exact + (jnp.log(nf / max_exact) / math.log(REL_MAX_DIST / max_exact)
                         * (REL_BUCKETS - max_exact)).astype(jnp.int32)
    large = jnp.minimum(large, REL_BUCKETS - 1)
    return jnp.where(n < max_exact, n, large)


def _bias_tiles(rel_bias):
    assert T_ATT + 1 >= REL_MAX_DIST
    t = T_ATT
    table = rel_bias.astype(F32)
    near = (table[_t5_bucket(jnp.arange(2 * t, dtype=jnp.int32))] - table[REL_BUCKETS - 1]) * LOG2E
    by_rel = jnp.concatenate([near, jnp.full((t, DIFF_HEADS), -jnp.inf, F32)], axis=0).T
    period = 3 * t
    flat = jnp.tile(by_rel, (1, t))[:, :t * (period - 1)]
    return flat.reshape(DIFF_HEADS, t, period - 1)[:, :, :2 * t]


def _merge_kernel(x_ref, g0_ref, g1_ref, g2_ref, u_ref, halo_ref, ygla_ref, ydiff_ref,
                  pw_ref, ps_ref, wb_ref, wo_ref, gpost_ref, o_ref, *, tiles_per_seq):
    i = pl.program_id(0)
    tm = TM_MERGE
    t_in_seq = i % tiles_per_seq
    halo = jnp.where(t_in_seq == 0, 0.0, halo_ref[...].astype(F32))
    ucat = jnp.concatenate([halo, u_ref[...].astype(F32)], axis=0)
    pos = t_in_seq * tm + lax.broadcasted_iota(jnp.int32, (tm, 1), 0)
    mixed = []
    for g, w in enumerate(POOL_WINDOWS):
        cols = slice(g * POOL_GROUP_DIM, (g + 1) * POOL_GROUP_DIM)
        win = ucat[:, cols]
        span = 1
        while span < w:
            win = win + pltpu.roll(win, span, axis=0)
            span *= 2
        cur = ucat[POOL_HALO:, cols]
        win = win[POOL_HALO:, :]
        cnt = jnp.minimum(pos + 1, w).astype(F32)
        pooled = win / cnt - cur
        mixed.append(jnp.dot(pooled.astype(BF16), pw_ref[g], preferred_element_type=F32))
    y_pool = (jnp.concatenate(mixed, axis=1) * ps_ref[...]).astype(BF16)

    merged = jax.nn.sigmoid(g0_ref[...].astype(F32)) * jnp.dot(y_pool, wb_ref[0], preferred_element_type=F32)
    merged = merged + jax.nn.sigmoid(g1_ref[...].astype(F32)) * jnp.dot(
        ygla_ref[...], wb_ref[1], preferred_element_type=F32)
    merged = merged + jax.nn.sigmoid(g2_ref[...].astype(F32)) * jnp.dot(
        ydiff_ref[...], wb_ref[2], preferred_element_type=F32)
    out = jnp.dot(merged.astype(BF16), wo_ref[...], preferred_element_type=F32)
    o_ref[...] = x_ref[...] + _rms(out, gpost_ref[...])


def _merge(layer, x2, proj, y_gla, y_diff, pool_w, pool_scale, w_branch, w_o, gpost, seq):
    n = x2.shape[0]
    tm = TM_MERGE
    halo_blocks = tm // POOL_HALO
    return pl.pallas_call(
        functools.partial(_merge_kernel, tiles_per_seq=seq // tm),
        out_shape=jax.ShapeDtypeStruct((n, D_MODEL), F32),
        grid=(n // tm,),
        in_specs=[
            pl.BlockSpec((tm, D_MODEL), lambda i: (i, 0)),
            pl.BlockSpec((tm, D_MODEL), lambda i: (i, C_GATE // D_MODEL + 0)),
            pl.BlockSpec((tm, D_MODEL), lambda i: (i, C_GATE // D_MODEL + 1)),
            pl.BlockSpec((tm, D_MODEL), lambda i: (i, C_GATE // D_MODEL + 2)),
            pl.BlockSpec((tm, POOL_DIM), lambda i: (i, C_U // POOL_DIM)),
            pl.BlockSpec((POOL_HALO, POOL_DIM),
                         lambda i: (jnp.maximum(i * halo_blocks - 1, 0), C_U // POOL_DIM)),
            pl.BlockSpec((tm, BRANCH_DIM), lambda i: (i, 0)),
            pl.BlockSpec((tm, BRANCH_DIM), lambda i: (i, 0)),
            _layer_spec(layer, (len(POOL_WINDOWS), POOL_GROUP_DIM, POOL_GROUP_DIM)),
            _layer_spec(layer, (1, POOL_DIM)),
            _layer_spec(layer, (3, BRANCH_DIM, D_MODEL)),
            _layer_spec(layer, (D_MODEL, D_MODEL)),
            _layer_spec(layer, (1, D_MODEL)),
        ],
        out_specs=pl.BlockSpec((tm, D_MODEL), lambda i: (i, 0)),
        compiler_params=pltpu.CompilerParams(
            dimension_semantics=("parallel",), vmem_limit_bytes=VMEM_LIMIT),
        name="merge",
    )(x2, proj, proj, proj, proj, proj, y_gla, y_diff, pool_w, pool_scale, w_branch, w_o, gpost)


def _ffn_kernel(x_ref, gpre_ref, wg_ref, wu_ref, wd_ref, gpost_ref, o_ref):
    x = x_ref[...]
    h = _rms(x, gpre_ref[...]).astype(BF16)
    g = jnp.dot(h, wg_ref[...], preferred_element_type=F32)
    u = jnp.dot(h, wu_ref[...], preferred_element_type=F32)
    a = (g * jax.nn.sigmoid(g) * u).astype(BF16)
    f = jnp.dot(a, wd_ref[...], preferred_element_type=F32)
    o_ref[...] = x + _rms(f, gpost_ref[...])


def _ffn(layer, x2, gpre, wg, wu, wd, gpost):
    n = x2.shape[0]
    tm = TM_FFN
    return pl.pallas_call(
        _ffn_kernel,
        out_shape=jax.ShapeDtypeStruct((n, D_MODEL), F32),
        grid=(n // tm,),
        in_specs=[
            pl.BlockSpec((tm, D_MODEL), lambda i: (i, 0)),
            _layer_spec(layer, (1, D_MODEL)),
            _layer_spec(layer, (D_MODEL, FFN_DIM)),
            _layer_spec(layer, (D_MODEL, FFN_DIM)),
            _layer_spec(layer, (FFN_DIM, D_MODEL)),
            _layer_spec(layer, (1, D_MODEL)),
        ],
        out_specs=pl.BlockSpec((tm, D_MODEL), lambda i: (i, 0)),
        compiler_params=pltpu.CompilerParams(
            dimension_semantics=("parallel",), vmem_limit_bytes=VMEM_LIMIT),
        name="ffn",
    )(x2, gpre, wg, wu, wd, gpost)


def _split_w_in(w):
    z0 = int(sum(IN_SPLITS[:5]))
    assert z0 == W_HEAD and w.shape[-1] - z0 - GLA_GATE_RANK == W_TAIL
    w = w.astype(BF16)
    w_z = jnp.pad(w[..., z0:z0 + GLA_GATE_RANK], ((0, 0), (0, 0), (0, Z_PAD - GLA_GATE_RANK)))
    return w[..., :z0], w_z, w[..., z0 + GLA_GATE_RANK:]


def kernel(x, rel_bias, ln_mix_pre, w_in, pool_w, pool_scale, gla_w_decay, gla_b_decay, gla_norm,
           diff_lambda, diff_norm, w_branch, w_o, ln_mix_post, ln_ffn_pre, ffn_w_gate, ffn_w_up,
           ffn_w_down, ln_ffn_post):
    batch, seq, d = x.shape
    depth = w_in.shape[0]
    assert POOL_HALO >= max(POOL_WINDOWS)
    assert d == D_MODEL and seq % TC_GLA == 0 and seq % T_ATT == 0 and seq % TM_MERGE == 0
    x2 = x.reshape(batch * seq, d)
    bias = _bias_tiles(rel_bias)
    row = lambda p: p[:, None, :]
    w_head, w_z, w_tail = _split_w_in(w_in)
    w_decay = jnp.pad(gla_w_decay, ((0, 0), (0, Z_PAD - GLA_GATE_RANK), (0, 0))).astype(BF16)
    pool_w, w_branch, w_o = pool_w.astype(BF16), w_branch.astype(BF16), w_o.astype(BF16)
    ffn_w_gate, ffn_w_up, ffn_w_down = ffn_w_gate.astype(BF16), ffn_w_up.astype(BF16), ffn_w_down.astype(BF16)
    for l in range(depth):
        lam_init = 0.8 - 0.6 * math.exp(-0.3 * l)
        proj = _in_proj(l, x2, row(ln_mix_pre), w_head, w_z, w_tail)
        y_gla = _gla(l, proj, w_decay, row(gla_b_decay), row(gla_norm), batch, seq)
        y_diff = _diff(l, proj, diff_lambda, bias, diff_norm[:, :, None], lam_init, batch, seq)
        x2 = _merge(l, x2, proj, y_gla, y_diff, pool_w, row(pool_scale), w_branch, w_o, row(ln_mix_post), seq)
        x2 = _ffn(l, x2, row(ln_ffn_pre), ffn_w_gate, ffn_w_up, ffn_w_down, row(ln_ffn_post))
    return x2.reshape(batch, seq, d)
```

```python
import functools
import math

import jax
import jax.numpy as jnp
from jax import lax
from jax.experimental import pallas as pl
from jax.experimental.pallas import tpu as pltpu

F32 = jnp.float32
BF16 = jnp.bfloat16

D_MODEL = 1024
POOL_WINDOWS = (2, 4, 8, 16)
POOL_GROUP_DIM = 128
POOL_DIM = 512
GLA_HEADS = 4
GLA_DK = 64
GLA_DV = 128
GLA_QK_DIM = 256
GLA_V_DIM = 512
GLA_GATE_RANK = 16
GLA_TAU = 16.0
GLA_CHUNK = 64
DIFF_HEADS = 4
DIFF_DK = 64
DIFF_DV = 128
REL_BUCKETS = 32
REL_MAX_DIST = 128
BRANCH_DIM = 512
FFN_DIM = 2816
RMS_EPS = 1e-6
IN_SPLITS = (512, 256, 256, 512, 512, 16, 512, 512, 512, 3072)

LANES = 128
BF16_SUBLANES = 16
MXU_TILE = 256
VMEM_LIMIT = 56 * 1024 * 1024

C_GATE = 0
C_U = 3072
C_GQ = 3584
C_GK = 3840
C_GV = 4096
C_GO = 4608
C_DQ = 5120
C_DK = 5632
C_DV = 6144
C_Z = 6656
Z_PAD = LANES
PROJ_DIM = C_Z + Z_PAD

TM_PROJ = 512
TM_MERGE = 512
TM_FFN = 512
TC_GLA = 2048
GLA_BLOCK = MXU_TILE
T_ATT = 256
POOL_HALO = BF16_SUBLANES


def _rms(x, gain):
    ms = jnp.mean(x * x, axis=-1, keepdims=True)
    return x * lax.rsqrt(ms + RMS_EPS) * gain


def _nt_dot(a, b):
    return lax.dot_general(a, b, (((1,), (1,)), ((), ())), preferred_element_type=F32)


def _tn_dot(a, b):
    return lax.dot_general(a, b, (((0,), (0,)), ((), ())), preferred_element_type=F32)


def _const_spec(shape):
    nd = len(shape)
    return pl.BlockSpec(shape, lambda *_: (0,) * nd, pipeline_mode=pl.Buffered(1))


def _layer_spec(layer, shape):
    nd = len(shape)
    return pl.BlockSpec((None,) + tuple(shape), lambda *_: (layer,) + (0,) * nd, pipeline_mode=pl.Buffered(1))


W_HEAD = C_DQ - C_U
W_TAIL = C_Z - C_DQ + C_U
N_DIFF = C_Z - C_DQ
PROJ_COL_STEP = 4 * MXU_TILE


def _in_proj_kernel(x_ref, g_ref, wh_ref, wz_ref, wt_ref, o_ref):
    h = _rms(x_ref[...], g_ref[...]).astype(BF16)
    step = PROJ_COL_STEP
    segments = ((wt_ref, N_DIFF, W_TAIL - N_DIFF, C_GATE), (wh_ref, 0, W_HEAD, C_U),
                (wt_ref, 0, N_DIFF, C_DQ), (wz_ref, 0, Z_PAD, C_Z))
    for w_ref, src, width, dst in segments:
        for c in range(0, width, step):
            w = min(step, width - c)
            o_ref[:, dst + c:dst + c + w] = jnp.dot(
                h, w_ref[:, src + c:src + c + w], preferred_element_type=F32).astype(BF16)


def _in_proj(layer, x2, gain, w_head, w_z, w_tail):
    n = x2.shape[0]
    return pl.pallas_call(
        _in_proj_kernel,
        out_shape=jax.ShapeDtypeStruct((n, PROJ_DIM), BF16),
        grid=(n // TM_PROJ,),
        in_specs=[
            pl.BlockSpec((TM_PROJ, D_MODEL), lambda i: (i, 0)),
            _layer_spec(layer, (1, D_MODEL)),
            _layer_spec(layer, (D_MODEL, W_HEAD)),
            _layer_spec(layer, (D_MODEL, Z_PAD)),
            _layer_spec(layer, (D_MODEL, W_TAIL)),
        ],
        out_specs=pl.BlockSpec((TM_PROJ, PROJ_DIM), lambda i: (i, 0)),
        compiler_params=pltpu.CompilerParams(
            dimension_semantics=("parallel",), vmem_limit_bytes=VMEM_LIMIT),
        name="in_proj",
    )(x2, gain, w_head, w_z, w_tail)


def _log_sigmoid(x):
    return jnp.minimum(x, 0.0) - jnp.log1p(jnp.exp(-jnp.abs(x)))


def _gla_kernel(q_ref, k_ref, v_ref, go_ref, z_ref, wd_ref, bd_ref, gn_ref, y_ref, st_ref):
    n_chunks = TC_GLA // GLA_CHUNK
    chunks_per_block = GLA_BLOCK // GLA_CHUNK

    @pl.when(pl.program_id(1) == 0)
    def _():
        st_ref[...] = jnp.zeros_like(st_ref)

    def pair(h):
        return slice((h // 2) * LANES, (h // 2 + 1) * LANES)

    def vcols(h):
        return slice(h * GLA_DV, (h + 1) * GLA_DV)

    def crow(c):
        return slice(c * GLA_CHUNK, (c + 1) * GLA_CHUNK)

    zl = jnp.dot(z_ref[...], wd_ref[...], preferred_element_type=F32) + bd_ref[...]
    la = _log_sigmoid(zl) / GLA_TAU

    hi = la.astype(BF16)
    r1 = la - hi.astype(F32)
    mid = r1.astype(BF16)
    lo = (r1 - mid.astype(F32)).astype(BF16)
    pieces = jnp.concatenate([hi, mid, lo], axis=1)
    row = lax.broadcasted_iota(jnp.int32, (GLA_CHUNK, GLA_CHUNK), 0)
    col = lax.broadcasted_iota(jnp.int32, (GLA_CHUNK, GLA_CHUNK), 1)
    tril_bf = jnp.where(row >= col, 1.0, 0.0).astype(BF16)
    cums, totals, decay = [], [], []
    for c in range(n_chunks):
        cs = jnp.dot(tril_bf, pieces[crow(c), :], preferred_element_type=F32)
        cum_c = cs[:, :GLA_QK_DIM] + cs[:, GLA_QK_DIM:2 * GLA_QK_DIM] + cs[:, 2 * GLA_QK_DIM:]
        last = cum_c[GLA_CHUNK - 1:GLA_CHUNK, :]
        cums.append(cum_c)
        totals.append(jnp.broadcast_to(last, (GLA_CHUNK, GLA_QK_DIM)))
        decay.append(jnp.exp(last))
    cum = jnp.concatenate(cums, axis=0)
    total = jnp.concatenate(totals, axis=0)

    q = q_ref[...].astype(F32)
    k = k_ref[...].astype(F32)
    q_dec = (q * (GLA_DK ** -0.5) * jnp.exp(cum)).astype(BF16)
    k_dec = k * jnp.exp(-cum)
    k_end = k * jnp.exp(total - cum)
    lane = lax.broadcasted_iota(jnp.int32, (1, LANES), 1)
    head_mask = (jnp.where(lane < GLA_DK, 1.0, 0.0), jnp.where(lane >= GLA_DK, 1.0, 0.0))
    k_dec_h = [(k_dec[:, pair(h)] * head_mask[h % 2]).astype(BF16) for h in range(GLA_HEADS)]
    k_end_h = [(k_end[:, pair(h)] * head_mask[h % 2]).astype(BF16) for h in range(GLA_HEADS)]

    state_in = [[None] * GLA_HEADS for _ in range(n_chunks)]
    for h in range(GLA_HEADS):
        st = st_ref[h]
        for c in range(n_chunks):
            state_in[c][h] = st.astype(BF16)
            st = st * decay[c][:, pair(h)] + _tn_dot(v_ref[crow(c), vcols(h)], k_end_h[h][crow(c), :])
        st_ref[h] = st

    row = lax.broadcasted_iota(jnp.int32, (GLA_BLOCK, GLA_BLOCK), 0)
    col = lax.broadcasted_iota(jnp.int32, (GLA_BLOCK, GLA_BLOCK), 1)
    visible = (row >= col) & (row // GLA_CHUNK == col // GLA_CHUNK)
    gn = gn_ref[...]
    for b in range(TC_GLA // GLA_BLOCK):
        rb = slice(b * GLA_BLOCK, (b + 1) * GLA_BLOCK)
        for h in range(GLA_HEADS):
            scores = jnp.where(visible, _nt_dot(q_dec[rb, pair(h)], k_dec_h[h][rb, :]), 0.0).astype(BF16)
            o = jnp.dot(scores, v_ref[rb, vcols(h)], preferred_element_type=F32)
            o = o + jnp.concatenate(
                [_nt_dot(q_dec[crow(c), pair(h)], state_in[c][h])
                 for c in range(b * chunks_per_block, (b + 1) * chunks_per_block)], axis=0)
            g = go_ref[rb, vcols(h)].astype(F32)
            y_ref[rb, vcols(h)] = (_rms(o, gn) * (g * jax.nn.sigmoid(g))).astype(BF16)


def _gla(layer, proj, w_decay, b_decay, norm_gain, batch, seq):
    n = proj.shape[0]
    tps = seq // TC_GLA
    row = lambda b, t: b * tps + t
    return pl.pallas_call(
        _gla_kernel,
        out_shape=jax.ShapeDtypeStruct((n, GLA_V_DIM), BF16),
        grid=(batch, tps),
        in_specs=[
            pl.BlockSpec((TC_GLA, GLA_QK_DIM), lambda b, t: (row(b, t), C_GQ // GLA_QK_DIM)),
            pl.BlockSpec((TC_GLA, GLA_QK_DIM), lambda b, t: (row(b, t), C_GK // GLA_QK_DIM)),
            pl.BlockSpec((TC_GLA, GLA_V_DIM), lambda b, t: (row(b, t), C_GV // GLA_V_DIM)),
            pl.BlockSpec((TC_GLA, GLA_V_DIM), lambda b, t: (row(b, t), C_GO // GLA_V_DIM)),
            pl.BlockSpec((TC_GLA, Z_PAD), lambda b, t: (row(b, t), C_Z // Z_PAD)),
            _layer_spec(layer, (Z_PAD, GLA_QK_DIM)),
            _layer_spec(layer, (1, GLA_QK_DIM)),
            _layer_spec(layer, (1, GLA_DV)),
        ],
        out_specs=pl.BlockSpec((TC_GLA, GLA_V_DIM), lambda b, t: (row(b, t), 0)),
        scratch_shapes=[pltpu.VMEM((GLA_HEADS, GLA_DV, LANES), F32)],
        compiler_params=pltpu.CompilerParams(
            dimension_semantics=("parallel", "arbitrary"), vmem_limit_bytes=VMEM_LIMIT),
        name="gla",
    )(proj, proj, proj, proj, proj, w_decay, b_decay, norm_gain)


ONES_ROWS = BF16_SUBLANES
LOG2E = 1.4426950408889634


def _diff_kernel(lam_ref, q_ref, k_ref, v_ref, bias_ref, gn_ref, y_ref,
                 qs_ref, vt_ref, s0_ref, s1_ref, mx0_ref, mx1_ref, m_ref, acc_ref, *, lam_init):
    i = pl.program_id(1)
    t = T_ATT

    def rows(j):
        return pl.ds(pl.multiple_of(j * t, t), t)

    def head_cols(h):
        return slice(h * LANES, (h + 1) * LANES)

    def start():
        lane = lax.broadcasted_iota(jnp.int32, (1, LANES), 1)
        for h in range(DIFF_HEADS):
            q = q_ref[:, head_cols(h)].astype(F32) * (DIFF_DK ** -0.5 * LOG2E)
            qs_ref[h, 0:t, :] = jnp.where(lane < DIFF_DK, q, 0.0).astype(BF16)
            qs_ref[h, t:2 * t, :] = jnp.where(lane >= DIFF_DK, q, 0.0).astype(BF16)
        m_ref[...] = jnp.full_like(m_ref, -1e30)
        acc_ref[...] = jnp.zeros_like(acc_ref)

    @pl.when(i == 0)
    def _():
        vt_ref[:, :, DIFF_DV:, :] = jnp.ones((DIFF_HEADS, vt_ref.shape[1], ONES_ROWS, t), BF16)

        def transpose_block(jb, carry):
            for h in range(DIFF_HEADS):
                vt_ref[h, jb, 0:DIFF_DV, :] = v_ref[rows(jb), head_cols(h)].T
            return carry

        lax.fori_loop(0, vt_ref.shape[1], transpose_block, 0)

    def logits(j, buf):
        s_ref, mx_ref = buf
        for h in range(DIFF_HEADS):
            s = _nt_dot(k_ref[rows(j), head_cols(h)], qs_ref[h])
            s_ref[h] = s
            mx_ref[h] = jnp.max(s, axis=0, keepdims=True)

    def softmax_pv(j, buf, bias_cols):
        s_ref, mx_ref = buf
        for h in range(DIFF_HEADS):
            s = s_ref[h]
            if bias_cols is None:
                s_max = mx_ref[h]
            else:
                bias = bias_ref[h, :, bias_cols]
                s = s + jnp.concatenate([bias, bias], axis=1)
                s_max = jnp.max(s, axis=0, keepdims=True)
            m_old = m_ref[h]
            m_new = jnp.maximum(m_old, s_max)
            alpha = jnp.exp2(m_old - m_new)
            p = jnp.exp2(s - m_new).astype(BF16)
            m_ref[h] = m_new
            acc_ref[h] = alpha * acc_ref[h] + jnp.dot(vt_ref[h, j], p, preferred_element_type=F32)

    s0 = (s0_ref, mx0_ref)
    s1 = (s1_ref, mx1_ref)
    n_far = jnp.maximum(i - 1, 0)
    peeled = n_far % 2

    def far_pair(j):
        logits(j + 1, s0)
        softmax_pv(j, s1, None)
        logits(j + 2, s1)
        softmax_pv(j + 1, s0, None)

    n_pairs = n_far // 2
    peeled_pair = n_pairs % 2

    @pl.when(i == 0)
    def _():
        start()
        logits(0, s0)

    def fill_odd(with_pair):
        start()
        logits(0, s1)
        if with_pair:
            far_pair(0)

    def fill_even(with_pair):
        start()
        logits(0, s0)
        logits(1, s1)
        softmax_pv(0, s0, None)
        if with_pair:
            far_pair(1)

    for with_pair in (False, True):
        pl.when((i % 2 == 1) & (peeled_pair == int(with_pair)))(functools.partial(fill_odd, with_pair))
        pl.when((peeled == 1) & (peeled_pair == int(with_pair)))(functools.partial(fill_even, with_pair))

    n_quads = n_pairs // 2
    peeled_quad = n_quads % 2

    n_octs = n_quads // 2
    first_oct = peeled + 2 * peeled_pair

    def far_oct(jj, carry):
        for d in range(0, 8, 2):
            far_pair(first_oct + 8 * jj + d)
        return carry

    lax.fori_loop(0, n_octs, far_oct, 0)

    def finish():
        lam_p = lam_ref[...]
        lam = (jnp.exp(jnp.sum(lam_p[0:1] * lam_p[1:2], axis=1, keepdims=True))
               - jnp.exp(jnp.sum(lam_p[2:3] * lam_p[3:4], axis=1, keepdims=True)) + lam_init)
        for h in range(DIFF_HEADS):
            acc = acc_ref[h, 0:DIFF_DV, :]
            inv_l = 1.0 / acc_ref[h, DIFF_DV:DIFF_DV + 1, :]
            o = acc[:, :t] * inv_l[:, :t] - lam * (acc[:, t:] * inv_l[:, t:])
            ms = jnp.mean(o * o, axis=0, keepdims=True)
            o = o * lax.rsqrt(ms + RMS_EPS) * gn_ref[...] * (1.0 - lam_init)
            y_ref[:, h * DIFF_DV:(h + 1) * DIFF_DV] = o.T.astype(BF16)

    def drain(with_quad):
        if with_quad:
            far_pair(first_oct + 8 * n_octs)
            far_pair(first_oct + 8 * n_octs + 2)
        logits(i, s0)
        softmax_pv(i - 1, s1, slice(t, 2 * t))
        softmax_pv(i, s0, slice(0, t))
        finish()

    for with_quad in (False, True):
        pl.when((i >= 1) & (peeled_quad == int(with_quad)))(functools.partial(drain, with_quad))

    @pl.when(i == 0)
    def _():
        softmax_pv(i, s0, slice(0, t))
        finish()


def _diff(layer, proj, lam, bias, norm_gain_col, lam_init, batch, seq):
    n = proj.shape[0]
    nq = seq // T_ATT
    hw = DIFF_HEADS * LANES
    return pl.pallas_call(
        functools.partial(_diff_kernel, lam_init=lam_init),
        out_shape=jax.ShapeDtypeStruct((n, DIFF_HEADS * DIFF_DV), BF16),
        grid=(batch, nq),
        in_specs=[
            _layer_spec(layer, (4, DIFF_DK)),
            pl.BlockSpec((T_ATT, hw), lambda b, i: (b * nq + i, C_DQ // hw)),
            pl.BlockSpec((seq, hw), lambda b, i: (b, C_DK // hw)),
            pl.BlockSpec((seq, hw), lambda b, i: (b, C_DV // hw)),
            _const_spec((DIFF_HEADS, T_ATT, 2 * T_ATT)),
            _layer_spec(layer, (DIFF_DV, 1)),
        ],
        out_specs=pl.BlockSpec((T_ATT, DIFF_HEADS * DIFF_DV), lambda b, i: (b * nq + i, 0)),
        scratch_shapes=[
            pltpu.VMEM((DIFF_HEADS, 2 * T_ATT, LANES), BF16),
            pltpu.VMEM((DIFF_HEADS, nq, DIFF_DV + ONES_ROWS, T_ATT), BF16),
            pltpu.VMEM((DIFF_HEADS, T_ATT, 2 * T_ATT), F32),
            pltpu.VMEM((DIFF_HEADS, T_ATT, 2 * T_ATT), F32),
            pltpu.VMEM((DIFF_HEADS, 1, 2 * T_ATT), F32),
            pltpu.VMEM((DIFF_HEADS, 1, 2 * T_ATT), F32),
            pltpu.VMEM((DIFF_HEADS, 1, 2 * T_ATT), F32),
            pltpu.VMEM((DIFF_HEADS, DIFF_DV + ONES_ROWS, 2 * T_ATT), F32),
        ],
        compiler_params=pltpu.CompilerParams(
            dimension_semantics=("parallel", "arbitrary"), vmem_limit_bytes=VMEM_LIMIT),
        name="diff_attn",
    )(lam, proj, proj, proj, bias, norm_gain_col)


def _t5_bucket(rel):
    n = jnp.maximum(rel, 0)
    max_exact = REL_BUCKETS // 2
    nf = jnp.maximum(n, 1).astype(F32)
    large = max_exact + (jnp.log(nf / max_exact) / math.log(REL_MAX_DIST / max_exact)
                         * (REL_BUCKETS - max_exact)).astype(jnp.int32)
    large = jnp.minimum(large, REL_BUCKETS - 1)
    return jnp.where(n < max_exact, n, large)


def _bias_tiles(rel_bias):
    assert T_ATT + 1 >= REL_MAX_DIST
    t = T_ATT
    table = rel_bias.astype(F32)
    near = (table[_t5_bucket(jnp.arange(2 * t, dtype=jnp.int32))] - table[REL_BUCKETS - 1]) * LOG2E
    by_rel = jnp.concatenate([near, jnp.full((t, DIFF_HEADS), -jnp.inf, F32)], axis=0).T
    period = 3 * t
    flat = jnp.tile(by_rel, (1, t))[:, :t * (period - 1)]
    return flat.reshape(DIFF_HEADS, t, period - 1)[:, :, :2 * t]


def _merge_kernel(x_ref, g0_ref, g1_ref, g2_ref, u_ref, halo_ref, ygla_ref, ydiff_ref,
                  pw_ref, ps_ref, wb_ref, wo_ref, gpost_ref, o_ref, *, tiles_per_seq):
    i = pl.program_id(0)
    tm = TM_MERGE
    t_in_seq = i % tiles_per_seq
    halo = jnp.where(t_in_seq == 0, 0.0, halo_ref[...].astype(F32))
    ucat = jnp.concatenate([halo, u_ref[...].astype(F32)], axis=0)
    pos = t_in_seq * tm + lax.broadcasted_iota(jnp.int32, (tm, 1), 0)
    mixed = []
    for g, w in enumerate(POOL_WINDOWS):
        cols = slice(g * POOL_GROUP_DIM, (g + 1) * POOL_GROUP_DIM)
        win = ucat[:, cols]
        span = 1
        while span < w:
            win = win + pltpu.roll(win, span, axis=0)
            span *= 2
        cur = ucat[POOL_HALO:, cols]
        win = win[POOL_HALO:, :]
        cnt = jnp.minimum(pos + 1, w).astype(F32)
        pooled = win / cnt - cur
        mixed.append(jnp.dot(pooled.astype(BF16), pw_ref[g], preferred_element_type=F32))
    y_pool = (jnp.concatenate(mixed, axis=1) * ps_ref[...]).astype(BF16)

    merged = jax.nn.sigmoid(g0_ref[...].astype(F32)) * jnp.dot(y_pool, wb_ref[0], preferred_element_type=F32)
    merged = merged + jax.nn.sigmoid(g1_ref[...].astype(F32)) * jnp.dot(
        ygla_ref[...], wb_ref[1], preferred_element_type=F32)
    merged = merged + jax.nn.sigmoid(g2_ref[...].astype(F32)) * jnp.dot(
        ydiff_ref[...], wb_ref[2], preferred_element_type=F32)
    out = jnp.dot(merged.astype(BF16), wo_ref[...], preferred_element_type=F32)
    o_ref[...] = x_ref[...] + _rms(out, gpost_ref[...])


def _merge(layer, x2, proj, y_gla, y_diff, pool_w, pool_scale, w_branch, w_o, gpost, seq):
    n = x2.shape[0]
    tm = TM_MERGE
    halo_blocks = tm // POOL_HALO
    return pl.pallas_call(
        functools.partial(_merge_kernel, tiles_per_seq=seq // tm),
        out_shape=jax.ShapeDtypeStruct((n, D_MODEL), F32),
        grid=(n // tm,),
        in_specs=[
            pl.BlockSpec((tm, D_MODEL), lambda i: (i, 0)),
            pl.BlockSpec((tm, D_MODEL), lambda i: (i, C_GATE // D_MODEL + 0)),
            pl.BlockSpec((tm, D_MODEL), lambda i: (i, C_GATE // D_MODEL + 1)),
            pl.BlockSpec((tm, D_MODEL), lambda i: (i, C_GATE // D_MODEL + 2)),
            pl.BlockSpec((tm, POOL_DIM), lambda i: (i, C_U // POOL_DIM)),
            pl.BlockSpec((POOL_HALO, POOL_DIM),
                         lambda i: (jnp.maximum(i * halo_blocks - 1, 0), C_U // POOL_DIM)),
            pl.BlockSpec((tm, BRANCH_DIM), lambda i: (i, 0)),
            pl.BlockSpec((tm, BRANCH_DIM), lambda i: (i, 0)),
            _layer_spec(layer, (len(POOL_WINDOWS), POOL_GROUP_DIM, POOL_GROUP_DIM)),
            _layer_spec(layer, (1, POOL_DIM)),
            _layer_spec(layer, (3, BRANCH_DIM, D_MODEL)),
            _layer_spec(layer, (D_MODEL, D_MODEL)),
            _layer_spec(layer, (1, D_MODEL)),
        ],
        out_specs=pl.BlockSpec((tm, D_MODEL), lambda i: (i, 0)),
        compiler_params=pltpu.CompilerParams(
            dimension_semantics=("parallel",), vmem_limit_bytes=VMEM_LIMIT),
        name="merge",
    )(x2, proj, proj, proj, proj, proj, y_gla, y_diff, pool_w, pool_scale, w_branch, w_o, gpost)


def _ffn_kernel(x_ref, gpre_ref, wg_ref, wu_ref, wd_ref, gpost_ref, o_ref):
    x = x_ref[...]
    h = _rms(x, gpre_ref[...]).astype(BF16)
    g = jnp.dot(h, wg_ref[...], preferred_element_type=F32)
    u = jnp.dot(h, wu_ref[...], preferred_element_type=F32)
    a = (g * jax.nn.sigmoid(g) * u).astype(BF16)
    f = jnp.dot(a, wd_ref[...], preferred_element_type=F32)
    o_ref[...] = x + _rms(f, gpost_ref[...])


def _ffn(layer, x2, gpre, wg, wu, wd, gpost):
    n = x2.shape[0]
    tm = TM_FFN
    return pl.pallas_call(
        _ffn_kernel,
        out_shape=jax.ShapeDtypeStruct((n, D_MODEL), F32),
        grid=(n // tm,),
        in_specs=[
            pl.BlockSpec((tm, D_MODEL), lambda i: (i, 0)),
            _layer_spec(layer, (1, D_MODEL)),
            _layer_spec(layer, (D_MODEL, FFN_DIM)),
            _layer_spec(layer, (D_MODEL, FFN_DIM)),
            _layer_spec(layer, (FFN_DIM, D_MODEL)),
            _layer_spec(layer, (1, D_MODEL)),
        ],
        out_specs=pl.BlockSpec((tm, D_MODEL), lambda i: (i, 0)),
        compiler_params=pltpu.CompilerParams(
            dimension_semantics=("parallel",), vmem_limit_bytes=VMEM_LIMIT),
        name="ffn",
    )(x2, gpre, wg, wu, wd, gpost)


def _split_w_in(w):
    z0 = int(sum(IN_SPLITS[:5]))
    assert z0 == W_HEAD and w.shape[-1] - z0 - GLA_GATE_RANK == W_TAIL
    w = w.astype(BF16)
    w_z = jnp.pad(w[..., z0:z0 + GLA_GATE_RANK], ((0, 0), (0, 0), (0, Z_PAD - GLA_GATE_RANK)))
    return w[..., :z0], w_z, w[..., z0 + GLA_GATE_RANK:]


def kernel(x, rel_bias, ln_mix_pre, w_in, pool_w, pool_scale, gla_w_decay, gla_b_decay, gla_norm,
           diff_lambda, diff_norm, w_branch, w_o, ln_mix_post, ln_ffn_pre, ffn_w_gate, ffn_w_up,
           ffn_w_down, ln_ffn_post):
    batch, seq, d = x.shape
    depth = w_in.shape[0]
    assert POOL_HALO >= max(POOL_WINDOWS)
    assert d == D_MODEL and seq % TC_GLA == 0 and seq % T_ATT == 0 and seq % TM_MERGE == 0
    x2 = x.reshape(batch * seq, d)
    bias = _bias_tiles(rel_bias)
    row = lambda p: p[:, None, :]
    w_head, w_z, w_tail = _split_w_in(w_in)
    w_decay = jnp.pad(gla_w_decay, ((0, 0), (0, Z_PAD - GLA_GATE_RANK), (0, 0))).astype(BF16)
    pool_w, w_branch, w_o = pool_w.astype(BF16), w_branch.astype(BF16), w_o.astype(BF16)
    ffn_w_gate, ffn_w_up, ffn_w_down = ffn_w_gate.astype(BF16), ffn_w_up.astype(BF16), ffn_w_down.astype(BF16)
    for l in range(depth):
        lam_init = 0.8 - 0.6 * math.exp(-0.3 * l)
        proj = _in_proj(l, x2, row(ln_mix_pre), w_head, w_z, w_tail)
        y_gla = _gla(l, proj, w_decay, row(gla_b_decay), row(gla_norm), batch, seq)
        y_diff = _diff(l, proj, diff_lambda, bias, diff_norm[:, :, None], lam_init, batch, seq)
        x2 = _merge(l, x2, proj, y_gla, y_diff, pool_w, row(pool_scale), w_branch, w_o, row(ln_mix_post), seq)
        x2 = _ffn(l, x2, row(ln_ffn_pre), ffn_w_gate, ffn_w_up, ffn_w_down, row(ln_ffn_post))
    return x2.reshape(batch, seq, d)
```

```python
import functools
import math

import jax
import jax.numpy as jnp
from jax import lax
from jax.experimental import pallas as pl
from jax.experimental.pallas import tpu as pltpu

F32 = jnp.float32
BF16 = jnp.bfloat16

D_MODEL = 1024
POOL_WINDOWS = (2, 4, 8, 16)
POOL_GROUP_DIM = 128
POOL_DIM = 512
GLA_HEADS = 4
GLA_DK = 64
GLA_DV = 128
GLA_QK_DIM = 256
GLA_V_DIM = 512
GLA_GATE_RANK = 16
GLA_TAU = 16.0
GLA_CHUNK = 64
DIFF_HEADS = 4
DIFF_DK = 64
DIFF_DV = 128
REL_BUCKETS = 32
REL_MAX_DIST = 128
BRANCH_DIM = 512
FFN_DIM = 2816
RMS_EPS = 1e-6
IN_SPLITS = (512, 256, 256, 512, 512, 16, 512, 512, 512, 3072)

LANES = 128
BF16_SUBLANES = 16
MXU_TILE = 256
VMEM_LIMIT = 56 * 1024 * 1024

C_GATE = 0
C_U = 3072
C_GQ = 3584
C_GK = 3840
C_GV = 4096
C_GO = 4608
C_DQ = 5120
C_DK = 5632
C_DV = 6144
C_Z = 6656
Z_PAD = LANES
PROJ_DIM = C_Z + Z_PAD

TM_PROJ = 1024
TM_MERGE = 1024
TM_FFN = 512
TC_GLA = 2048
GLA_BLOCK = MXU_TILE
T_ATT = 256
POOL_HALO = BF16_SUBLANES


def _rms(x, gain):
    ms = jnp.mean(x * x, axis=-1, keepdims=True)
    return x * lax.rsqrt(ms + RMS_EPS) * gain


def _nt_dot(a, b):
    return lax.dot_general(a, b, (((1,), (1,)), ((), ())), preferred_element_type=F32)


def _tn_dot(a, b):
    return lax.dot_general(a, b, (((0,), (0,)), ((), ())), preferred_element_type=F32)


def _const_spec(shape):
    nd = len(shape)
    return pl.BlockSpec(shape, lambda *_: (0,) * nd, pipeline_mode=pl.Buffered(1))


def _layer_spec(layer, shape):
    nd = len(shape)
    return pl.BlockSpec((None,) + tuple(shape), lambda *_: (layer,) + (0,) * nd, pipeline_mode=pl.Buffered(1))


W_HEAD = C_DQ - C_U
W_TAIL = C_Z - C_DQ + C_U
N_DIFF = C_Z - C_DQ
PROJ_COL_STEP = 4 * MXU_TILE


def _in_proj_kernel(x_ref, g_ref, wh_ref, wz_ref, wt_ref, o_ref):
    h = _rms(x_ref[...], g_ref[...]).astype(BF16)
    step = PROJ_COL_STEP
    segments = ((wt_ref, N_DIFF, W_TAIL - N_DIFF, C_GATE), (wh_ref, 0, W_HEAD, C_U),
                (wt_ref, 0, N_DIFF, C_DQ), (wz_ref, 0, Z_PAD, C_Z))
    for w_ref, src, width, dst in segments:
        for c in range(0, width, step):
            w = min(step, width - c)
            o_ref[:, dst + c:dst + c + w] = jnp.dot(
                h, w_ref[:, src + c:src + c + w], preferred_element_type=F32).astype(BF16)


def _in_proj(layer, x2, gain, w_head, w_z, w_tail):
    n = x2.shape[0]
    return pl.pallas_call(
        _in_proj_kernel,
        out_shape=jax.ShapeDtypeStruct((n, PROJ_DIM), BF16),
        grid=(n // TM_PROJ,),
        in_specs=[
            pl.BlockSpec((TM_PROJ, D_MODEL), lambda i: (i, 0)),
            _layer_spec(layer, (1, D_MODEL)),
            _layer_spec(layer, (D_MODEL, W_HEAD)),
            _layer_spec(layer, (D_MODEL, Z_PAD)),
            _layer_spec(layer, (D_MODEL, W_TAIL)),
        ],
        out_specs=pl.BlockSpec((TM_PROJ, PROJ_DIM), lambda i: (i, 0)),
        compiler_params=pltpu.CompilerParams(
            dimension_semantics=("parallel",), vmem_limit_bytes=VMEM_LIMIT),
        name="in_proj",
    )(x2, gain, w_head, w_z, w_tail)


def _log_sigmoid(x):
    return jnp.minimum(x, 0.0) - jnp.log1p(jnp.exp(-jnp.abs(x)))


def _gla_kernel(q_ref, k_ref, v_ref, go_ref, z_ref, wd_ref, bd_ref, gn_ref, y_ref, st_ref):
    n_chunks = TC_GLA // GLA_CHUNK
    chunks_per_block = GLA_BLOCK // GLA_CHUNK

    @pl.when(pl.program_id(1) == 0)
    def _():
        st_ref[...] = jnp.zeros_like(st_ref)

    def pair(h):
        return slice((h // 2) * LANES, (h // 2 + 1) * LANES)

    def vcols(h):
        return slice(h * GLA_DV, (h + 1) * GLA_DV)

    def crow(c):
        return slice(c * GLA_CHUNK, (c + 1) * GLA_CHUNK)

    zl = jnp.dot(z_ref[...], wd_ref[...], preferred_element_type=F32) + bd_ref[...]
    la = _log_sigmoid(zl) / GLA_TAU

    hi = la.astype(BF16)
    r1 = la - hi.astype(F32)
    mid = r1.astype(BF16)
    lo = (r1 - mid.astype(F32)).astype(BF16)
    pieces = jnp.concatenate([hi, mid, lo], axis=1)
    row = lax.broadcasted_iota(jnp.int32, (GLA_CHUNK, GLA_CHUNK), 0)
    col = lax.broadcasted_iota(jnp.int32, (GLA_CHUNK, GLA_CHUNK), 1)
    tril_bf = jnp.where(row >= col, 1.0, 0.0).astype(BF16)
    cums, totals, decay = [], [], []
    for c in range(n_chunks):
        cs = jnp.dot(tril_bf, pieces[crow(c), :], preferred_element_type=F32)
        cum_c = cs[:, :GLA_QK_DIM] + cs[:, GLA_QK_DIM:2 * GLA_QK_DIM] + cs[:, 2 * GLA_QK_DIM:]
        last = cum_c[GLA_CHUNK - 1:GLA_CHUNK, :]
        cums.append(cum_c)
        totals.append(jnp.broadcast_to(last, (GLA_CHUNK, GLA_QK_DIM)))
        decay.append(jnp.exp(last))
    cum = jnp.concatenate(cums, axis=0)
    total = jnp.concatenate(totals, axis=0)

    q = q_ref[...].astype(F32)
    k = k_ref[...].astype(F32)
    q_dec = (q * (GLA_DK ** -0.5) * jnp.exp(cum)).astype(BF16)
    k_dec = k * jnp.exp(-cum)
    k_end = k * jnp.exp(total - cum)
    lane = lax.broadcasted_iota(jnp.int32, (1, LANES), 1)
    head_mask = (jnp.where(lane < GLA_DK, 1.0, 0.0), jnp.where(lane >= GLA_DK, 1.0, 0.0))
    k_dec_h = [(k_dec[:, pair(h)] * head_mask[h % 2]).astype(BF16) for h in range(GLA_HEADS)]
    k_end_h = [(k_end[:, pair(h)] * head_mask[h % 2]).astype(BF16) for h in range(GLA_HEADS)]

    state_in = [[None] * GLA_HEADS for _ in range(n_chunks)]
    for h in range(GLA_HEADS):
        st = st_ref[h]
        for c in range(n_chunks):
            state_in[c][h] = st.astype(BF16)
            st = st * decay[c][:, pair(h)] + _tn_dot(v_ref[crow(c), vcols(h)], k_end_h[h][crow(c), :])
        st_ref[h] = st

    row = lax.broadcasted_iota(jnp.int32, (GLA_BLOCK, GLA_BLOCK), 0)
    col = lax.broadcasted_iota(jnp.int32, (GLA_BLOCK, GLA_BLOCK), 1)
    visible = (row >= col) & (row // GLA_CHUNK == col // GLA_CHUNK)
    gn = gn_ref[...]
    for b in range(TC_GLA // GLA_BLOCK):
        rb = slice(b * GLA_BLOCK, (b + 1) * GLA_BLOCK)
        for h in range(GLA_HEADS):
            scores = jnp.where(visible, _nt_dot(q_dec[rb, pair(h)], k_dec_h[h][rb, :]), 0.0).astype(BF16)
            o = jnp.dot(scores, v_ref[rb, vcols(h)], preferred_element_type=F32)
            o = o + jnp.concatenate(
                [_nt_dot(q_dec[crow(c), pair(h)], state_in[c][h])
                 for c in range(b * chunks_per_block, (b + 1) * chunks_per_block)], axis=0)
            g = go_ref[rb, vcols(h)].astype(F32)
            y_ref[rb, vcols(h)] = (_rms(o, gn) * (g * jax.nn.sigmoid(g))).astype(BF16)


def _gla(layer, proj, w_decay, b_decay, norm_gain, batch, seq):
    n = proj.shape[0]
    tps = seq // TC_GLA
    row = lambda b, t: b * tps + t
    return pl.pallas_call(
        _gla_kernel,
        out_shape=jax.ShapeDtypeStruct((n, GLA_V_DIM), BF16),
        grid=(batch, tps),
        in_specs=[
            pl.BlockSpec((TC_GLA, GLA_QK_DIM), lambda b, t: (row(b, t), C_GQ // GLA_QK_DIM)),
            pl.BlockSpec((TC_GLA, GLA_QK_DIM), lambda b, t: (row(b, t), C_GK // GLA_QK_DIM)),
            pl.BlockSpec((TC_GLA, GLA_V_DIM), lambda b, t: (row(b, t), C_GV // GLA_V_DIM)),
            pl.BlockSpec((TC_GLA, GLA_V_DIM), lambda b, t: (row(b, t), C_GO // GLA_V_DIM)),
            pl.BlockSpec((TC_GLA, Z_PAD), lambda b, t: (row(b, t), C_Z // Z_PAD)),
            _layer_spec(layer, (Z_PAD, GLA_QK_DIM)),
            _layer_spec(layer, (1, GLA_QK_DIM)),
            _layer_spec(layer, (1, GLA_DV)),
        ],
        out_specs=pl.BlockSpec((TC_GLA, GLA_V_DIM), lambda b, t: (row(b, t), 0)),
        scratch_shapes=[pltpu.VMEM((GLA_HEADS, GLA_DV, LANES), F32)],
        compiler_params=pltpu.CompilerParams(
            dimension_semantics=("parallel", "arbitrary"), vmem_limit_bytes=VMEM_LIMIT),
        name="gla",
    )(proj, proj, proj, proj, proj, w_decay, b_decay, norm_gain)


ONES_ROWS = BF16_SUBLANES
LOG2E = 1.4426950408889634


def _diff_kernel(lam_ref, q_ref, k_ref, v_ref, bias_ref, gn_ref, y_ref,
                 qs_ref, vt_ref, s0_ref, s1_ref, mx0_ref, mx1_ref, m_ref, acc_ref, *, lam_init):
    i = pl.program_id(1)
    t = T_ATT

    def rows(j):
        return pl.ds(pl.multiple_of(j * t, t), t)

    def head_cols(h):
        return slice(h * LANES, (h + 1) * LANES)

    def start():
        lane = lax.broadcasted_iota(jnp.int32, (1, LANES), 1)
        for h in range(DIFF_HEADS):
            q = q_ref[:, head_cols(h)].astype(F32) * (DIFF_DK ** -0.5 * LOG2E)
            qs_ref[h, 0:t, :] = jnp.where(lane < DIFF_DK, q, 0.0).astype(BF16)
            qs_ref[h, t:2 * t, :] = jnp.where(lane >= DIFF_DK, q, 0.0).astype(BF16)
        m_ref[...] = jnp.full_like(m_ref, -1e30)
        acc_ref[...] = jnp.zeros_like(acc_ref)

    @pl.when(i == 0)
    def _():
        vt_ref[:, :, DIFF_DV:, :] = jnp.ones((DIFF_HEADS, vt_ref.shape[1], ONES_ROWS, t), BF16)

        def transpose_block(jb, carry):
            for h in range(DIFF_HEADS):
                vt_ref[h, jb, 0:DIFF_DV, :] = v_ref[rows(jb), head_cols(h)].T
            return carry

        lax.fori_loop(0, vt_ref.shape[1], transpose_block, 0)

    def logits(j, buf):
        s_ref, mx_ref = buf
        for h in range(DIFF_HEADS):
            s = _nt_dot(k_ref[rows(j), head_cols(h)], qs_ref[h])
            s_ref[h] = s
            mx_ref[h] = jnp.max(s, axis=0, keepdims=True)

    def softmax_pv(j, buf, bias_cols):
        s_ref, mx_ref = buf
        for h in range(DIFF_HEADS):
            s = s_ref[h]
            if bias_cols is None:
                s_max = mx_ref[h]
            else:
                bias = bias_ref[h, :, bias_cols]
                s = s + jnp.concatenate([bias, bias], axis=1)
                s_max = jnp.max(s, axis=0, keepdims=True)
            m_old = m_ref[h]
            m_new = jnp.maximum(m_old, s_max)
            alpha = jnp.exp2(m_old - m_new)
            p = jnp.exp2(s - m_new).astype(BF16)
            m_ref[h] = m_new
            acc_ref[h] = alpha * acc_ref[h] + jnp.dot(vt_ref[h, j], p, preferred_element_type=F32)

    s0 = (s0_ref, mx0_ref)
    s1 = (s1_ref, mx1_ref)
    n_far = jnp.maximum(i - 1, 0)
    peeled = n_far % 2

    def far_pair(j):
        logits(j + 1, s0)
        softmax_pv(j, s1, None)
        logits(j + 2, s1)
        softmax_pv(j + 1, s0, None)

    n_pairs = n_far // 2
    peeled_pair = n_pairs % 2

    @pl.when(i == 0)
    def _():
        start()
        logits(0, s0)

    def fill_odd(with_pair):
        start()
        logits(0, s1)
        if with_pair:
            far_pair(0)

    def fill_even(with_pair):
        start()
        logits(0, s0)
        logits(1, s1)
        softmax_pv(0, s0, None)
        if with_pair:
            far_pair(1)

    for with_pair in (False, True):
        pl.when((i % 2 == 1) & (peeled_pair == int(with_pair)))(functools.partial(fill_odd, with_pair))
        pl.when((peeled == 1) & (peeled_pair == int(with_pair)))(functools.partial(fill_even, with_pair))

    n_quads = n_pairs // 2
    peeled_quad = n_quads % 2

    n_octs = n_quads // 2
    first_oct = peeled + 2 * peeled_pair

    def far_oct(jj, carry):
        for d in range(0, 8, 2):
            far_pair(first_oct + 8 * jj + d)
        return carry

    lax.fori_loop(0, n_octs, far_oct, 0)

    def finish():
        lam_p = lam_ref[...]
        lam = (jnp.exp(jnp.sum(lam_p[0:1] * lam_p[1:2], axis=1, keepdims=True))
               - jnp.exp(jnp.sum(lam_p[2:3] * lam_p[3:4], axis=1, keepdims=True)) + lam_init)
        for h in range(DIFF_HEADS):
            acc = acc_ref[h, 0:DIFF_DV, :]
            inv_l = 1.0 / acc_ref[h, DIFF_DV:DIFF_DV + 1, :]
            o = acc[:, :t] * inv_l[:, :t] - lam * (acc[:, t:] * inv_l[:, t:])
            ms = jnp.mean(o * o, axis=0, keepdims=True)
            o = o * lax.rsqrt(ms + RMS_EPS) * gn_ref[...] * (1.0 - lam_init)
            y_ref[:, h * DIFF_DV:(h + 1) * DIFF_DV] = o.T.astype(BF16)

    def drain(with_quad):
        if with_quad:
            far_pair(first_oct + 8 * n_octs)
            far_pair(first_oct + 8 * n_octs + 2)
        logits(i, s0)
        softmax_pv(i - 1, s1, slice(t, 2 * t))
        softmax_pv(i, s0, slice(0, t))
        finish()

    for with_quad in (False, True):
        pl.when((i >= 1) & (peeled_quad == int(with_quad)))(functools.partial(drain, with_quad))

    @pl.when(i == 0)
    def _():
        softmax_pv(i, s0, slice(0, t))
        finish()


def _diff(layer, proj, lam, bias, norm_gain_col, lam_init, batch, seq):
    n = proj.shape[0]
    nq = seq // T_ATT
    hw = DIFF_HEADS * LANES
    return pl.pallas_call(
        functools.partial(_diff_kernel, lam_init=lam_init),
        out_shape=jax.ShapeDtypeStruct((n, DIFF_HEADS * DIFF_DV), BF16),
        grid=(batch, nq),
        in_specs=[
            _layer_spec(layer, (4, DIFF_DK)),
            pl.BlockSpec((T_ATT, hw), lambda b, i: (b * nq + i, C_DQ // hw)),
            pl.BlockSpec((seq, hw), lambda b, i: (b, C_DK // hw)),
            pl.BlockSpec((seq, hw), lambda b, i: (b, C_DV // hw)),
            _const_spec((DIFF_HEADS, T_ATT, 2 * T_ATT)),
            _layer_spec(layer, (DIFF_DV, 1)),
        ],
        out_specs=pl.BlockSpec((T_ATT, DIFF_HEADS * DIFF_DV), lambda b, i: (b * nq + i, 0)),
        scratch_shapes=[
            pltpu.VMEM((DIFF_HEADS, 2 * T_ATT, LANES), BF16),
            pltpu.VMEM((DIFF_HEADS, nq, DIFF_DV + ONES_ROWS, T_ATT), BF16),
            pltpu.VMEM((DIFF_HEADS, T_ATT, 2 * T_ATT), F32),
            pltpu.VMEM((DIFF_HEADS, T_ATT, 2 * T_ATT), F32),
            pltpu.VMEM((DIFF_HEADS, 1, 2 * T_ATT), F32),
            pltpu.VMEM((DIFF_HEADS, 1, 2 * T_ATT), F32),
            pltpu.VMEM((DIFF_HEADS, 1, 2 * T_ATT), F32),
            pltpu.VMEM((DIFF_HEADS, DIFF_DV + ONES_ROWS, 2 * T_ATT), F32),
        ],
        compiler_params=pltpu.CompilerParams(
            dimension_semantics=("parallel", "arbitrary"), vmem_limit_bytes=VMEM_LIMIT),
        name="diff_attn",
    )(lam, proj, proj, proj, bias, norm_gain_col)


def _t5_bucket(rel):
    n = jnp.maximum(rel, 0)
    max_exact = REL_BUCKETS // 2
    nf = jnp.maximum(n, 1).astype(F32)
    large = max_exact + (jnp.log(nf / max_exact) / math.log(REL_MAX_DIST / max_exact)
                         * (REL_BUCKETS - max_exact)).astype(jnp.int32)
    large = jnp.minimum(large, REL_BUCKETS - 1)
    return jnp.where(n < max_exact, n, large)


def _bias_tiles(rel_bias):
    assert T_ATT + 1 >= REL_MAX_DIST
    t = T_ATT
    table = rel_bias.astype(F32)
    near = (table[_t5_bucket(jnp.arange(2 * t, dtype=jnp.int32))] - table[REL_BUCKETS - 1]) * LOG2E
    by_rel = jnp.concatenate([near, jnp.full((t, DIFF_HEADS), -jnp.inf, F32)], axis=0).T
    period = 3 * t
    flat = jnp.tile(by_rel, (1, t))[:, :t * (period - 1)]
    return flat.reshape(DIFF_HEADS, t, period - 1)[:, :, :2 * t]


def _merge_kernel(x_ref, g0_ref, g1_ref, g2_ref, u_ref, halo_ref, ygla_ref, ydiff_ref,
                  pw_ref, ps_ref, wb_ref, wo_ref, gpost_ref, o_ref, *, tiles_per_seq):
    i = pl.program_id(0)
    tm = TM_MERGE
    t_in_seq = i % tiles_per_seq
    halo = jnp.where(t_in_seq == 0, 0.0, halo_ref[...].astype(F32))
    ucat = jnp.concatenate([halo, u_ref[...].astype(F32)], axis=0)
    pos = t_in_seq * tm + lax.broadcasted_iota(jnp.int32, (tm, 1), 0)
    mixed = []
    for g, w in enumerate(POOL_WINDOWS):
        cols = slice(g * POOL_GROUP_DIM, (g + 1) * POOL_GROUP_DIM)
        win = ucat[:, cols]
        span = 1
        while span < w:
            win = win + pltpu.roll(win, span, axis=0)
            span *= 2
        cur = ucat[POOL_HALO:, cols]
        win = win[POOL_HALO:, :]
        cnt = jnp.minimum(pos + 1, w).astype(F32)
        pooled = win / cnt - cur
        mixed.append(jnp.dot(pooled.astype(BF16), pw_ref[g], preferred_element_type=F32))
    y_pool = (jnp.concatenate(mixed, axis=1) * ps_ref[...]).astype(BF16)

    merged = jax.nn.sigmoid(g0_ref[...].astype(F32)) * jnp.dot(y_pool, wb_ref[0], preferred_element_type=F32)
    merged = merged + jax.nn.sigmoid(g1_ref[...].astype(F32)) * jnp.dot(
        ygla_ref[...], wb_ref[1], preferred_element_type=F32)
    merged = merged + jax.nn.sigmoid(g2_ref[...].astype(F32)) * jnp.dot(
        ydiff_ref[...], wb_ref[2], preferred_element_type=F32)
    out = jnp.dot(merged.astype(BF16), wo_ref[...], preferred_element_type=F32)
    o_ref[...] = x_ref[...] + _rms(out, gpost_ref[...])


def _merge(layer, x2, proj, y_gla, y_diff, pool_w, pool_scale, w_branch, w_o, gpost, seq):
    n = x2.shape[0]
    tm = TM_MERGE
    halo_blocks = tm // POOL_HALO
    return pl.pallas_call(
        functools.partial(_merge_kernel, tiles_per_seq=seq // tm),
        out_shape=jax.ShapeDtypeStruct((n, D_MODEL), F32),
        grid=(n // tm,),
        in_specs=[
            pl.BlockSpec((tm, D_MODEL), lambda i: (i, 0)),
            pl.BlockSpec((tm, D_MODEL), lambda i: (i, C_GATE // D_MODEL + 0)),
            pl.BlockSpec((tm, D_MODEL), lambda i: (i, C_GATE // D_MODEL + 1)),
            pl.BlockSpec((tm, D_MODEL), lambda i: (i, C_GATE // D_MODEL + 2)),
            pl.BlockSpec((tm, POOL_DIM), lambda i: (i, C_U // POOL_DIM)),
            pl.BlockSpec((POOL_HALO, POOL_DIM),
                         lambda i: (jnp.maximum(i * halo_blocks - 1, 0), C_U // POOL_DIM)),
            pl.BlockSpec((tm, BRANCH_DIM), lambda i: (i, 0)),
            pl.BlockSpec((tm, BRANCH_DIM), lambda i: (i, 0)),
            _layer_spec(layer, (len(POOL_WINDOWS), POOL_GROUP_DIM, POOL_GROUP_DIM)),
            _layer_spec(layer, (1, POOL_DIM)),
            _layer_spec(layer, (3, BRANCH_DIM, D_MODEL)),
            _layer_spec(layer, (D_MODEL, D_MODEL)),
            _layer_spec(layer, (1, D_MODEL)),
        ],
        out_specs=pl.BlockSpec((tm, D_MODEL), lambda i: (i, 0)),
        compiler_params=pltpu.CompilerParams(
            dimension_semantics=("parallel",), vmem_limit_bytes=VMEM_LIMIT),
        name="merge",
    )(x2, proj, proj, proj, proj, proj, y_gla, y_diff, pool_w, pool_scale, w_branch, w_o, gpost)


def _ffn_kernel(x_ref, gpre_ref, wg_ref, wu_ref, wd_ref, gpost_ref, o_ref):
    x = x_ref[...]
    h = _rms(x, gpre_ref[...]).astype(BF16)
    g = jnp.dot(h, wg_ref[...], preferred_element_type=F32)
    u = jnp.dot(h, wu_ref[...], preferred_element_type=F32)
    a = (g * jax.nn.sigmoid(g) * u).astype(BF16)
    f = jnp.dot(a, wd_ref[...], preferred_element_type=F32)
    o_ref[...] = x + _rms(f, gpost_ref[...])


def _ffn(layer, x2, gpre, wg, wu, wd, gpost):
    n = x2.shape[0]
    tm = TM_FFN
    return pl.pallas_call(
        _ffn_kernel,
        out_shape=jax.ShapeDtypeStruct((n, D_MODEL), F32),
        grid=(n // tm,),
        in_specs=[
            pl.BlockSpec((tm, D_MODEL), lambda i: (i, 0)),
            _layer_spec(layer, (1, D_MODEL)),
            _layer_spec(layer, (D_MODEL, FFN_DIM)),
            _layer_spec(layer, (D_MODEL, FFN_DIM)),
            _layer_spec(layer, (FFN_DIM, D_MODEL)),
            _layer_spec(layer, (1, D_MODEL)),
        ],
        out_specs=pl.BlockSpec((tm, D_MODEL), lambda i: (i, 0)),
        compiler_params=pltpu.CompilerParams(
            dimension_semantics=("parallel",), vmem_limit_bytes=VMEM_LIMIT),
        name="ffn",
    )(x2, gpre, wg, wu, wd, gpost)


def _split_w_in(w):
    z0 = int(sum(IN_SPLITS[:5]))
    assert z0 == W_HEAD and w.shape[-1] - z0 - GLA_GATE_RANK == W_TAIL
    w = w.astype(BF16)
    w_z = jnp.pad(w[..., z0:z0 + GLA_GATE_RANK], ((0, 0), (0, 0), (0, Z_PAD - GLA_GATE_RANK)))
    return w[..., :z0], w_z, w[..., z0 + GLA_GATE_RANK:]


def kernel(x, rel_bias, ln_mix_pre, w_in, pool_w, pool_scale, gla_w_decay, gla_b_decay, gla_norm,
           diff_lambda, diff_norm, w_branch, w_o, ln_mix_post, ln_ffn_pre, ffn_w_gate, ffn_w_up,
           ffn_w_down, ln_ffn_post):
    batch, seq, d = x.shape
    depth = w_in.shape[0]
    assert POOL_HALO >= max(POOL_WINDOWS)
    assert d == D_MODEL and seq % TC_GLA == 0 and seq % T_ATT == 0 and seq % TM_MERGE == 0
    x2 = x.reshape(batch * seq, d)
    bias = _bias_tiles(rel_bias)
    row = lambda p: p[:, None, :]
    w_head, w_z, w_tail = _split_w_in(w_in)
    w_decay = jnp.pad(gla_w_decay, ((0, 0), (0, Z_PAD - GLA_GATE_RANK), (0, 0))).astype(BF16)
    pool_w, w_branch, w_o = pool_w.astype(BF16), w_branch.astype(BF16), w_o.astype(BF16)
    ffn_w_gate, ffn_w_up, ffn_w_down = ffn_w_gate.astype(BF16), ffn_w_up.astype(BF16), ffn_w_down.astype(BF16)
    for l in range(depth):
        lam_init = 0.8 - 0.6 * math.exp(-0.3 * l)
        proj = _in_proj(l, x2, row(ln_mix_pre), w_head, w_z, w_tail)
        y_gla = _gla(l, proj, w_decay, row(gla_b_decay), row(gla_norm), batch, seq)
        y_diff = _diff(l, proj, diff_lambda, bias, diff_norm[:, :, None], lam_init, batch, seq)
        x2 = _merge(l, x2, proj, y_gla, y_diff, pool_w, row(pool_scale), w_branch, w_o, row(ln_mix_post), seq)
        x2 = _ffn(l, x2, row(ln_ffn_pre), ffn_w_gate, ffn_w_up, ffn_w_down, row(ln_ffn_post))
    return x2.reshape(batch, seq, d)
```

```python
import functools
import math

import jax
import jax.numpy as jnp
from jax import lax
from jax.experimental import pallas as pl
from jax.experimental.pallas import tpu as pltpu

F32 = jnp.float32
BF16 = jnp.bfloat16

D_MODEL = 1024
POOL_WINDOWS = (2, 4, 8, 16)
POOL_GROUP_DIM = 128
POOL_DIM = 512
GLA_HEADS = 4
GLA_DK = 64
GLA_DV = 128
GLA_QK_DIM = 256
GLA_V_DIM = 512
GLA_GATE_RANK = 16
GLA_TAU = 16.0
GLA_CHUNK = 64
DIFF_HEADS = 4
DIFF_DK = 64
DIFF_DV = 128
REL_BUCKETS = 32
REL_MAX_DIST = 128
BRANCH_DIM = 512
FFN_DIM = 2816
RMS_EPS = 1e-6
IN_SPLITS = (512, 256, 256, 512, 512, 16, 512, 512, 512, 3072)

LANES = 128
BF16_SUBLANES = 16
MXU_TILE = 256
VMEM_LIMIT = 56 * 1024 * 1024

C_GATE = 0
C_U = 3072
C_GQ = 3584
C_GK = 3840
C_GV = 4096
C_GO = 4608
C_DQ = 5120
C_DK = 5632
C_DV = 6144
C_Z = 6656
Z_PAD = LANES
PROJ_DIM = C_Z + Z_PAD

TM_PROJ = 512
TM_MERGE = 512
TM_FFN = 512
TC_GLA = 2048
GLA_BLOCK = MXU_TILE
T_ATT = 256
POOL_HALO = BF16_SUBLANES


def _rms(x, gain):
    ms = jnp.mean(x * x, axis=-1, keepdims=True)
    return x * lax.rsqrt(ms + RMS_EPS) * gain


def _nt_dot(a, b):
    return lax.dot_general(a, b, (((1,), (1,)), ((), ())), preferred_element_type=F32)


def _tn_dot(a, b):
    return lax.dot_general(a, b, (((0,), (0,)), ((), ())), preferred_element_type=F32)


def _const_spec(shape):
    nd = len(shape)
    return pl.BlockSpec(shape, lambda *_: (0,) * nd, pipeline_mode=pl.Buffered(1))


def _layer_spec(layer, shape):
    nd = len(shape)
    return pl.BlockSpec((None,) + tuple(shape), lambda *_: (layer,) + (0,) * nd, pipeline_mode=pl.Buffered(1))


W_HEAD = C_DQ - C_U
W_TAIL = C_Z - C_DQ + C_U
N_DIFF = C_Z - C_DQ
PROJ_COL_STEP = 4 * MXU_TILE


def _in_proj_kernel(x_ref, g_ref, wh_ref, wz_ref, wt_ref, o_ref):
    h = _rms(x_ref[...], g_ref[...]).astype(BF16)
    step = PROJ_COL_STEP
    segments = ((wt_ref, N_DIFF, W_TAIL - N_DIFF, C_GATE), (wh_ref, 0, W_HEAD, C_U),
                (wt_ref, 0, N_DIFF, C_DQ), (wz_ref, 0, Z_PAD, C_Z))
    for w_ref, src, width, dst in segments:
        for c in range(0, width, step):
            w = min(step, width - c)
            o_ref[:, dst + c:dst + c + w] = jnp.dot(
                h, w_ref[:, src + c:src + c + w], preferred_element_type=F32).astype(BF16)


def _in_proj(layer, x2, gain, w_head, w_z, w_tail):
    n = x2.shape[0]
    return pl.pallas_call(
        _in_proj_kernel,
        out_shape=jax.ShapeDtypeStruct((n, PROJ_DIM), BF16),
        grid=(n // TM_PROJ,),
        in_specs=[
            pl.BlockSpec((TM_PROJ, D_MODEL), lambda i: (i, 0)),
            _layer_spec(layer, (1, D_MODEL)),
            _layer_spec(layer, (D_MODEL, W_HEAD)),
            _layer_spec(layer, (D_MODEL, Z_PAD)),
            _layer_spec(layer, (D_MODEL, W_TAIL)),
        ],
        out_specs=pl.BlockSpec((TM_PROJ, PROJ_DIM), lambda i: (i, 0)),
        compiler_params=pltpu.CompilerParams(
            dimension_semantics=("parallel",), vmem_limit_bytes=VMEM_LIMIT),
        name="in_proj",
    )(x2, gain, w_head, w_z, w_tail)


def _log_sigmoid(x):
    return jnp.minimum(x, 0.0) - jnp.log1p(jnp.exp(-jnp.abs(x)))


def _gla_kernel(q_ref, k_ref, v_ref, go_ref, z_ref, wd_ref, bd_ref, gn_ref, y_ref, st_ref):
    n_chunks = TC_GLA // GLA_CHUNK
    chunks_per_block = GLA_BLOCK // GLA_CHUNK

    @pl.when(pl.program_id(1) == 0)
    def _():
        st_ref[...] = jnp.zeros_like(st_ref)

    def pair(h):
        return slice((h // 2) * LANES, (h // 2 + 1) * LANES)

    def vcols(h):
        return slice(h * GLA_DV, (h + 1) * GLA_DV)

    def crow(c):
        return slice(c * GLA_CHUNK, (c + 1) * GLA_CHUNK)

    zl = jnp.dot(z_ref[...], wd_ref[...], preferred_element_type=F32) + bd_ref[...]
    la = _log_sigmoid(zl) / GLA_TAU

    hi = la.astype(BF16)
    r1 = la - hi.astype(F32)
    mid = r1.astype(BF16)
    lo = (r1 - mid.astype(F32)).astype(BF16)
    pieces = jnp.concatenate([hi, mid, lo], axis=1)
    row = lax.broadcasted_iota(jnp.int32, (GLA_CHUNK, GLA_CHUNK), 0)
    col = lax.broadcasted_iota(jnp.int32, (GLA_CHUNK, GLA_CHUNK), 1)
    tril_bf = jnp.where(row >= col, 1.0, 0.0).astype(BF16)
    cums, totals, decay = [], [], []
    for c in range(n_chunks):
        cs = jnp.dot(tril_bf, pieces[crow(c), :], preferred_element_type=F32)
        cum_c = cs[:, :GLA_QK_DIM] + cs[:, GLA_QK_DIM:2 * GLA_QK_DIM] + cs[:, 2 * GLA_QK_DIM:]
        last = cum_c[GLA_CHUNK - 1:GLA_CHUNK, :]
        cums.append(cum_c)
        totals.append(jnp.broadcast_to(last, (GLA_CHUNK, GLA_QK_DIM)))
        decay.append(jnp.exp(last))
    cum = jnp.concatenate(cums, axis=0)
    total = jnp.concatenate(totals, axis=0)

    q = q_ref[...].astype(F32)
    k = k_ref[...].astype(F32)
    q_dec = (q * (GLA_DK ** -0.5) * jnp.exp(cum)).astype(BF16)
    k_dec = k * jnp.exp(-cum)
    k_end = k * jnp.exp(total - cum)
    lane = lax.broadcasted_iota(jnp.int32, (1, LANES), 1)
    head_mask = (jnp.where(lane < GLA_DK, 1.0, 0.0), jnp.where(lane >= GLA_DK, 1.0, 0.0))
    k_dec_h = [(k_dec[:, pair(h)] * head_mask[h % 2]).astype(BF16) for h in range(GLA_HEADS)]
    k_end_h = [(k_end[:, pair(h)] * head_mask[h % 2]).astype(BF16) for h in range(GLA_HEADS)]

    state_in = [[None] * GLA_HEADS for _ in range(n_chunks)]
    for h in range(GLA_HEADS):
        st = st_ref[h]
        for c in range(n_chunks):
            state_in[c][h] = st.astype(BF16)
            st = st * decay[c][:, pair(h)] + _tn_dot(v_ref[crow(c), vcols(h)], k_end_h[h][crow(c), :])
        st_ref[h] = st

    row = lax.broadcasted_iota(jnp.int32, (GLA_BLOCK, GLA_BLOCK), 0)
    col = lax.broadcasted_iota(jnp.int32, (GLA_BLOCK, GLA_BLOCK), 1)
    visible = (row >= col) & (row // GLA_CHUNK == col // GLA_CHUNK)
    gn = gn_ref[...]
    for b in range(TC_GLA // GLA_BLOCK):
        rb = slice(b * GLA_BLOCK, (b + 1) * GLA_BLOCK)
        for h in range(GLA_HEADS):
            scores = jnp.where(visible, _nt_dot(q_dec[rb, pair(h)], k_dec_h[h][rb, :]), 0.0).astype(BF16)
            o = jnp.dot(scores, v_ref[rb, vcols(h)], preferred_element_type=F32)
            o = o + jnp.concatenate(
                [_nt_dot(q_dec[crow(c), pair(h)], state_in[c][h])
                 for c in range(b * chunks_per_block, (b + 1) * chunks_per_block)], axis=0)
            g = go_ref[rb, vcols(h)].astype(F32)
            y_ref[rb, vcols(h)] = (_rms(o, gn) * (g * jax.nn.sigmoid(g))).astype(BF16)


def _gla(layer, proj, w_decay, b_decay, norm_gain, batch, seq):
    n = proj.shape[0]
    tps = seq // TC_GLA
    row = lambda b, t: b * tps + t
    return pl.pallas_call(
        _gla_kernel,
        out_shape=jax.ShapeDtypeStruct((n, GLA_V_DIM), BF16),
        grid=(batch, tps),
        in_specs=[
            pl.BlockSpec((TC_GLA, GLA_QK_DIM), lambda b, t: (row(b, t), C_GQ // GLA_QK_DIM)),
            pl.BlockSpec((TC_GLA, GLA_QK_DIM), lambda b, t: (row(b, t), C_GK // GLA_QK_DIM)),
            pl.BlockSpec((TC_GLA, GLA_V_DIM), lambda b, t: (row(b, t), C_GV // GLA_V_DIM)),
            pl.BlockSpec((TC_GLA, GLA_V_DIM), lambda b, t: (row(b, t), C_GO // GLA_V_DIM)),
            pl.BlockSpec((TC_GLA, Z_PAD), lambda b, t: (row(b, t), C_Z // Z_PAD)),
            _layer_spec(layer, (Z_PAD, GLA_QK_DIM)),
            _layer_spec(layer, (1, GLA_QK_DIM)),
            _layer_spec(layer, (1, GLA_DV)),
        ],
        out_specs=pl.BlockSpec((TC_GLA, GLA_V_DIM), lambda b, t: (row(b, t), 0)),
        scratch_shapes=[pltpu.VMEM((GLA_HEADS, GLA_DV, LANES), F32)],
        compiler_params=pltpu.CompilerParams(
            dimension_semantics=("parallel", "arbitrary"), vmem_limit_bytes=VMEM_LIMIT),
        name="gla",
    )(proj, proj, proj, proj, proj, w_decay, b_decay, norm_gain)


ONES_ROWS = BF16_SUBLANES
LOG2E = 1.4426950408889634


def _diff_kernel(lam_ref, q_ref, k_ref, v_ref, bias_ref, gn_ref, y_ref,
                 qs_ref, vt_ref, s0_ref, s1_ref, mx0_ref, mx1_ref, m_ref, acc_ref, *, lam_init):
    i = pl.program_id(1)
    t = T_ATT

    def rows(j):
        return pl.ds(pl.multiple_of(j * t, t), t)

    def head_cols(h):
        return slice(h * LANES, (h + 1) * LANES)

    def start():
        lane = lax.broadcasted_iota(jnp.int32, (1, LANES), 1)
        for h in range(DIFF_HEADS):
            q = q_ref[:, head_cols(h)].astype(F32) * (DIFF_DK ** -0.5 * LOG2E)
            qs_ref[h, 0:t, :] = jnp.where(lane < DIFF_DK, q, 0.0).astype(BF16)
            qs_ref[h, t:2 * t, :] = jnp.where(lane >= DIFF_DK, q, 0.0).astype(BF16)
        m_ref[...] = jnp.full_like(m_ref, -1e30)
        acc_ref[...] = jnp.zeros_like(acc_ref)

    @pl.when(i == 0)
    def _():
        vt_ref[:, :, DIFF_DV:, :] = jnp.ones((DIFF_HEADS, vt_ref.shape[1], ONES_ROWS, t), BF16)

        def transpose_block(jb, carry):
            for h in range(DIFF_HEADS):
                vt_ref[h, jb, 0:DIFF_DV, :] = v_ref[rows(jb), head_cols(h)].T
            return carry

        lax.fori_loop(0, vt_ref.shape[1], transpose_block, 0)

    diag_cols = slice(0, t)
    near_cols = slice(t, 2 * t)

    def logits(j, buf, bias_cols=None):
        s_ref, mx_ref = buf
        for h in range(DIFF_HEADS):
            s = _nt_dot(k_ref[rows(j), head_cols(h)], qs_ref[h])
            if bias_cols is not None:
                bias = bias_ref[h, :, bias_cols]
                s = s + jnp.concatenate([bias, bias], axis=1)
            s_ref[h] = s
            mx_ref[h] = jnp.max(s, axis=0, keepdims=True)

    def softmax_pv(j, buf, bias_cols):
        s_ref, mx_ref = buf
        for h in range(DIFF_HEADS):
            s = s_ref[h]
            if bias_cols is None:
                s_max = mx_ref[h]
            else:
                bias = bias_ref[h, :, bias_cols]
                s = s + jnp.concatenate([bias, bias], axis=1)
                s_max = jnp.max(s, axis=0, keepdims=True)
            m_old = m_ref[h]
            m_new = jnp.maximum(m_old, s_max)
            alpha = jnp.exp2(m_old - m_new)
            p = jnp.exp2(s - m_new).astype(BF16)
            m_ref[h] = m_new
            acc_ref[h] = alpha * acc_ref[h] + jnp.dot(vt_ref[h, j], p, preferred_element_type=F32)

    s0 = (s0_ref, mx0_ref)
    s1 = (s1_ref, mx1_ref)
    n_far = jnp.maximum(i - 1, 0)
    peeled = n_far % 2

    def far_pair(j):
        logits(j + 1, s0)
        softmax_pv(j, s1, None)
        logits(j + 2, s1)
        softmax_pv(j + 1, s0, None)

    n_pairs = n_far // 2
    peeled_pair = n_pairs % 2

    @pl.when(i == 0)
    def _():
        start()
        logits(0, s0, diag_cols)

    def fill_odd(with_pair):
        start()
        logits(0, s1)
        if with_pair:
            far_pair(0)

    def fill_even(with_pair):
        start()
        logits(0, s0)
        logits(1, s1)
        softmax_pv(0, s0, None)
        if with_pair:
            far_pair(1)

    for with_pair in (False, True):
        pl.when((i % 2 == 1) & (peeled_pair == int(with_pair)))(functools.partial(fill_odd, with_pair))
        pl.when((peeled == 1) & (peeled_pair == int(with_pair)))(functools.partial(fill_even, with_pair))

    n_quads = n_pairs // 2
    peeled_quad = n_quads % 2

    n_octs = n_quads // 2
    first_oct = peeled + 2 * peeled_pair

    def far_oct(jj, carry):
        for d in range(0, 8, 2):
            far_pair(first_oct + 8 * jj + d)
        return carry

    lax.fori_loop(0, n_octs, far_oct, 0)

    def finish():
        lam_p = lam_ref[...]
        lam = (jnp.exp(jnp.sum(lam_p[0:1] * lam_p[1:2], axis=1, keepdims=True))
               - jnp.exp(jnp.sum(lam_p[2:3] * lam_p[3:4], axis=1, keepdims=True)) + lam_init)
        for h in range(DIFF_HEADS):
            acc = acc_ref[h, 0:DIFF_DV, :]
            inv_l = 1.0 / acc_ref[h, DIFF_DV:DIFF_DV + 1, :]
            o = acc[:, :t] * inv_l[:, :t] - lam * (acc[:, t:] * inv_l[:, t:])
            ms = jnp.mean(o * o, axis=0, keepdims=True)
            o = o * lax.rsqrt(ms + RMS_EPS) * gn_ref[...] * (1.0 - lam_init)
            y_ref[:, h * DIFF_DV:(h + 1) * DIFF_DV] = o.T.astype(BF16)

    def drain(with_quad):
        if with_quad:
            far_pair(first_oct + 8 * n_octs)
            far_pair(first_oct + 8 * n_octs + 2)
        logits(i, s0, diag_cols)
        softmax_pv(i - 1, s1, near_cols)
        softmax_pv(i, s0, None)
        finish()

    for with_quad in (False, True):
        pl.when((i >= 1) & (peeled_quad == int(with_quad)))(functools.partial(drain, with_quad))

    @pl.when(i == 0)
    def _():
        softmax_pv(i, s0, None)
        finish()


def _diff(layer, proj, lam, bias, norm_gain_col, lam_init, batch, seq):
    n = proj.shape[0]
    nq = seq // T_ATT
    hw = DIFF_HEADS * LANES
    return pl.pallas_call(
        functools.partial(_diff_kernel, lam_init=lam_init),
        out_shape=jax.ShapeDtypeStruct((n, DIFF_HEADS * DIFF_DV), BF16),
        grid=(batch, nq),
        in_specs=[
            _layer_spec(layer, (4, DIFF_DK)),
            pl.BlockSpec((T_ATT, hw), lambda b, i: (b * nq + i, C_DQ // hw)),
            pl.BlockSpec((seq, hw), lambda b, i: (b, C_DK // hw)),
            pl.BlockSpec((seq, hw), lambda b, i: (b, C_DV // hw)),
            _const_spec((DIFF_HEADS, T_ATT, 2 * T_ATT)),
            _layer_spec(layer, (DIFF_DV, 1)),
        ],
        out_specs=pl.BlockSpec((T_ATT, DIFF_HEADS * DIFF_DV), lambda b, i: (b * nq + i, 0)),
        scratch_shapes=[
            pltpu.VMEM((DIFF_HEADS, 2 * T_ATT, LANES), BF16),
            pltpu.VMEM((DIFF_HEADS, nq, DIFF_DV + ONES_ROWS, T_ATT), BF16),
            pltpu.VMEM((DIFF_HEADS, T_ATT, 2 * T_ATT), F32),
            pltpu.VMEM((DIFF_HEADS, T_ATT, 2 * T_ATT), F32),
            pltpu.VMEM((DIFF_HEADS, 1, 2 * T_ATT), F32),
            pltpu.VMEM((DIFF_HEADS, 1, 2 * T_ATT), F32),
            pltpu.VMEM((DIFF_HEADS, 1, 2 * T_ATT), F32),
            pltpu.VMEM((DIFF_HEADS, DIFF_DV + ONES_ROWS, 2 * T_ATT), F32),
        ],
        compiler_params=pltpu.CompilerParams(
            dimension_semantics=("parallel", "arbitrary"), vmem_limit_bytes=VMEM_LIMIT),
        name="diff_attn",
    )(lam, proj, proj, proj, bias, norm_gain_col)


def _t5_bucket(rel):
    n = jnp.maximum(rel, 0)
    max_exact = REL_BUCKETS // 2
    nf = jnp.maximum(n, 1).astype(F32)
    large = max_exact + (jnp.log(nf / max_exact) / math.log(REL_MAX_DIST / max_exact)
                         * (REL_BUCKETS - max_exact)).astype(jnp.int32)
    large = jnp.minimum(large, REL_BUCKETS - 1)
    return jnp.where(n < max_exact, n, large)


def _bias_tiles(rel_bias):
    assert T_ATT + 1 >= REL_MAX_DIST
    t = T_ATT
    table = rel_bias.astype(F32)
    near = (table[_t5_bucket(jnp.arange(2 * t, dtype=jnp.int32))] - table[REL_BUCKETS - 1]) * LOG2E
    by_rel = jnp.concatenate([near, jnp.full((t, DIFF_HEADS), -jnp.inf, F32)], axis=0).T
    period = 3 * t
    flat = jnp.tile(by_rel, (1, t))[:, :t * (period - 1)]
    return flat.reshape(DIFF_HEADS, t, period - 1)[:, :, :2 * t]


def _merge_kernel(x_ref, g0_ref, g1_ref, g2_ref, u_ref, halo_ref, ygla_ref, ydiff_ref,
                  pw_ref, ps_ref, wb_ref, wo_ref, gpost_ref, o_ref, *, tiles_per_seq):
    i = pl.program_id(0)
    tm = TM_MERGE
    t_in_seq = i % tiles_per_seq
    halo = jnp.where(t_in_seq == 0, 0.0, halo_ref[...].astype(F32))
    ucat = jnp.concatenate([halo, u_ref[...].astype(F32)], axis=0)
    pos = t_in_seq * tm + lax.broadcasted_iota(jnp.int32, (tm, 1), 0)
    mixed = []
    for g, w in enumerate(POOL_WINDOWS):
        cols = slice(g * POOL_GROUP_DIM, (g + 1) * POOL_GROUP_DIM)
        win = ucat[:, cols]
        span = 1
        while span < w:
            win = win + pltpu.roll(win, span, axis=0)
            span *= 2
        cur = ucat[POOL_HALO:, cols]
        win = win[POOL_HALO:, :]
        cnt = jnp.minimum(pos + 1, w).astype(F32)
        pooled = win / cnt - cur
        mixed.append(jnp.dot(pooled.astype(BF16), pw_ref[g], preferred_element_type=F32))
    y_pool = (jnp.concatenate(mixed, axis=1) * ps_ref[...]).astype(BF16)

    merged = jax.nn.sigmoid(g0_ref[...].astype(F32)) * jnp.dot(y_pool, wb_ref[0], preferred_element_type=F32)
    merged = merged + jax.nn.sigmoid(g1_ref[...].astype(F32)) * jnp.dot(
        ygla_ref[...], wb_ref[1], preferred_element_type=F32)
    merged = merged + jax.nn.sigmoid(g2_ref[...].astype(F32)) * jnp.dot(
        ydiff_ref[...], wb_ref[2], preferred_element_type=F32)
    out = jnp.dot(merged.astype(BF16), wo_ref[...], preferred_element_type=F32)
    o_ref[...] = x_ref[...] + _rms(out, gpost_ref[...])


def _merge(layer, x2, proj, y_gla, y_diff, pool_w, pool_scale, w_branch, w_o, gpost, seq):
    n = x2.shape[0]
    tm = TM_MERGE
    halo_blocks = tm // POOL_HALO
    return pl.pallas_call(
        functools.partial(_merge_kernel, tiles_per_seq=seq // tm),
        out_shape=jax.ShapeDtypeStruct((n, D_MODEL), F32),
        grid=(n // tm,),
        in_specs=[
            pl.BlockSpec((tm, D_MODEL), lambda i: (i, 0)),
            pl.BlockSpec((tm, D_MODEL), lambda i: (i, C_GATE // D_MODEL + 0)),
            pl.BlockSpec((tm, D_MODEL), lambda i: (i, C_GATE // D_MODEL + 1)),
            pl.BlockSpec((tm, D_MODEL), lambda i: (i, C_GATE // D_MODEL + 2)),
            pl.BlockSpec((tm, POOL_DIM), lambda i: (i, C_U // POOL_DIM)),
            pl.BlockSpec((POOL_HALO, POOL_DIM),
                         lambda i: (jnp.maximum(i * halo_blocks - 1, 0), C_U // POOL_DIM)),
            pl.BlockSpec((tm, BRANCH_DIM), lambda i: (i, 0)),
            pl.BlockSpec((tm, BRANCH_DIM), lambda i: (i, 0)),
            _layer_spec(layer, (len(POOL_WINDOWS), POOL_GROUP_DIM, POOL_GROUP_DIM)),
            _layer_spec(layer, (1, POOL_DIM)),
            _layer_spec(layer, (3, BRANCH_DIM, D_MODEL)),
            _layer_spec(layer, (D_MODEL, D_MODEL)),
            _layer_spec(layer, (1, D_MODEL)),
        ],
        out_specs=pl.BlockSpec((tm, D_MODEL), lambda i: (i, 0)),
        compiler_params=pltpu.CompilerParams(
            dimension_semantics=("parallel",), vmem_limit_bytes=VMEM_LIMIT),
        name="merge",
    )(x2, proj, proj, proj, proj, proj, y_gla, y_diff, pool_w, pool_scale, w_branch, w_o, gpost)


def _ffn_kernel(x_ref, gpre_ref, wg_ref, wu_ref, wd_ref, gpost_ref, o_ref):
    x = x_ref[...]
    h = _rms(x, gpre_ref[...]).astype(BF16)
    g = jnp.dot(h, wg_ref[...], preferred_element_type=F32)
    u = jnp.dot(h, wu_ref[...], preferred_element_type=F32)
    a = (g * jax.nn.sigmoid(g) * u).astype(BF16)
    f = jnp.dot(a, wd_ref[...], preferred_element_type=F32)
    o_ref[...] = x + _rms(f, gpost_ref[...])


def _ffn(layer, x2, gpre, wg, wu, wd, gpost):
    n = x2.shape[0]
    tm = TM_FFN
    return pl.pallas_call(
        _ffn_kernel,
        out_shape=jax.ShapeDtypeStruct((n, D_MODEL), F32),
        grid=(n // tm,),
        in_specs=[
            pl.BlockSpec((tm, D_MODEL), lambda i: (i, 0)),
            _layer_spec(layer, (1, D_MODEL)),
            _layer_spec(layer, (D_MODEL, FFN_DIM)),
            _layer_spec(layer, (D_MODEL, FFN_DIM)),
            _layer_spec(layer, (FFN_DIM, D_MODEL)),
            _layer_spec(layer, (1, D_MODEL)),
        ],
        out_specs=pl.BlockSpec((tm, D_MODEL), lambda i: (i, 0)),
        compiler_params=pltpu.CompilerParams(
            dimension_semantics=("parallel",), vmem_limit_bytes=VMEM_LIMIT),
        name="ffn",
    )(x2, gpre, wg, wu, wd, gpost)


def _split_w_in(w):
    z0 = int(sum(IN_SPLITS[:5]))
    assert z0 == W_HEAD and w.shape[-1] - z0 - GLA_GATE_RANK == W_TAIL
    w = w.astype(BF16)
    w_z = jnp.pad(w[..., z0:z0 + GLA_GATE_RANK], ((0, 0), (0, 0), (0, Z_PAD - GLA_GATE_RANK)))
    return w[..., :z0], w_z, w[..., z0 + GLA_GATE_RANK:]


def kernel(x, rel_bias, ln_mix_pre, w_in, pool_w, pool_scale, gla_w_decay, gla_b_decay, gla_norm,
           diff_lambda, diff_norm, w_branch, w_o, ln_mix_post, ln_ffn_pre, ffn_w_gate, ffn_w_up,
           ffn_w_down, ln_ffn_post):
    batch, seq, d = x.shape
    depth = w_in.shape[0]
    assert POOL_HALO >= max(POOL_WINDOWS)
    assert d == D_MODEL and seq % TC_GLA == 0 and seq % T_ATT == 0 and seq % TM_MERGE == 0
    x2 = x.reshape(batch * seq, d)
    bias = _bias_tiles(rel_bias)
    row = lambda p: p[:, None, :]
    w_head, w_z, w_tail = _split_w_in(w_in)
    w_decay = jnp.pad(gla_w_decay, ((0, 0), (0, Z_PAD - GLA_GATE_RANK), (0, 0))).astype(BF16)
    pool_w, w_branch, w_o = pool_w.astype(BF16), w_branch.astype(BF16), w_o.astype(BF16)
    ffn_w_gate, ffn_w_up, ffn_w_down = ffn_w_gate.astype(BF16), ffn_w_up.astype(BF16), ffn_w_down.astype(BF16)
    for l in range(depth):
        lam_init = 0.8 - 0.6 * math.exp(-0.3 * l)
        proj = _in_proj(l, x2, row(ln_mix_pre), w_head, w_z, w_tail)
        y_gla = _gla(l, proj, w_decay, row(gla_b_decay), row(gla_norm), batch, seq)
        y_diff = _diff(l, proj, diff_lambda, bias, diff_norm[:, :, None], lam_init, batch, seq)
        x2 = _merge(l, x2, proj, y_gla, y_diff, pool_w, row(pool_scale), w_branch, w_o, row(ln_mix_post), seq)
        x2 = _ffn(l, x2, row(ln_ffn_pre), ffn_w_gate, ffn_w_up, ffn_w_down, row(ln_ffn_post))
    return x2.reshape(batch, seq, d)
```
